```python
import jax, jax.numpy as jnp
from jax import lax
import numpy as np

D_MODEL = 1024
BATCH = 16
SEQ = 2048
DEPTH = 2

HEAD_DIM = 64
N_HEADS_SB = 8
N_HEADS_RET = 8
N_HEADS_SWA = 16
N_KV_SWA = 4
WINDOW = 128
Q_BLOCK = 128
RET_CHUNK = 128
ROPE_THETA = 10000.0
N_GROUPS = 4
EXPERTS_PER_GROUP = 8
N_EXPERTS = N_GROUPS * EXPERTS_PER_GROUP
TOP_K_IN_GROUP = 2
D_EXPERT = 512
MOE_BLOCK = 256
EPS = 1e-6

W_SB = N_HEADS_SB * HEAD_DIM
W_RET = N_HEADS_RET * HEAD_DIM
IN_AB = 3 * W_SB + 4 * W_RET
OUT_AB = W_SB + W_RET
W_Q_SWA = N_HEADS_SWA * HEAD_DIM
W_KV_SWA = N_KV_SWA * HEAD_DIM
IN_C = W_Q_SWA + 2 * W_KV_SWA
N_EVEN = (DEPTH + 1) // 2
N_ODD = DEPTH // 2

kernel_name = 'hybrid_sb_retnet_swa_hmoe'


def rms_norm(x, g):
    xf = x.astype(jnp.float32)
    y = xf * lax.rsqrt(jnp.mean(jnp.square(xf), axis=-1, keepdims=True) + EPS)
    return (y * g.astype(jnp.float32)).astype(x.dtype)


def rope_tables(seq):
    pos = jnp.arange(seq, dtype=jnp.float32)
    inv = ROPE_THETA ** (-jnp.arange(0, HEAD_DIM, 2, dtype=jnp.float32) / HEAD_DIM)
    ang = pos[:, None] * inv[None, :]
    return jnp.cos(ang), jnp.sin(ang)


def apply_rope(x, cos, sin):
    half = x.shape[-1] // 2
    xf = x.astype(jnp.float32)
    x1, x2 = xf[..., :half], xf[..., half:]
    c = cos[None, :, None, :]
    s = sin[None, :, None, :]
    return jnp.concatenate([x1 * c - x2 * s, x2 * c + x1 * s], axis=-1).astype(x.dtype)


def stick_breaking_attention(q, k, v):
    B, S, H, d = q.shape
    scale = d ** -0.5
    outs = []
    for start in range(0, S, Q_BLOCK):
        end = start + Q_BLOCK
        qb = q[:, start:end]
        kb = k[:, :end]
        vb = v[:, :end]
        z = jnp.einsum('bqhd,bkhd->bhqk', qb, kb).astype(jnp.float32) * scale
        qpos = start + jnp.arange(Q_BLOCK)
        kpos = jnp.arange(end)
        strict = kpos[None, :] < qpos[:, None]
        log_fail = jnp.where(strict, jax.nn.log_sigmoid(-z), 0.0)
        after = lax.cumsum(log_fail, axis=3, reverse=True) - log_fail
        w = jnp.where(strict, jnp.exp(jax.nn.log_sigmoid(z) + after), 0.0)
        outs.append(jnp.einsum('bhqk,bkhd->bqhd', w.astype(vb.dtype), vb))
    return jnp.concatenate(outs, axis=1)


def retention_chunkwise(q, k, v):
    B, S, H, d = q.shape
    C = RET_CHUNK
    n = S // C
    q = q.astype(jnp.float32)
    k = k.astype(jnp.float32) * (d ** -0.5)
    v = v.astype(jnp.float32)
    log_g = jnp.log(1.0 - 2.0 ** (-5.0 - jnp.arange(H, dtype=jnp.float32)))
    idx = jnp.arange(C, dtype=jnp.float32)
    diff = idx[:, None] - idx[None, :]
    decay_in = jnp.where(diff[None] >= 0, jnp.exp(log_g[:, None, None] * jnp.maximum(diff, 0.0)[None]), 0.0)
    xi = jnp.exp(log_g[:, None] * (idx[None, :] + 1.0))
    zeta = jnp.exp(log_g[:, None] * (C - 1.0 - idx[None, :]))
    g_chunk = jnp.exp(log_g * C)

    def to_chunks(t):
        return t.reshape(B, n, C, H, d).transpose(1, 0, 3, 2, 4)

    def step(R, inp):
        qi, ki, vi = inp
        inner = jnp.einsum('bhqd,bhkd->bhqk', qi, ki) * decay_in[None]
        o = jnp.einsum('bhqk,bhke->bhqe', inner, vi) + jnp.einsum('bhqd,bhde->bhqe', qi, R) * xi[None, :, :, None]
        R = R * g_chunk[None, :, None, None] + jnp.einsum('bhkd,bhke->bhde', ki * zeta[None, :, :, None], vi)
        return R, o

    R0 = jnp.zeros((B, H, d, d), jnp.float32)
    _, o = lax.scan(step, R0, (to_chunks(q), to_chunks(k), to_chunks(v)))
    return o.transpose(1, 0, 3, 2, 4).reshape(B, S, H, d)


def head_group_norm(y, gain):
    B, S, H, d = y.shape
    mu = jnp.mean(y, axis=-1, keepdims=True)
    var = jnp.mean(jnp.square(y - mu), axis=-1, keepdims=True)
    yn = (y - mu) * lax.rsqrt(var + EPS)
    return yn.reshape(B, S, H * d) * gain.astype(jnp.float32)


def sliding_window_gqa_sinks(q, k, v, sinks):
    B, S, Hq, d = q.shape
    Hkv = k.shape[2]
    G = Hq // Hkv
    W = WINDOW
    nb = S // W
    qb = q.reshape(B, nb, W, Hkv, G, d)

    def band(t):
        tp = jnp.pad(t, ((0, 0), (W, 0), (0, 0), (0, 0))).reshape(B, nb + 1, W, Hkv, d)
        return jnp.concatenate([tp[:, :-1], tp[:, 1:]], axis=2)

    kband, vband = band(k), band(v)
    s = jnp.einsum('bnqhgd,bnkhd->bnhgqk', qb, kband).astype(jnp.float32) * (d ** -0.5)
    qi = jnp.arange(W)[:, None]
    kj = jnp.arange(2 * W)[None, :]
    rel = qi + W - kj
    local = (rel >= 0) & (rel < W)
    blk = jnp.arange(nb)[:, None, None]
    valid = local[None] & ((blk > 0) | (kj[None] >= W))
    s = jnp.where(valid[None, :, None, None], s, -jnp.inf)
    sink = sinks.astype(jnp.float32).reshape(Hkv, G)[None, None, :, :, None, None]
    m = jnp.maximum(jnp.max(s, axis=-1, keepdims=True), sink)
    p = jnp.exp(s - m)
    p = p / (jnp.sum(p, axis=-1, keepdims=True) + jnp.exp(sink - m))
    o = jnp.einsum('bnhgqk,bnkhd->bnqhgd', p.astype(vband.dtype), vband)
    return o.reshape(B, S, Hq, d)


def mixer_ab(hn, w_in, gn_gain, w_out, cos, sin):
    B, S, _ = hn.shape
    proj = hn @ w_in
    cuts = list(np.cumsum([W_SB, W_SB, W_SB, W_RET, W_RET, W_RET])[:])
    qa, ka, va, qr, kr, vr, gr = jnp.split(proj, cuts, axis=-1)
    heads = lambda t, h: t.reshape(B, S, h, HEAD_DIM)
    a = stick_breaking_attention(heads(qa, N_HEADS_SB), heads(ka, N_HEADS_SB), heads(va, N_HEADS_SB))
    r = retention_chunkwise(apply_rope(heads(qr, N_HEADS_RET), cos, sin),
                            apply_rope(heads(kr, N_HEADS_RET), cos, sin),
                            heads(vr, N_HEADS_RET))
    r = head_group_norm(r, gn_gain) * jax.nn.silu(gr.astype(jnp.float32))
    cat = jnp.concatenate([a.reshape(B, S, W_SB), r.astype(a.dtype)], axis=-1)
    return cat @ w_out


def mixer_c(hn, w_in, b_in, sinks, w_out, b_out, cos, sin):
    B, S, _ = hn.shape
    proj = hn @ w_in + b_in
    q, k, v = jnp.split(proj, [W_Q_SWA, W_Q_SWA + W_KV_SWA], axis=-1)
    q = apply_rope(q.reshape(B, S, N_HEADS_SWA, HEAD_DIM), cos, sin)
    k = apply_rope(k.reshape(B, S, N_KV_SWA, HEAD_DIM), cos, sin)
    v = v.reshape(B, S, N_KV_SWA, HEAD_DIM)
    o = sliding_window_gqa_sinks(q, k, v, sinks)
    return o.reshape(B, S, W_Q_SWA) @ w_out + b_out


def hierarchical_moe(x, w_group, w_expert_router, w_gate, w_up, w_down):
    B, S, D = x.shape
    N = B * S
    M = MOE_BLOCK
    xf = x.reshape(N, D)
    group_probs = jax.nn.softmax((xf @ w_group).astype(jnp.float32), axis=-1)
    g_prob, g_idx = lax.top_k(group_probs, 1)
    e_logits = (xf @ w_expert_router).astype(jnp.float32).reshape(N, N_GROUPS, EXPERTS_PER_GROUP)
    e_logits = e_logits[jnp.arange(N), g_idx[:, 0]]
    e_prob, e_idx = lax.top_k(jax.nn.softmax(e_logits, axis=-1), TOP_K_IN_GROUP)
    e_prob = e_prob / jnp.sum(e_prob, axis=-1, keepdims=True)
    gate = (g_prob * e_prob).reshape(-1)
    eid = (g_idx * EXPERTS_PER_GROUP + e_idx).reshape(-1).astype(jnp.int32)
    tok = jnp.repeat(jnp.arange(N, dtype=jnp.int32), TOP_K_IN_GROUP)
    P = N * TOP_K_IN_GROUP
    order = jnp.argsort(eid)
    s_eid, s_tok, s_w = eid[order], tok[order], gate[order]
    counts = jnp.bincount(eid, length=N_EXPERTS).astype(jnp.int32)
    padded = ((counts + M - 1) // M) * M
    start = jnp.cumsum(counts) - counts
    pend = jnp.cumsum(padded)
    pstart = pend - padded
    dest = pstart[s_eid] + jnp.arange(P, dtype=jnp.int32) - start[s_eid]
    R = P + N_EXPERTS * M
    nblk = R // M
    row_tok = jnp.zeros((R,), jnp.int32).at[dest].set(s_tok)
    row_w = jnp.zeros((R,), s_w.dtype).at[dest].set(s_w)
    blk_exp = jnp.minimum(jnp.searchsorted(pend, jnp.arange(nblk, dtype=jnp.int32) * M, side='right'), N_EXPERTS - 1)
    xs = xf[row_tok].reshape(nblk, M, D)

    def expert_block(args):
        xb, e = args
        h = jax.nn.silu(xb @ w_gate[e]) * (xb @ w_up[e])
        return h @ w_down[e]

    ys = lax.map(expert_block, (xs, blk_exp)).reshape(R, D)
    y = jnp.zeros((N, D), ys.dtype).at[row_tok].add(ys * row_w[:, None].astype(ys.dtype))
    return y.reshape(B, S, D)


def setup_inputs(seed: int = 0) -> dict:
    key = jax.random.key(seed)
    ks = jax.random.split(key, 20)
    f32 = jnp.float32
    nrm = lambda k, shape, fan_in: jax.random.normal(k, shape, f32) * (fan_in ** -0.5)
    return {
        'x': jax.random.normal(ks[0], (BATCH, SEQ, D_MODEL), f32),
        'attn_norm_g': 1.0 + 0.02 * jax.random.normal(ks[1], (DEPTH, D_MODEL), f32),
        'ffn_norm_g': 1.0 + 0.02 * jax.random.normal(ks[2], (DEPTH, D_MODEL), f32),
        'w_in_ab': nrm(ks[3], (N_EVEN, D_MODEL, IN_AB), D_MODEL),
        'w_out_ab': nrm(ks[4], (N_EVEN, OUT_AB, D_MODEL), OUT_AB),
        'ret_gn_g': 1.0 + 0.02 * jax.random.normal(ks[5], (N_EVEN, W_RET), f32),
        'w_in_c': nrm(ks[6], (N_ODD, D_MODEL, IN_C), D_MODEL),
        'b_in_c': 0.02 * jax.random.normal(ks[7], (N_ODD, IN_C), f32),
        'sinks': 0.5 * jax.random.normal(ks[8], (N_ODD, N_HEADS_SWA), f32),
        'w_out_c': nrm(ks[9], (N_ODD, W_Q_SWA, D_MODEL), W_Q_SWA),
        'b_out_c': 0.02 * jax.random.normal(ks[10], (N_ODD, D_MODEL), f32),
        'w_group': nrm(ks[11], (DEPTH, D_MODEL, N_GROUPS), D_MODEL),
        'w_expert_router': nrm(ks[12], (DEPTH, D_MODEL, N_EXPERTS), D_MODEL),
        'w_gate': nrm(ks[13], (DEPTH, N_EXPERTS, D_MODEL, D_EXPERT), D_MODEL),
        'w_up': nrm(ks[14], (DEPTH, N_EXPERTS, D_MODEL, D_EXPERT), D_MODEL),
        'w_down': nrm(ks[15], (DEPTH, N_EXPERTS, D_EXPERT, D_MODEL), D_EXPERT),
        'final_norm_g': 1.0 + 0.02 * jax.random.normal(ks[16], (D_MODEL,), f32),
    }


def reference(x, attn_norm_g, ffn_norm_g, w_in_ab, w_out_ab, ret_gn_g, w_in_c, b_in_c, sinks,
              w_out_c, b_out_c, w_group, w_expert_router, w_gate, w_up, w_down, final_norm_g):
    S = x.shape[1]
    cos, sin = rope_tables(S)
    h = x
    for layer in range(DEPTH):
        i = layer // 2
        hn = rms_norm(h, attn_norm_g[layer])
        if layer % 2 == 0:
            mix = mixer_ab(hn, w_in_ab[i], ret_gn_g[i], w_out_ab[i], cos, sin)
        else:
            mix = mixer_c(hn, w_in_c[i], b_in_c[i], sinks[i], w_out_c[i], b_out_c[i], cos, sin)
        h = h + mix.astype(h.dtype)
        ff = hierarchical_moe(rms_norm(h, ffn_norm_g[layer]), w_group[layer], w_expert_router[layer],
                              w_gate[layer], w_up[layer], w_down[layer])
        h = h + ff.astype(h.dtype)
    return rms_norm(h, final_norm_g)
```

```python
import functools

import jax
import jax.numpy as jnp
import numpy as np
from jax import lax
from jax.experimental import pallas as pl
from jax.experimental.pallas import tpu as pltpu

F32 = jnp.float32
BF16 = jnp.bfloat16
I32 = jnp.int32

HEAD_DIM = 64
N_HEADS_SB = 8
N_HEADS_RET = 8
N_HEADS_SWA = 16
N_KV_SWA = 4
WINDOW = 128
ROPE_THETA = 10000.0
N_GROUPS = 4
EXPERTS_PER_GROUP = 8
N_EXPERTS = N_GROUPS * EXPERTS_PER_GROUP
MOE_BLOCK = 256
EPS = 1e-6

LANES = 128
HALF = HEAD_DIM // 2
QK_SCALE = HEAD_DIM ** -0.5

SB_TILE = 256
RET_CHUNK = 256
PROJ_TM = 512
TOK_TM = 256
VMEM_LIMIT = 48 * 1024 * 1024

_NT = (((1,), (1,)), ((), ()))
_TN = (((0,), (0,)), ((), ()))


def _cparams(*sem):
    return pltpu.CompilerParams(dimension_semantics=sem, vmem_limit_bytes=VMEM_LIMIT)


def _head0_mask():
    return lax.broadcasted_iota(I32, (1, LANES), 1) < HEAD_DIM


def _norm_proj_kernel(x_ref, g_ref, w_ref, b_ref, cos_ref, sin_ref, o_ref, *, tn, col_ops):
    x = x_ref[...]
    ms = jnp.mean(x * x, axis=-1, keepdims=True)
    xn = (x * lax.rsqrt(ms + EPS) * g_ref[...]).astype(BF16)
    lane = lax.broadcasted_iota(I32, (1, LANES), 1)
    first_half = (lane % HEAD_DIM) < HALF
    for j, op in enumerate(col_ops):
        cols = slice(j * tn, (j + 1) * tn)
        acc = jnp.dot(xn, w_ref[:, cols], preferred_element_type=F32) + b_ref[:, cols]
        if "rope" in op:
            cos = cos_ref[...]
            sin = sin_ref[...]
            slabs = []
            for s in range(tn // LANES):
                a = acc[:, s * LANES:(s + 1) * LANES]
                partner = jnp.where(first_half, pltpu.roll(a, LANES - HALF, 1), pltpu.roll(a, HALF, 1))
                slabs.append(a * cos + partner * sin)
            acc = jnp.concatenate(slabs, axis=1)
        if "scale" in op:
            acc = acc * QK_SCALE
        o_ref[:, cols] = acc.astype(BF16)


def norm_proj(x, g, w, b, cos, sin, *, seq, tn, col_ops):
    n, d = x.shape
    f = w.shape[1]
    tm = min(PROJ_TM, seq)
    assert n % tm == 0 and seq % tm == 0 and f == tn * len(col_ops)
    pos_blocks = seq // tm
    return pl.pallas_call(
        functools.partial(_norm_proj_kernel, tn=tn, col_ops=col_ops),
        out_shape=jax.ShapeDtypeStruct((n, f), BF16),
        grid=(n // tm,),
        in_specs=[
            pl.BlockSpec((tm, d), lambda i: (i, 0)),
            pl.BlockSpec((1, d), lambda i: (0, 0)),
            pl.BlockSpec((d, f), lambda i: (0, 0)),
            pl.BlockSpec((1, f), lambda i: (0, 0)),
            pl.BlockSpec((tm, LANES), lambda i: (i % pos_blocks, 0)),
            pl.BlockSpec((tm, LANES), lambda i: (i % pos_blocks, 0)),
        ],
        out_specs=pl.BlockSpec((tm, f), lambda i: (i, 0)),
        compiler_params=_cparams("arbitrary"),
        name="norm_proj",
    )(x, g.reshape(1, d), w, b.reshape(1, f), cos, sin)


def rope_lane_tables(seq):
    pos = jnp.arange(seq, dtype=F32)
    inv = ROPE_THETA ** (-jnp.arange(0, HEAD_DIM, 2, dtype=F32) / HEAD_DIM)
    ang = pos[:, None] * inv[None, :]
    cos, sin = jnp.cos(ang), jnp.sin(ang)
    return jnp.tile(cos, (1, 4)), jnp.tile(jnp.concatenate([-sin, sin], axis=1), (1, 2))


def _sb_kernel(q_ref, k_ref, v_ref, t_ref, o_ref, acc_ref, c0_ref, c1_ref, *, tile):
    qi = pl.program_id(2)
    m0 = _head0_mask()
    q = q_ref[...]
    zero = jnp.zeros_like(q)
    q_heads = (jnp.where(m0, q, zero), jnp.where(m0, zero, q))
    carries = (c0_ref, c1_ref)
    row = lax.broadcasted_iota(I32, (tile, tile), 0)
    col = lax.broadcasted_iota(I32, (tile, tile), 1)
    strict = col < row

    def key_tile(kb, diag):
        start = pl.multiple_of(kb * tile, tile)
        k = k_ref[pl.ds(start, tile), :]
        v = v_ref[pl.ds(start, tile), :]
        v_heads = (jnp.where(m0, v, zero), jnp.where(m0, zero, v))
        out = None
        for h in range(2):
            z = lax.dot_general(q_heads[h], k, _NT, preferred_element_type=F32)
            log_fail = -(jnp.maximum(z, 0.0) + jnp.log(1.0 + jnp.exp(-jnp.abs(z))))
            if diag:
                log_fail = jnp.where(strict, log_fail, 0.0)
            hi = log_fail.astype(BF16)
            lo = (log_fail - hi.astype(F32)).astype(BF16)
            after = (jnp.dot(hi, t_ref[...], preferred_element_type=F32)
                     + jnp.dot(lo, t_ref[...], preferred_element_type=F32))
            logw = z + log_fail + after
            if not diag:
                logw = logw + carries[h][...]
            w = jnp.exp(logw)
            if diag:
                w = jnp.where(strict, w, 0.0)
            pv = jnp.dot(w.astype(BF16), v_heads[h], preferred_element_type=F32)
            out = pv if out is None else out + pv
            tile_sum = jnp.sum(log_fail, axis=-1, keepdims=True)
            if diag:
                carries[h][...] = tile_sum
            else:
                carries[h][...] += tile_sum
        if diag:
            acc_ref[...] = out
        else:
            acc_ref[...] += out

    key_tile(qi, True)

    def body(j, c):
        key_tile(qi - 1 - j, False)
        return c

    lax.fori_loop(0, qi, body, 0)
    o_ref[...] = acc_ref[...].astype(BF16)


def sb_attention(proj, later_mat, *, batch, seq, q_blk, k_blk, v_blk):
    n = proj.shape[0]
    tile = min(SB_TILE, seq)
    nq = seq // tile
    pairs = N_HEADS_SB * HEAD_DIM // LANES
    return pl.pallas_call(
        functools.partial(_sb_kernel, tile=tile),
        out_shape=jax.ShapeDtypeStruct((n, pairs * LANES), BF16),
        grid=(batch, pairs, nq),
        in_specs=[
            pl.BlockSpec((tile, LANES), lambda b, p, i: (b * nq + i, q_blk + p)),
            pl.BlockSpec((seq, LANES), lambda b, p, i: (b, k_blk + p)),
            pl.BlockSpec((seq, LANES), lambda b, p, i: (b, v_blk + p)),
            pl.BlockSpec((tile, tile), lambda b, p, i: (0, 0)),
        ],
        out_specs=pl.BlockSpec((tile, LANES), lambda b, p, i: (b * nq + i, p)),
        scratch_shapes=[pltpu.VMEM((tile, LANES), F32), pltpu.VMEM((tile, 1), F32), pltpu.VMEM((tile, 1), F32)],
        compiler_params=_cparams("arbitrary", "arbitrary", "arbitrary"),
        name="sb_attention",
    )(proj, proj, proj, later_mat)


def _retention_kernel(q_ref, k_ref, v_ref, gate_ref, gain_ref, dec_ref, xi_ref, zeta_ref, gch_ref,
                      o_ref, state_ref):
    c = pl.program_id(2)

    @pl.when(c == 0)
    def _():
        state_ref[...] = jnp.zeros_like(state_ref)

    m0 = _head0_mask()
    q = q_ref[...]
    k = k_ref[...]
    v = v_ref[...]
    zero = jnp.zeros_like(q)
    state = state_ref[...]
    y = jnp.dot(q, state.astype(BF16), preferred_element_type=F32) * xi_ref[...]
    for h in range(2):
        qh = jnp.where(m0, q, zero) if h == 0 else jnp.where(m0, zero, q)
        vh = jnp.where(m0, v, zero) if h == 0 else jnp.where(m0, zero, v)
        s = lax.dot_general(qh, k, _NT, preferred_element_type=F32)
        inner = (s * dec_ref[h]).astype(BF16)
        y = y + jnp.dot(inner, vh, preferred_element_type=F32)

    kz = (k.astype(F32) * zeta_ref[...]).astype(BF16)
    upd = lax.dot_general(kz, v, _TN, preferred_element_type=F32)
    r = lax.broadcasted_iota(I32, (LANES, LANES), 0) < HEAD_DIM
    cc = lax.broadcasted_iota(I32, (LANES, LANES), 1) < HEAD_DIM
    state_ref[...] = state * gch_ref[...] + jnp.where(r == cc, upd, 0.0)

    def head_mean(t):
        s0 = jnp.sum(jnp.where(m0, t, 0.0), axis=-1, keepdims=True)
        s1 = jnp.sum(jnp.where(m0, 0.0, t), axis=-1, keepdims=True)
        return jnp.where(m0, s0, s1) * (1.0 / HEAD_DIM)

    d = y - head_mean(y)
    yn = d * lax.rsqrt(head_mean(d * d) + EPS) * gain_ref[...]
    g = gate_ref[...].astype(F32)
    o_ref[...] = (yn * (g * (1.0 / (1.0 + jnp.exp(-g))))).astype(BF16)


def retention_tables(chunk):
    h = N_HEADS_RET
    log_g = jnp.log(1.0 - 2.0 ** (-5.0 - jnp.arange(h, dtype=F32)))
    idx = jnp.arange(chunk, dtype=F32)
    diff = idx[:, None] - idx[None, :]
    dec = jnp.where(diff[None] >= 0, jnp.exp(log_g[:, None, None] * jnp.maximum(diff, 0.0)[None]), 0.0)
    xi = jnp.exp(log_g[:, None] * (idx[None, :] + 1.0))
    zeta = jnp.exp(log_g[:, None] * (chunk - 1.0 - idx[None, :]))
    gch = jnp.exp(log_g * chunk)
    lanes = lambda t: jnp.repeat(t.T, HEAD_DIM, axis=1)
    return dec, lanes(xi), lanes(zeta), jnp.repeat(gch, HEAD_DIM)[None, :]


def retention(proj, gain, *, batch, seq, q_blk, k_blk, v_blk, g_blk):
    n = proj.shape[0]
    chunk = min(RET_CHUNK, seq)
    nc = seq // chunk
    pairs = N_HEADS_RET * HEAD_DIM // LANES
    dec, xi, zeta, gch = retention_tables(chunk)
    blk = lambda off: pl.BlockSpec((chunk, LANES), lambda b, p, c: (b * nc + c, off + p))
    return pl.pallas_call(
        _retention_kernel,
        out_shape=jax.ShapeDtypeStruct((n, pairs * LANES), BF16),
        grid=(batch, pairs, nc),
        in_specs=[
            blk(q_blk), blk(k_blk), blk(v_blk), blk(g_blk),
            pl.BlockSpec((1, LANES), lambda b, p, c: (0, p)),
            pl.BlockSpec((2, chunk, chunk), lambda b, p, c: (p, 0, 0)),
            pl.BlockSpec((chunk, LANES), lambda b, p, c: (0, p)),
            pl.BlockSpec((chunk, LANES), lambda b, p, c: (0, p)),
            pl.BlockSpec((1, LANES), lambda b, p, c: (0, p)),
        ],
        out_specs=pl.BlockSpec((chunk, LANES), lambda b, p, c: (b * nc + c, p)),
        scratch_shapes=[pltpu.VMEM((LANES, LANES), F32)],
        compiler_params=_cparams("arbitrary", "arbitrary", "arbitrary"),
        name="retention",
    )(proj, proj, proj, proj, gain.reshape(1, -1).astype(F32), dec, xi, zeta, gch)


def _swa_kernel(sink_ref, q_ref, kp_ref, kc_ref, vp_ref, vc_ref, o_ref):
    blk = pl.program_id(1)
    m0 = _head0_mask()
    w = WINDOW
    qi = lax.broadcasted_iota(I32, (w, 2 * w), 0)
    kj = lax.broadcasted_iota(I32, (w, 2 * w), 1)
    rel = qi + w - kj
    valid = (rel >= 0) & (rel < w) & ((blk > 0) | (kj >= w))
    kband = jnp.concatenate([kp_ref[...], kc_ref[...]], axis=0)
    vband = jnp.concatenate([vp_ref[...], vc_ref[...]], axis=0)
    group = N_HEADS_SWA // N_KV_SWA
    swap = lambda t: jnp.concatenate([t[:, HEAD_DIM:], t[:, :HEAD_DIM]], axis=1)
    zero_kv = jnp.zeros((2 * w, LANES), BF16)
    zero_q = jnp.zeros((w, LANES), BF16)

    def softmax_sink(s, head):
        sink = sink_ref[head]
        s = jnp.where(valid, s, -jnp.inf)
        m = jnp.maximum(jnp.max(s, axis=-1, keepdims=True), sink)
        p = jnp.exp(s - m)
        return p / (jnp.sum(p, axis=-1, keepdims=True) + jnp.exp(sink - m))

    for pair in range(N_KV_SWA * HEAD_DIM // LANES):
        k2 = kband[:, pair * LANES:(pair + 1) * LANES]
        v2 = vband[:, pair * LANES:(pair + 1) * LANES]
        k2s, v2s = swap(k2), swap(v2)
        for c in range(2):
            kvh = 2 * pair + c
            k_even, k_odd = (k2, k2s) if c == 0 else (k2s, k2)
            v_even, v_odd = (v2, v2s) if c == 0 else (v2s, v2)
            v_even = jnp.where(m0, v_even, zero_kv)
            v_odd = jnp.where(m0, zero_kv, v_odd)
            for slab in range(group * HEAD_DIM // LANES):
                cols = slice(kvh * group * HEAD_DIM + slab * LANES, kvh * group * HEAD_DIM + (slab + 1) * LANES)
                a = q_ref[:, cols]
                head = kvh * group + 2 * slab
                s_even = lax.dot_general(jnp.where(m0, a, zero_q), k_even, _NT, preferred_element_type=F32)
                s_odd = lax.dot_general(jnp.where(m0, zero_q, a), k_odd, _NT, preferred_element_type=F32)
                p_even = softmax_sink(s_even, head).astype(BF16)
                p_odd = softmax_sink(s_odd, head + 1).astype(BF16)
                out = (jnp.dot(p_even, v_even, preferred_element_type=F32)
                       + jnp.dot(p_odd, v_odd, preferred_element_type=F32))
                o_ref[:, cols] = out.astype(BF16)


def swa_attention(proj, sinks, *, batch, seq):
    n = proj.shape[0]
    w = WINDOW
    nb = seq // w
    wq = N_HEADS_SWA * HEAD_DIM
    wkv = N_KV_SWA * HEAD_DIM
    k_blk = wq // wkv
    cur = lambda off: pl.BlockSpec((w, wkv), lambda b, i: (b * nb + i, off))
    prev = lambda off: pl.BlockSpec((w, wkv), lambda b, i: (b * nb + jnp.maximum(i - 1, 0), off))
    return pl.pallas_call(
        _swa_kernel,
        out_shape=jax.ShapeDtypeStruct((n, wq), BF16),
        grid=(batch, nb),
        in_specs=[
            pl.BlockSpec(memory_space=pltpu.SMEM),
            pl.BlockSpec((w, wq), lambda b, i: (b * nb + i, 0)),
            prev(k_blk), cur(k_blk), prev(k_blk + 1), cur(k_blk + 1),
        ],
        out_specs=pl.BlockSpec((w, wq), lambda b, i: (b * nb + i, 0)),
        compiler_params=_cparams("arbitrary", "arbitrary"),
        name="swa_attention",
    )(sinks.astype(F32), proj, proj, proj, proj, proj)


ROUTE_EID = 0
ROUTE_RANK = 2
GROUP_LANE = N_EXPERTS


def _split_bf16(t):
    hi = t.astype(BF16)
    return hi, (t - hi.astype(F32)).astype(BF16)


def _outproj_router_kernel(*refs, n_lhs):
    lhs = refs[:n_lhs]
    ws = refs[n_lhs:2 * n_lhs]
    b_ref, h_ref, g_ref, wr_ref, before_ref = refs[2 * n_lhs:2 * n_lhs + 5]
    h1_ref, xn_ref, route_ref, gate_ref, cnt_ref, carry_ref = refs[2 * n_lhs + 5:]
    step = pl.program_id(0)

    @pl.when(step == 0)
    def _():
        carry_ref[...] = jnp.zeros_like(carry_ref)

    mix = b_ref[...]
    for a_ref, w_ref in zip(lhs, ws):
        mix = mix + jnp.dot(a_ref[...], w_ref[...], preferred_element_type=F32)
    h1 = h_ref[...] + mix
    h1_ref[...] = h1
    ms = jnp.mean(h1 * h1, axis=-1, keepdims=True)
    xn = h1 * lax.rsqrt(ms + EPS) * g_ref[...]
    xn_ref[...] = xn

    x_hi, x_lo = _split_bf16(xn)
    w_hi, w_lo = _split_bf16(wr_ref[...])
    logits = (jnp.dot(x_hi, w_hi, preferred_element_type=F32)
              + jnp.dot(x_lo, w_hi, preferred_element_type=F32)
              + jnp.dot(x_hi, w_lo, preferred_element_type=F32))
    tm = logits.shape[0]
    lane = lax.broadcasted_iota(I32, (tm, LANES), 1)
    lane_f = lane.astype(F32)
    neg = -jnp.inf

    def lane_max(t):
        return jnp.max(t, axis=-1, keepdims=True)

    def lane_sum(t):
        return jnp.sum(t, axis=-1, keepdims=True)

    def first_lane_of(t, value, mask):
        return jnp.min(jnp.where(mask & (t == value), lane_f, float(LANES)), axis=-1, keepdims=True)

    is_group = (lane >= GROUP_LANE) & (lane < GROUP_LANE + N_GROUPS)
    gl = jnp.where(is_group, logits, neg)
    ge = jnp.exp(gl - lane_max(gl))
    gp = ge / lane_sum(ge)
    g_prob = lane_max(gp)
    g_idx = first_lane_of(gp, g_prob, is_group) - float(GROUP_LANE)
    group_of_lane = lax.shift_right_logical(lane, int(np.log2(EXPERTS_PER_GROUP))).astype(F32)
    in_group = (lane < N_EXPERTS) & (group_of_lane == g_idx)
    el = jnp.where(in_group, logits, neg)
    ee = jnp.exp(el - lane_max(el))
    ep = ee / lane_sum(ee)
    p1 = lane_max(ep)
    i1 = first_lane_of(ep, p1, in_group)
    rest = in_group & (lane_f != i1)
    ep2 = jnp.where(rest, ep, neg)
    p2 = lane_max(ep2)
    i2 = first_lane_of(ep2, p2, rest)
    denom = p1 + p2
    gate1 = g_prob * (p1 / denom)
    gate2 = g_prob * (p2 / denom)

    oh1 = lane_f == i1
    oh2 = lane_f == i2
    cnt = jnp.where(oh1 | oh2, 1.0, 0.0)
    prefix = jnp.dot(before_ref[...], cnt.astype(BF16), preferred_element_type=F32) + carry_ref[...]
    r1 = lane_sum(jnp.where(oh1, prefix, 0.0))
    r2 = lane_sum(jnp.where(oh2, prefix, 0.0))
    carry_ref[...] += jnp.sum(cnt, axis=0, keepdims=True)
    cnt_ref[...] = jnp.broadcast_to(carry_ref[...], cnt_ref.shape).astype(I32)

    route = jnp.where(lane == ROUTE_EID, i1, 0.0)
    route = jnp.where(lane == ROUTE_EID + 1, i2, route)
    route = jnp.where(lane == ROUTE_RANK, r1, route)
    route = jnp.where(lane == ROUTE_RANK + 1, r2, route)
    route_ref[...] = route.astype(I32)
    gate_ref[...] = jnp.where(lane == 0, gate1, jnp.where(lane == 1, gate2, 0.0))


def outproj_router(lhs, ws, bias, h, g, w_router):
    n, d = h.shape
    tm = TOK_TM
    assert n % tm == 0
    before = (jnp.arange(tm)[None, :] < jnp.arange(tm)[:, None]).astype(BF16)
    row_blk = lambda width: pl.BlockSpec((tm, width), lambda i: (i, 0))
    full = lambda a: pl.BlockSpec(a.shape, lambda i: (0, 0))
    bias2, g2 = bias.reshape(1, d), g.reshape(1, d)
    args = [*lhs, *ws, bias2, h, g2, w_router, before]
    in_specs = ([row_blk(a.shape[1]) for a in lhs] + [full(w) for w in ws]
                + [full(bias2), row_blk(d), full(g2), full(w_router), full(before)])
    return pl.pallas_call(
        functools.partial(_outproj_router_kernel, n_lhs=len(lhs)),
        out_shape=(
            jax.ShapeDtypeStruct((n, d), F32), jax.ShapeDtypeStruct((n, d), F32),
            jax.ShapeDtypeStruct((n, LANES), I32), jax.ShapeDtypeStruct((n, LANES), F32),
            jax.ShapeDtypeStruct((8, LANES), I32),
        ),
        grid=(n // tm,),
        in_specs=in_specs,
        out_specs=(row_blk(d), row_blk(d), row_blk(LANES), row_blk(LANES),
                   pl.BlockSpec((8, LANES), lambda i: (0, 0))),
        scratch_shapes=[pltpu.VMEM((1, LANES), F32)],
        compiler_params=_cparams("arbitrary"),
        name="outproj_router",
    )(*args)


def router_weights(w_group, w_expert_router):
    d = w_group.shape[0]
    pad = jnp.zeros((d, LANES - N_EXPERTS - N_GROUPS), F32)
    return jnp.concatenate([w_expert_router.astype(F32), w_group.astype(F32), pad], axis=1)


def _row_copy(src, src_row, dst, dst_row, sem):
    return pltpu.make_async_copy(src.at[pl.ds(src_row, 1), :], dst.at[pl.ds(dst_row, 1), :], sem)


def _dispatch_kernel(dest_hbm, x_ref, xs_in, xs_hbm, dest_smem, idx_sem, row_sem, *, tm):
    del xs_in
    step = pl.program_id(0)
    idx_copy = pltpu.make_async_copy(dest_hbm.at[step], dest_smem, idx_sem)
    idx_copy.start()
    idx_copy.wait()

    def issue(t, c):
        for s in range(2):
            _row_copy(x_ref, t, xs_hbm, dest_smem[2 * t + s], row_sem).start()
        return c

    lax.fori_loop(0, tm, issue, 0)

    def drain(t, c):
        for s in range(2):
            _row_copy(x_ref, 0, xs_hbm, 0, row_sem).wait()
        return c

    lax.fori_loop(0, tm, drain, 0)


def moe_dispatch(xn, dest, rows):
    n, d = xn.shape
    tm = TOK_TM
    xs0 = jnp.zeros((rows, d), F32)
    return pl.pallas_call(
        functools.partial(_dispatch_kernel, tm=tm),
        out_shape=jax.ShapeDtypeStruct((rows, d), F32),
        grid=(n // tm,),
        in_specs=[pl.BlockSpec(memory_space=pl.ANY), pl.BlockSpec((tm, d), lambda i: (i, 0)),
                  pl.BlockSpec(memory_space=pl.ANY)],
        out_specs=pl.BlockSpec(memory_space=pl.ANY),
        input_output_aliases={2: 0},
        scratch_shapes=[pltpu.SMEM((2 * tm,), I32), pltpu.SemaphoreType.DMA, pltpu.SemaphoreType.DMA],
        compiler_params=_cparams("arbitrary"),
        name="moe_dispatch",
    )(dest, xn, xs0)


def _experts_kernel(blk_exp_ref, nused_ref, x_ref, wg_ref, wu_ref, wd_ref, y_ref, wg_bf, wu_bf, wd_bf):
    i = pl.program_id(0)
    fresh = (i == 0) | (blk_exp_ref[i] != blk_exp_ref[jnp.maximum(i - 1, 0)])

    @pl.when(fresh)
    def _():
        wg_bf[...] = wg_ref[0].astype(BF16)
        wu_bf[...] = wu_ref[0].astype(BF16)
        wd_bf[...] = wd_ref[0].astype(BF16)

    @pl.when(i < nused_ref[0])
    def _():
        x = x_ref[...].astype(BF16)
        gate = jnp.dot(x, wg_bf[...], preferred_element_type=F32)
        up = jnp.dot(x, wu_bf[...], preferred_element_type=F32)
        hidden = (gate * (1.0 / (1.0 + jnp.exp(-gate))) * up).astype(BF16)
        y_ref[...] = jnp.dot(hidden, wd_bf[...], preferred_element_type=F32)

    @pl.when(i >= nused_ref[0])
    def _():
        y_ref[...] = jnp.zeros_like(y_ref)


def moe_experts(xs, blk_exp, nused, w_gate, w_up, w_down):
    rows, d = xs.shape
    m = MOE_BLOCK
    de = w_gate.shape[2]
    last = lambda i, be, nu: jnp.minimum(i, nu[0] - 1)
    return pl.pallas_call(
        _experts_kernel,
        out_shape=jax.ShapeDtypeStruct((rows, d), F32),
        grid_spec=pltpu.PrefetchScalarGridSpec(
            num_scalar_prefetch=2,
            grid=(rows // m,),
            in_specs=[
                pl.BlockSpec((m, d), lambda i, be, nu: (last(i, be, nu), 0)),
                pl.BlockSpec((1, d, de), lambda i, be, nu: (be[i], 0, 0)),
                pl.BlockSpec((1, d, de), lambda i, be, nu: (be[i], 0, 0)),
                pl.BlockSpec((1, de, d), lambda i, be, nu: (be[i], 0, 0)),
            ],
            out_specs=pl.BlockSpec((m, d), lambda i, be, nu: (i, 0)),
            scratch_shapes=[pltpu.VMEM((d, de), BF16), pltpu.VMEM((d, de), BF16), pltpu.VMEM((de, d), BF16)],
        ),
        compiler_params=_cparams("arbitrary"),
        name="moe_experts",
    )(blk_exp, nused, xs, w_gate, w_up, w_down)


def _combine_kernel(dest_hbm, gate_ref, h_ref, g_ref, ys_hbm, o_ref, dest_smem, buf, idx_sem, row_sem,
                    *, tm, final_norm):
    step = pl.program_id(0)
    idx_copy = pltpu.make_async_copy(dest_hbm.at[step], dest_smem, idx_sem)
    idx_copy.start()
    idx_copy.wait()

    def issue(t, c):
        for s in range(2):
            _row_copy(ys_hbm, dest_smem[2 * t + s], buf.at[s], t, row_sem).start()
        return c

    lax.fori_loop(0, tm, issue, 0)

    def drain(t, c):
        for s in range(2):
            _row_copy(ys_hbm, 0, buf.at[s], 0, row_sem).wait()
        return c

    lax.fori_loop(0, tm, drain, 0)
    gate = gate_ref[...]
    out = h_ref[...] + (buf[0] * gate[:, 0:1] + buf[1] * gate[:, 1:2])
    if final_norm:
        ms = jnp.mean(out * out, axis=-1, keepdims=True)
        out = out * lax.rsqrt(ms + EPS) * g_ref[...]
    o_ref[...] = out


def moe_combine(ys, dest, gate, h, g, *, final_norm):
    n, d = h.shape
    tm = TOK_TM
    return pl.pallas_call(
        functools.partial(_combine_kernel, tm=tm, final_norm=final_norm),
        out_shape=jax.ShapeDtypeStruct((n, d), F32),
        grid=(n // tm,),
        in_specs=[pl.BlockSpec(memory_space=pl.ANY), pl.BlockSpec((tm, LANES), lambda i: (i, 0)),
                  pl.BlockSpec((tm, d), lambda i: (i, 0)), pl.BlockSpec((1, d), lambda i: (0, 0)),
                  pl.BlockSpec(memory_space=pl.ANY)],
        out_specs=pl.BlockSpec((tm, d), lambda i: (i, 0)),
        scratch_shapes=[pltpu.SMEM((2 * tm,), I32), pltpu.VMEM((2, tm, d), F32),
                        pltpu.SemaphoreType.DMA, pltpu.SemaphoreType.DMA],
        compiler_params=_cparams("arbitrary"),
        name="moe_combine",
    )(dest, gate, h, g.reshape(1, d), ys)


def dispatch_plan(route, counts):
    m = MOE_BLOCK
    n = route.shape[0]
    counts = counts[0, :N_EXPERTS]
    padded = ((counts + m - 1) // m) * m
    pend = jnp.cumsum(padded)
    pstart = pend - padded
    eid = route[:, ROUTE_EID:ROUTE_EID + 2]
    rank = route[:, ROUTE_RANK:ROUTE_RANK + 2]
    onehot = eid[:, :, None] == jnp.arange(N_EXPERTS, dtype=I32)[None, None, :]
    dest = rank + jnp.sum(jnp.where(onehot, pstart[None, None, :], 0), axis=-1)
    rows = 2 * n + N_EXPERTS * m
    nblk = rows // m
    blk_start = jnp.arange(nblk, dtype=I32) * m
    nused = (pend[-1] // m).astype(I32)
    blk_exp = jnp.sum(blk_start[:, None] >= pend[None, :], axis=1).astype(I32)
    last_exp = jnp.sum(jnp.where(jnp.arange(nblk) == nused - 1, blk_exp, 0))
    blk_exp = jnp.where(jnp.arange(nblk) < nused, blk_exp, last_exp).astype(I32)
    return dest.reshape(n // TOK_TM, 2 * TOK_TM).astype(I32), blk_exp, nused.reshape(1), rows


def moe_layer(h1, xn, route, gate, counts, w_gate, w_up, w_down, norm_g, *, final_norm):
    dest, blk_exp, nused, rows = dispatch_plan(route, counts)
    xs = moe_dispatch(xn, dest, rows)
    ys = moe_experts(xs, blk_exp, nused, w_gate, w_up, w_down)
    return moe_combine(ys, dest, gate, h1, norm_g, final_norm=final_norm)


def kernel(x, attn_norm_g, ffn_norm_g, w_in_ab, w_out_ab, ret_gn_g, w_in_c, b_in_c, sinks, w_out_c, b_out_c,
           w_group, w_expert_router, w_gate, w_up, w_down, final_norm_g):
    batch, seq, d = x.shape
    n = batch * seq
    depth = attn_norm_g.shape[0]
    cos, sin = rope_lane_tables(seq)
    later = (jnp.arange(min(SB_TILE, seq))[:, None] > jnp.arange(min(SB_TILE, seq))[None, :]).astype(BF16)
    w_sb = N_HEADS_SB * HEAD_DIM
    w_ret = N_HEADS_RET * HEAD_DIM
    h = x.reshape(n, d)
    for layer in range(depth):
        i = layer // 2
        last = layer == depth - 1
        if layer % 2 == 0:
            proj = norm_proj(h, attn_norm_g[layer], w_in_ab[i].astype(BF16), jnp.zeros((w_in_ab.shape[2],), F32),
                             cos, sin, seq=seq, tn=w_sb,
                             col_ops=("scale", "", "", "rope", "rope scale", "", ""))
            per = w_sb // LANES
            a = sb_attention(proj, later, batch=batch, seq=seq, q_blk=0, k_blk=per, v_blk=2 * per)
            r = retention(proj, ret_gn_g[i], batch=batch, seq=seq,
                          q_blk=3 * per, k_blk=4 * per, v_blk=5 * per, g_blk=6 * per)
            w_out = w_out_ab[i].astype(BF16)
            lhs, ws = [a, r], [w_out[:w_sb], w_out[w_sb:]]
            bias = jnp.zeros((d,), F32)
        else:
            wkv = N_KV_SWA * HEAD_DIM
            nq = N_HEADS_SWA * HEAD_DIM // wkv
            proj = norm_proj(h, attn_norm_g[layer], w_in_c[i].astype(BF16), b_in_c[i].astype(F32),
                             cos, sin, seq=seq, tn=wkv,
                             col_ops=("rope scale",) * nq + ("rope", ""))
            o = swa_attention(proj, sinks[i], batch=batch, seq=seq)
            lhs, ws = [o], [w_out_c[i].astype(BF16)]
            bias = b_out_c[i].astype(F32)
        h1, xn, route, gate, counts = outproj_router(
            lhs, ws, bias, h, ffn_norm_g[layer], router_weights(w_group[layer], w_expert_router[layer]))
        h = moe_layer(h1, xn, route, gate, counts, w_gate[layer], w_up[layer], w_down[layer],
                      final_norm_g, final_norm=last)
    return h.reshape(batch, seq, d)
```

```python
import functools

import jax
import jax.numpy as jnp
import numpy as np
from jax import lax
from jax.experimental import pallas as pl
from jax.experimental.pallas import tpu as pltpu

F32 = jnp.float32
BF16 = jnp.bfloat16
I32 = jnp.int32

HEAD_DIM = 64
N_HEADS_SB = 8
N_HEADS_RET = 8
N_HEADS_SWA = 16
N_KV_SWA = 4
WINDOW = 128
ROPE_THETA = 10000.0
N_GROUPS = 4
EXPERTS_PER_GROUP = 8
N_EXPERTS = N_GROUPS * EXPERTS_PER_GROUP
MOE_BLOCK = 256
EPS = 1e-6

LANES = 128
HALF = HEAD_DIM // 2
QK_SCALE = HEAD_DIM ** -0.5

SB_TILE = 256
RET_CHUNK = 256
PROJ_TM = 512
TOK_TM = 256
VMEM_LIMIT = 48 * 1024 * 1024

_NT = (((1,), (1,)), ((), ()))
_TN = (((0,), (0,)), ((), ()))


def _cparams(*sem):
    return pltpu.CompilerParams(dimension_semantics=sem, vmem_limit_bytes=VMEM_LIMIT)


def _head0_mask():
    return lax.broadcasted_iota(I32, (1, LANES), 1) < HEAD_DIM


def _norm_proj_kernel(x_ref, g_ref, w_ref, b_ref, cos_ref, sin_ref, o_ref, *, tn, col_ops):
    x = x_ref[...]
    ms = jnp.mean(x * x, axis=-1, keepdims=True)
    xn = (x * lax.rsqrt(ms + EPS) * g_ref[...]).astype(BF16)
    lane = lax.broadcasted_iota(I32, (1, LANES), 1)
    first_half = (lane % HEAD_DIM) < HALF
    for j, op in enumerate(col_ops):
        cols = slice(j * tn, (j + 1) * tn)
        acc = jnp.dot(xn, w_ref[:, cols], preferred_element_type=F32) + b_ref[:, cols]
        if "rope" in op:
            cos = cos_ref[...]
            sin = sin_ref[...]
            slabs = []
            for s in range(tn // LANES):
                a = acc[:, s * LANES:(s + 1) * LANES]
                partner = jnp.where(first_half, pltpu.roll(a, LANES - HALF, 1), pltpu.roll(a, HALF, 1))
                slabs.append(a * cos + partner * sin)
            acc = jnp.concatenate(slabs, axis=1)
        if "scale" in op:
            acc = acc * QK_SCALE
        o_ref[:, cols] = acc.astype(BF16)


def norm_proj(x, g, w, b, cos, sin, *, seq, tn, col_ops):
    n, d = x.shape
    f = w.shape[1]
    tm = min(PROJ_TM, seq)
    assert n % tm == 0 and seq % tm == 0 and f == tn * len(col_ops)
    pos_blocks = seq // tm
    return pl.pallas_call(
        functools.partial(_norm_proj_kernel, tn=tn, col_ops=col_ops),
        out_shape=jax.ShapeDtypeStruct((n, f), BF16),
        grid=(n // tm,),
        in_specs=[
            pl.BlockSpec((tm, d), lambda i: (i, 0)),
            pl.BlockSpec((1, d), lambda i: (0, 0)),
            pl.BlockSpec((d, f), lambda i: (0, 0)),
            pl.BlockSpec((1, f), lambda i: (0, 0)),
            pl.BlockSpec((tm, LANES), lambda i: (i % pos_blocks, 0)),
            pl.BlockSpec((tm, LANES), lambda i: (i % pos_blocks, 0)),
        ],
        out_specs=pl.BlockSpec((tm, f), lambda i: (i, 0)),
        compiler_params=_cparams("arbitrary"),
        name="norm_proj",
    )(x, g.reshape(1, d), w, b.reshape(1, f), cos, sin)


def rope_lane_tables(seq):
    pos = jnp.arange(seq, dtype=F32)
    inv = ROPE_THETA ** (-jnp.arange(0, HEAD_DIM, 2, dtype=F32) / HEAD_DIM)
    ang = pos[:, None] * inv[None, :]
    cos, sin = jnp.cos(ang), jnp.sin(ang)
    return jnp.tile(cos, (1, 4)), jnp.tile(jnp.concatenate([-sin, sin], axis=1), (1, 2))


SB_PAIRS = 2


def _sb_kernel(q_ref, k_ref, v_ref, t_ref, o_ref, acc_ref, carry_ref, *, tile):
    qi = pl.program_id(2)
    m0 = _head0_mask()
    zero = jnp.zeros((tile, LANES), BF16)
    row = lax.broadcasted_iota(I32, (tile, tile), 0)
    col = lax.broadcasted_iota(I32, (tile, tile), 1)
    strict = col < row

    def key_tile(kb, diag):
        start = pl.multiple_of(kb * tile, tile)
        for p in range(SB_PAIRS):
            lanes = slice(p * LANES, (p + 1) * LANES)
            q = q_ref[:, lanes]
            k = k_ref[pl.ds(start, tile), lanes]
            v = v_ref[pl.ds(start, tile), lanes]
            out = None
            for h in range(2):
                qh = jnp.where(m0, q, zero) if h == 0 else jnp.where(m0, zero, q)
                vh = jnp.where(m0, v, zero) if h == 0 else jnp.where(m0, zero, v)
                z = lax.dot_general(qh, k, _NT, preferred_element_type=F32)
                log_fail = -(jnp.maximum(z, 0.0) + jnp.log(1.0 + jnp.exp(-jnp.abs(z))))
                if diag:
                    log_fail = jnp.where(strict, log_fail, 0.0)
                after = jnp.dot(log_fail.astype(BF16), t_ref[...], preferred_element_type=F32)
                logw = z + log_fail + after
                if not diag:
                    logw = logw + carry_ref[2 * p + h]
                w = jnp.exp(logw)
                if diag:
                    w = jnp.where(strict, w, 0.0)
                pv = jnp.dot(w.astype(BF16), vh, preferred_element_type=F32)
                out = pv if out is None else out + pv
                tile_sum = jnp.sum(log_fail, axis=-1, keepdims=True)
                if diag:
                    carry_ref[2 * p + h] = tile_sum
                else:
                    carry_ref[2 * p + h] += tile_sum
            if diag:
                acc_ref[:, lanes] = out
            else:
                acc_ref[:, lanes] += out

    key_tile(qi, True)

    def body(j, c):
        key_tile(qi - 1 - j, False)
        return c

    lax.fori_loop(0, qi, body, 0)
    o_ref[...] = acc_ref[...].astype(BF16)


def sb_attention(proj, later_mat, *, batch, seq, q_blk, k_blk, v_blk):
    n = proj.shape[0]
    tile = min(SB_TILE, seq)
    nq = seq // tile
    width = SB_PAIRS * LANES
    steps = N_HEADS_SB * HEAD_DIM // width
    assert q_blk % SB_PAIRS == 0 and k_blk % SB_PAIRS == 0 and v_blk % SB_PAIRS == 0
    col = lambda blk: blk // SB_PAIRS
    return pl.pallas_call(
        functools.partial(_sb_kernel, tile=tile),
        out_shape=jax.ShapeDtypeStruct((n, steps * width), BF16),
        grid=(batch, steps, nq),
        in_specs=[
            pl.BlockSpec((tile, width), lambda b, p, i: (b * nq + i, col(q_blk) + p)),
            pl.BlockSpec((seq, width), lambda b, p, i: (b, col(k_blk) + p)),
            pl.BlockSpec((seq, width), lambda b, p, i: (b, col(v_blk) + p)),
            pl.BlockSpec((tile, tile), lambda b, p, i: (0, 0)),
        ],
        out_specs=pl.BlockSpec((tile, width), lambda b, p, i: (b * nq + i, p)),
        scratch_shapes=[pltpu.VMEM((tile, width), F32), pltpu.VMEM((2 * SB_PAIRS, tile, 1), F32)],
        compiler_params=_cparams("arbitrary", "arbitrary", "arbitrary"),
        name="sb_attention",
    )(proj, proj, proj, later_mat)


def _retention_kernel(q_ref, k_ref, v_ref, gate_ref, gain_ref, dec_ref, xi_ref, zeta_ref, gch_ref,
                      o_ref, state_ref):
    c = pl.program_id(2)

    @pl.when(c == 0)
    def _():
        state_ref[...] = jnp.zeros_like(state_ref)

    m0 = _head0_mask()
    q = q_ref[...]
    k = k_ref[...]
    v = v_ref[...]
    zero = jnp.zeros_like(q)
    state = state_ref[...]
    y = jnp.dot(q, state.astype(BF16), preferred_element_type=F32) * xi_ref[...]
    for h in range(2):
        qh = jnp.where(m0, q, zero) if h == 0 else jnp.where(m0, zero, q)
        vh = jnp.where(m0, v, zero) if h == 0 else jnp.where(m0, zero, v)
        s = lax.dot_general(qh, k, _NT, preferred_element_type=F32)
        inner = (s * dec_ref[h]).astype(BF16)
        y = y + jnp.dot(inner, vh, preferred_element_type=F32)

    kz = (k.astype(F32) * zeta_ref[...]).astype(BF16)
    upd = lax.dot_general(kz, v, _TN, preferred_element_type=F32)
    r = lax.broadcasted_iota(I32, (LANES, LANES), 0) < HEAD_DIM
    cc = lax.broadcasted_iota(I32, (LANES, LANES), 1) < HEAD_DIM
    state_ref[...] = state * gch_ref[...] + jnp.where(r == cc, upd, 0.0)

    def head_mean(t):
        s0 = jnp.sum(jnp.where(m0, t, 0.0), axis=-1, keepdims=True)
        s1 = jnp.sum(jnp.where(m0, 0.0, t), axis=-1, keepdims=True)
        return jnp.where(m0, s0, s1) * (1.0 / HEAD_DIM)

    d = y - head_mean(y)
    yn = d * lax.rsqrt(head_mean(d * d) + EPS) * gain_ref[...]
    g = gate_ref[...].astype(F32)
    o_ref[...] = (yn * (g * (1.0 / (1.0 + jnp.exp(-g))))).astype(BF16)


def retention_tables(chunk):
    h = N_HEADS_RET
    log_g = jnp.log(1.0 - 2.0 ** (-5.0 - jnp.arange(h, dtype=F32)))
    idx = jnp.arange(chunk, dtype=F32)
    diff = idx[:, None] - idx[None, :]
    dec = jnp.where(diff[None] >= 0, jnp.exp(log_g[:, None, None] * jnp.maximum(diff, 0.0)[None]), 0.0)
    xi = jnp.exp(log_g[:, None] * (idx[None, :] + 1.0))
    zeta = jnp.exp(log_g[:, None] * (chunk - 1.0 - idx[None, :]))
    gch = jnp.exp(log_g * chunk)
    lanes = lambda t: jnp.repeat(t.T, HEAD_DIM, axis=1)
    return dec, lanes(xi), lanes(zeta), jnp.repeat(gch, HEAD_DIM)[None, :]


def retention(proj, gain, *, batch, seq, q_blk, k_blk, v_blk, g_blk):
    n = proj.shape[0]
    chunk = min(RET_CHUNK, seq)
    nc = seq // chunk
    pairs = N_HEADS_RET * HEAD_DIM // LANES
    dec, xi, zeta, gch = retention_tables(chunk)
    blk = lambda off: pl.BlockSpec((chunk, LANES), lambda b, p, c: (b * nc + c, off + p))
    return pl.pallas_call(
        _retention_kernel,
        out_shape=jax.ShapeDtypeStruct((n, pairs * LANES), BF16),
        grid=(batch, pairs, nc),
        in_specs=[
            blk(q_blk), blk(k_blk), blk(v_blk), blk(g_blk),
            pl.BlockSpec((1, LANES), lambda b, p, c: (0, p)),
            pl.BlockSpec((2, chunk, chunk), lambda b, p, c: (p, 0, 0)),
            pl.BlockSpec((chunk, LANES), lambda b, p, c: (0, p)),
            pl.BlockSpec((chunk, LANES), lambda b, p, c: (0, p)),
            pl.BlockSpec((1, LANES), lambda b, p, c: (0, p)),
        ],
        out_specs=pl.BlockSpec((chunk, LANES), lambda b, p, c: (b * nc + c, p)),
        scratch_shapes=[pltpu.VMEM((LANES, LANES), F32)],
        compiler_params=_cparams("arbitrary", "arbitrary", "arbitrary"),
        name="retention",
    )(proj, proj, proj, proj, gain.reshape(1, -1).astype(F32), dec, xi, zeta, gch)


def _swa_kernel(sink_ref, q_ref, kp_ref, kc_ref, vp_ref, vc_ref, o_ref):
    blk = pl.program_id(1)
    m0 = _head0_mask()
    w = WINDOW
    qi = lax.broadcasted_iota(I32, (w, 2 * w), 0)
    kj = lax.broadcasted_iota(I32, (w, 2 * w), 1)
    rel = qi + w - kj
    valid = (rel >= 0) & (rel < w) & ((blk > 0) | (kj >= w))
    kband = jnp.concatenate([kp_ref[...], kc_ref[...]], axis=0)
    vband = jnp.concatenate([vp_ref[...], vc_ref[...]], axis=0)
    group = N_HEADS_SWA // N_KV_SWA
    slabs = group * HEAD_DIM // LANES
    swap = lambda t: jnp.concatenate([t[:, HEAD_DIM:], t[:, :HEAD_DIM]], axis=1)
    zero_kv = jnp.zeros((2 * w, LANES), BF16)
    zero_q = jnp.zeros((slabs * w, LANES), BF16)
    ones_kv = jnp.ones((2 * w, LANES), BF16)

    def half_heads(q_stack, k_half, v_half, first_head):
        s = lax.dot_general(q_stack, k_half, _NT, preferred_element_type=F32)
        probs, sink_terms = [], []
        for slab in range(slabs):
            sink = sink_ref[first_head + 2 * slab]
            sh = jnp.where(valid, s[slab * w:(slab + 1) * w], -jnp.inf)
            m = jnp.maximum(jnp.max(sh, axis=-1, keepdims=True), sink)
            probs.append(jnp.exp(sh - m).astype(BF16))
            sink_terms.append(jnp.exp(sink - jnp.broadcast_to(m, (w, LANES))))
        p = jnp.concatenate(probs, axis=0)
        pv = jnp.dot(p, jnp.concatenate([v_half, ones_kv], axis=1), preferred_element_type=F32)
        denom = pv[:, LANES:] + jnp.concatenate(sink_terms, axis=0)
        return pv[:, :LANES] / denom

    for pair in range(N_KV_SWA * HEAD_DIM // LANES):
        k2 = kband[:, pair * LANES:(pair + 1) * LANES]
        v2 = vband[:, pair * LANES:(pair + 1) * LANES]
        k2s, v2s = swap(k2), swap(v2)
        for c in range(2):
            kvh = 2 * pair + c
            k_even, k_odd = (k2, k2s) if c == 0 else (k2s, k2)
            v_even, v_odd = (v2, v2s) if c == 0 else (v2s, v2)
            base = kvh * group * HEAD_DIM
            q_stack = jnp.concatenate([q_ref[:, base + t * LANES:base + (t + 1) * LANES] for t in range(slabs)],
                                      axis=0)
            out = (half_heads(jnp.where(m0, q_stack, zero_q), k_even, jnp.where(m0, v_even, zero_kv), kvh * group)
                   + half_heads(jnp.where(m0, zero_q, q_stack), k_odd, jnp.where(m0, zero_kv, v_odd),
                                kvh * group + 1))
            for t in range(slabs):
                o_ref[:, base + t * LANES:base + (t + 1) * LANES] = out[t * w:(t + 1) * w].astype(BF16)


def swa_attention(proj, sinks, *, batch, seq):
    n = proj.shape[0]
    w = WINDOW
    nb = seq // w
    wq = N_HEADS_SWA * HEAD_DIM
    wkv = N_KV_SWA * HEAD_DIM
    k_blk = wq // wkv
    cur = lambda off: pl.BlockSpec((w, wkv), lambda b, i: (b * nb + i, off))
    prev = lambda off: pl.BlockSpec((w, wkv), lambda b, i: (b * nb + jnp.maximum(i - 1, 0), off))
    return pl.pallas_call(
        _swa_kernel,
        out_shape=jax.ShapeDtypeStruct((n, wq), BF16),
        grid=(batch, nb),
        in_specs=[
            pl.BlockSpec(memory_space=pltpu.SMEM),
            pl.BlockSpec((w, wq), lambda b, i: (b * nb + i, 0)),
            prev(k_blk), cur(k_blk), prev(k_blk + 1), cur(k_blk + 1),
        ],
        out_specs=pl.BlockSpec((w, wq), lambda b, i: (b * nb + i, 0)),
        compiler_params=_cparams("arbitrary", "arbitrary"),
        name="swa_attention",
    )(sinks.astype(F32), proj, proj, proj, proj, proj)


ROUTE_EID = 0
GROUP_LANE = N_EXPERTS


def _split_bf16(t):
    hi = t.astype(BF16)
    return hi, (t - hi.astype(F32)).astype(BF16)


def _outproj_router_kernel(*refs, n_lhs):
    lhs = refs[:n_lhs]
    ws = refs[n_lhs:2 * n_lhs]
    b_ref, h_ref, g_ref, wr_ref = refs[2 * n_lhs:2 * n_lhs + 4]
    h1_ref, xn_ref, route_ref, gate_ref, cnt_ref, carry_ref = refs[2 * n_lhs + 4:]
    step = pl.program_id(0)

    @pl.when(step == 0)
    def _():
        carry_ref[...] = jnp.zeros_like(carry_ref)

    mix = b_ref[...]
    for a_ref, w_ref in zip(lhs, ws):
        mix = mix + jnp.dot(a_ref[...], w_ref[...], preferred_element_type=F32)
    h1 = h_ref[...] + mix
    h1_ref[...] = h1
    ms = jnp.mean(h1 * h1, axis=-1, keepdims=True)
    xn = h1 * lax.rsqrt(ms + EPS) * g_ref[...]
    xn_ref[...] = xn

    x_hi, x_lo = _split_bf16(xn)
    w_hi, w_lo = _split_bf16(wr_ref[...])
    logits = (jnp.dot(x_hi, w_hi, preferred_element_type=F32)
              + jnp.dot(x_lo, w_hi, preferred_element_type=F32)
              + jnp.dot(x_hi, w_lo, preferred_element_type=F32))
    tm = logits.shape[0]
    lane = lax.broadcasted_iota(I32, (tm, LANES), 1)
    lane_f = lane.astype(F32)
    neg = -jnp.inf

    def lane_max(t):
        return jnp.max(t, axis=-1, keepdims=True)

    def lane_sum(t):
        return jnp.sum(t, axis=-1, keepdims=True)

    def first_lane_of(t, value, mask):
        return jnp.min(jnp.where(mask & (t == value), lane_f, float(LANES)), axis=-1, keepdims=True)

    is_group = (lane >= GROUP_LANE) & (lane < GROUP_LANE + N_GROUPS)
    gl = jnp.where(is_group, logits, neg)
    ge = jnp.exp(gl - lane_max(gl))
    gp = ge / lane_sum(ge)
    g_prob = lane_max(gp)
    g_idx = first_lane_of(gp, g_prob, is_group) - float(GROUP_LANE)
    group_of_lane = lax.shift_right_logical(lane, int(np.log2(EXPERTS_PER_GROUP))).astype(F32)
    in_group = (lane < N_EXPERTS) & (group_of_lane == g_idx)
    el = jnp.where(in_group, logits, neg)
    ee = jnp.exp(el - lane_max(el))
    ep = ee / lane_sum(ee)
    p1 = lane_max(ep)
    i1 = first_lane_of(ep, p1, in_group)
    rest = in_group & (lane_f != i1)
    ep2 = jnp.where(rest, ep, neg)
    p2 = lane_max(ep2)
    i2 = first_lane_of(ep2, p2, rest)
    denom = p1 + p2
    gate1 = g_prob * (p1 / denom)
    gate2 = g_prob * (p2 / denom)

    cnt = jnp.where((lane_f == i1) | (lane_f == i2), 1.0, 0.0)
    carry_ref[...] += jnp.sum(cnt, axis=0, keepdims=True)
    cnt_ref[...] = jnp.broadcast_to(carry_ref[...], cnt_ref.shape).astype(I32)

    route = jnp.where(lane == ROUTE_EID, i1, 0.0)
    route = jnp.where(lane == ROUTE_EID + 1, i2, route)
    route_ref[...] = route.astype(I32)
    gate_ref[...] = jnp.where(lane == 0, gate1, jnp.where(lane == 1, gate2, 0.0))


def outproj_router(lhs, ws, bias, h, g, w_router):
    n, d = h.shape
    tm = TOK_TM
    assert n % tm == 0
    row_blk = lambda width: pl.BlockSpec((tm, width), lambda i: (i, 0))
    full = lambda a: pl.BlockSpec(a.shape, lambda i: (0, 0))
    bias2, g2 = bias.reshape(1, d), g.reshape(1, d)
    args = [*lhs, *ws, bias2, h, g2, w_router]
    in_specs = ([row_blk(a.shape[1]) for a in lhs] + [full(w) for w in ws]
                + [full(bias2), row_blk(d), full(g2), full(w_router)])
    return pl.pallas_call(
        functools.partial(_outproj_router_kernel, n_lhs=len(lhs)),
        out_shape=(
            jax.ShapeDtypeStruct((n, d), F32), jax.ShapeDtypeStruct((n, d), F32),
            jax.ShapeDtypeStruct((n, LANES), I32), jax.ShapeDtypeStruct((n, LANES), F32),
            jax.ShapeDtypeStruct((8, LANES), I32),
        ),
        grid=(n // tm,),
        in_specs=in_specs,
        out_specs=(row_blk(d), row_blk(d), row_blk(LANES), row_blk(LANES),
                   pl.BlockSpec((8, LANES), lambda i: (0, 0))),
        scratch_shapes=[pltpu.VMEM((1, LANES), F32)],
        compiler_params=_cparams("arbitrary"),
        name="outproj_router",
    )(*args)


def router_weights(w_group, w_expert_router):
    d = w_group.shape[0]
    pad = jnp.zeros((d, LANES - N_EXPERTS - N_GROUPS), F32)
    return jnp.concatenate([w_expert_router.astype(F32), w_group.astype(F32), pad], axis=1)


IDX_CHUNK = 1024
IDX_RING = 4
DMA_UNROLL = 8


def _row_copy(src, src_row, dst, dst_row, sem):
    return pltpu.make_async_copy(src.at[pl.ds(src_row, 1), :], dst.at[pl.ds(dst_row, 1), :], sem)


def _moe_kernel(blk_exp, q0s, nvalids, nused, order_hbm, xn_hbm, wg_ref, wu_ref, wd_ref, out_hbm,
                ibuf, xbuf, ybuf, wg_bf, wu_bf, wd_bf, isem, gsem, ssem):
    i = pl.program_id(0)
    nu = nused[0]
    m = MOE_BLOCK
    n_tok = xn_hbm.shape[0]

    window = 2 * IDX_CHUNK

    def idx_copy(blk):
        base = pl.multiple_of(q0s[blk] & ~(IDX_CHUNK - 1), IDX_CHUNK)
        slot = blk & (IDX_RING - 1)
        dst = ibuf.at[pl.ds(pl.multiple_of(slot * window, window), window)]
        return pltpu.make_async_copy(order_hbm.at[pl.ds(base, window)], dst, isem.at[slot])

    def pair_base(blk):
        return (blk & (IDX_RING - 1)) * window + (q0s[blk] & (IDX_CHUNK - 1))

    def tile_rows(buf, g):
        return buf.at[pl.ds(pl.multiple_of(g * DMA_UNROLL, DMA_UNROLL), DMA_UNROLL), :]

    def issue_gathers(blk):
        slot = blk & 1
        base = pair_base(blk)
        dst = xbuf.at[slot]
        sem = gsem.at[slot]

        def body(g, c):
            rows = tile_rows(dst, g)
            for u in range(DMA_UNROLL):
                pair = ibuf[base + g * DMA_UNROLL + u]
                tok = jnp.where(pair >= n_tok, pair - n_tok, pair)
                _row_copy(xn_hbm, tok, rows, u, sem).start()
            return c

        lax.fori_loop(0, m // DMA_UNROLL, body, 0)

    def wait_gathers(blk):
        slot = blk & 1
        pltpu.make_async_copy(xn_hbm.at[pl.ds(0, m), :], xbuf.at[slot], gsem.at[slot]).wait()

    def issue_scatters(blk):
        slot = blk & 1
        base = pair_base(blk)
        src = ybuf.at[slot]
        sem = ssem.at[slot]
        nv = nvalids[blk]
        groups = lax.shift_right_logical(nv, DMA_UNROLL.bit_length() - 1)

        def body(g, c):
            rows = tile_rows(src, g)
            for u in range(DMA_UNROLL):
                _row_copy(rows, u, out_hbm, ibuf[base + g * DMA_UNROLL + u], sem).start()
            return c

        def tail(r, c):
            _row_copy(src, r, out_hbm, ibuf[base + r], sem).start()
            return c

        lax.fori_loop(0, groups, body, 0)
        lax.fori_loop(groups * DMA_UNROLL, nv, tail, 0)

    def wait_scatters(blk):
        slot = blk & 1
        nv = nvalids[blk]
        whole = pl.multiple_of(nv & ~(DMA_UNROLL - 1), DMA_UNROLL)

        @pl.when(whole > 0)
        def _():
            pltpu.make_async_copy(ybuf.at[slot].at[pl.ds(0, whole), :], out_hbm.at[pl.ds(0, whole), :],
                                  ssem.at[slot]).wait()

        def tail(r, c):
            _row_copy(ybuf.at[slot], 0, out_hbm, 0, ssem.at[slot]).wait()
            return c

        lax.fori_loop(whole, nv, tail, 0)

    @pl.when(i == 0)
    def _():
        first = idx_copy(0)
        first.start()
        first.wait()
        issue_gathers(0)

        @pl.when(nu > 1)
        def _():
            idx_copy(1).start()

    @pl.when(i < nu)
    def _():
        @pl.when(i + 2 < nu)
        def _():
            idx_copy(i + 2).start()

        @pl.when(i + 1 < nu)
        def _():
            idx_copy(i + 1).wait()
            issue_gathers(i + 1)

        wait_gathers(i)

        @pl.when((i == 0) | (blk_exp[i] != blk_exp[jnp.maximum(i - 1, 0)]))
        def _():
            wg_bf[...] = wg_ref[0, 0].astype(BF16)
            wu_bf[...] = wu_ref[0, 0].astype(BF16)
            wd_bf[...] = wd_ref[0, 0].astype(BF16)

        @pl.when(i >= 2)
        def _():
            wait_scatters(i - 2)

        x = xbuf[i & 1].astype(BF16)
        gate = jnp.dot(x, wg_bf[...], preferred_element_type=F32)
        up = jnp.dot(x, wu_bf[...], preferred_element_type=F32)
        hidden = (gate * (1.0 / (1.0 + jnp.exp(-gate))) * up).astype(BF16)
        ybuf[i & 1] = jnp.dot(hidden, wd_bf[...], preferred_element_type=F32)
        issue_scatters(i)

        @pl.when(i == nu - 1)
        def _():
            @pl.when(i >= 1)
            def _():
                wait_scatters(i - 1)

            wait_scatters(i)


def moe_experts(order, plan, xn, w_gate, w_up, w_down, layer):
    blk_exp, q0s, nvalids, nused = plan
    n, d = xn.shape
    m = MOE_BLOCK
    de = w_gate.shape[3]
    w_in_spec = pl.BlockSpec((1, 1, d, de), lambda i, be, q0, nv, nu: (layer, be[i], 0, 0))
    return pl.pallas_call(
        _moe_kernel,
        out_shape=jax.ShapeDtypeStruct((2 * n, d), F32),
        grid_spec=pltpu.PrefetchScalarGridSpec(
            num_scalar_prefetch=4,
            grid=(blk_exp.shape[0],),
            in_specs=[
                pl.BlockSpec(memory_space=pl.ANY), pl.BlockSpec(memory_space=pl.ANY),
                w_in_spec, w_in_spec,
                pl.BlockSpec((1, 1, de, d), lambda i, be, q0, nv, nu: (layer, be[i], 0, 0)),
            ],
            out_specs=pl.BlockSpec(memory_space=pl.ANY),
            scratch_shapes=[
                pltpu.SMEM((IDX_RING * 2 * IDX_CHUNK,), I32),
                pltpu.VMEM((2, m, d), F32), pltpu.VMEM((2, m, d), F32),
                pltpu.VMEM((d, de), BF16), pltpu.VMEM((d, de), BF16), pltpu.VMEM((de, d), BF16),
                pltpu.SemaphoreType.DMA((IDX_RING,)), pltpu.SemaphoreType.DMA((2,)), pltpu.SemaphoreType.DMA((2,)),
            ],
        ),
        compiler_params=_cparams("arbitrary"),
        name="moe_experts",
    )(blk_exp, q0s, nvalids, nused, order, xn, w_gate, w_up, w_down)


def _moe_finish_kernel(y0_ref, y1_ref, gate_ref, h_ref, g_ref, o_ref, *, final_norm):
    gate = gate_ref[...]
    out = h_ref[...] + (y0_ref[...] * gate[:, 0:1] + y1_ref[...] * gate[:, 1:2])
    if final_norm:
        ms = jnp.mean(out * out, axis=-1, keepdims=True)
        out = out * lax.rsqrt(ms + EPS) * g_ref[...]
    o_ref[...] = out


def moe_finish(ys, gate, h, g, *, final_norm):
    n, d = h.shape
    tm = TOK_TM
    nt = n // tm
    row_blk = lambda width: pl.BlockSpec((tm, width), lambda i: (i, 0))
    return pl.pallas_call(
        functools.partial(_moe_finish_kernel, final_norm=final_norm),
        out_shape=jax.ShapeDtypeStruct((n, d), F32),
        grid=(nt,),
        in_specs=[row_blk(d), pl.BlockSpec((tm, d), lambda i: (i + nt, 0)), row_blk(LANES), row_blk(d),
                  pl.BlockSpec((1, d), lambda i: (0, 0))],
        out_specs=row_blk(d),
        compiler_params=_cparams("arbitrary"),
        name="moe_finish",
    )(ys, ys, gate, h, g.reshape(1, d))


def dispatch_plan(route, counts):
    m = MOE_BLOCK
    n = route.shape[0]
    assert (2 * n) % IDX_CHUNK == 0
    counts = counts[0, :N_EXPERTS]
    padded = ((counts + m - 1) // m) * m
    pend = jnp.cumsum(padded)
    pstart = pend - padded
    start = jnp.cumsum(counts) - counts
    eid = jnp.concatenate([route[:, ROUTE_EID], route[:, ROUTE_EID + 1]])
    order = jnp.argsort(eid, stable=True).astype(I32)
    order = jnp.concatenate([order, jnp.zeros((2 * IDX_CHUNK,), I32)])
    nblk = (2 * n + N_EXPERTS * m) // m
    blk_start = jnp.arange(nblk, dtype=I32) * m
    nused = pend[-1] // m
    used = jnp.arange(nblk) < nused
    e = jnp.minimum(jnp.sum(blk_start[:, None] >= pend[None, :], axis=1), N_EXPERTS - 1)
    r0 = blk_start - pstart[e]
    q0s = jnp.where(used, start[e] + r0, 0)
    nvalids = jnp.where(used, jnp.clip(counts[e] - r0, 0, m), 0)
    last_exp = jnp.sum(jnp.where(jnp.arange(nblk) == nused - 1, e, 0))
    blk_exp = jnp.where(used, e, last_exp)
    as_i32 = lambda t: t.astype(I32)
    return order, (as_i32(blk_exp), as_i32(q0s), as_i32(nvalids), as_i32(nused).reshape(1))


def moe_layer(h1, xn, route, gate, counts, w_gate, w_up, w_down, layer, norm_g, *, final_norm):
    order, plan = dispatch_plan(route, counts)
    ys = moe_experts(order, plan, xn, w_gate, w_up, w_down, layer)
    return moe_finish(ys, gate, h1, norm_g, final_norm=final_norm)


def kernel(x, attn_norm_g, ffn_norm_g, w_in_ab, w_out_ab, ret_gn_g, w_in_c, b_in_c, sinks, w_out_c, b_out_c,
           w_group, w_expert_router, w_gate, w_up, w_down, final_norm_g):
    batch, seq, d = x.shape
    n = batch * seq
    depth = attn_norm_g.shape[0]
    cos, sin = rope_lane_tables(seq)
    later = (jnp.arange(min(SB_TILE, seq))[:, None] > jnp.arange(min(SB_TILE, seq))[None, :]).astype(BF16)
    w_sb = N_HEADS_SB * HEAD_DIM
    w_ret = N_HEADS_RET * HEAD_DIM
    h = x.reshape(n, d)
    for layer in range(depth):
        i = layer // 2
        last = layer == depth - 1
        if layer % 2 == 0:
            proj = norm_proj(h, attn_norm_g[layer], w_in_ab[i].astype(BF16), jnp.zeros((w_in_ab.shape[2],), F32),
                             cos, sin, seq=seq, tn=w_sb,
                             col_ops=("scale", "", "", "rope", "rope scale", "", ""))
            per = w_sb // LANES
            a = sb_attention(proj, later, batch=batch, seq=seq, q_blk=0, k_blk=per, v_blk=2 * per)
            r = retention(proj, ret_gn_g[i], batch=batch, seq=seq,
                          q_blk=3 * per, k_blk=4 * per, v_blk=5 * per, g_blk=6 * per)
            w_out = w_out_ab[i].astype(BF16)
            lhs, ws = [a, r], [w_out[:w_sb], w_out[w_sb:]]
            bias = jnp.zeros((d,), F32)
        else:
            wkv = N_KV_SWA * HEAD_DIM
            nq = N_HEADS_SWA * HEAD_DIM // wkv
            proj = norm_proj(h, attn_norm_g[layer], w_in_c[i].astype(BF16), b_in_c[i].astype(F32),
                             cos, sin, seq=seq, tn=wkv,
                             col_ops=("rope scale",) * nq + ("rope", ""))
            o = swa_attention(proj, sinks[i], batch=batch, seq=seq)
            lhs, ws = [o], [w_out_c[i].astype(BF16)]
            bias = b_out_c[i].astype(F32)
        h1, xn, route, gate, counts = outproj_router(
            lhs, ws, bias, h, ffn_norm_g[layer], router_weights(w_group[layer], w_expert_router[layer]))
        h = moe_layer(h1, xn, route, gate, counts, w_gate, w_up, w_down, layer, final_norm_g, final_norm=last)
    return h.reshape(batch, seq, d)
```

```python
import functools

import jax
import jax.numpy as jnp
import numpy as np
from jax import lax
from jax.experimental import pallas as pl
from jax.experimental.pallas import tpu as pltpu

F32 = jnp.float32
BF16 = jnp.bfloat16
I32 = jnp.int32

HEAD_DIM = 64
N_HEADS_SB = 8
N_HEADS_RET = 8
N_HEADS_SWA = 16
N_KV_SWA = 4
WINDOW = 128
ROPE_THETA = 10000.0
N_GROUPS = 4
EXPERTS_PER_GROUP = 8
N_EXPERTS = N_GROUPS * EXPERTS_PER_GROUP
MOE_BLOCK = 256
EPS = 1e-6

LANES = 128
HALF = HEAD_DIM // 2
QK_SCALE = HEAD_DIM ** -0.5

SB_TILE = 256
RET_CHUNK = 256
PROJ_TM = 512
TOK_TM = 256
ROUTER_TM = 512
ROUTER_SUB = 256
VMEM_LIMIT = 48 * 1024 * 1024

_NT = (((1,), (1,)), ((), ()))
_TN = (((0,), (0,)), ((), ()))


def _cparams(*sem):
    return pltpu.CompilerParams(dimension_semantics=sem, vmem_limit_bytes=VMEM_LIMIT)


def _head0_mask():
    return lax.broadcasted_iota(I32, (1, LANES), 1) < HEAD_DIM


def _norm_proj_kernel(x_ref, g_ref, w_ref, b_ref, cos_ref, sin_ref, o_ref, *, tn, col_ops):
    x = x_ref[...]
    ms = jnp.mean(x * x, axis=-1, keepdims=True)
    xn = (x * lax.rsqrt(ms + EPS) * g_ref[...]).astype(BF16)
    lane = lax.broadcasted_iota(I32, (1, LANES), 1)
    first_half = (lane % HEAD_DIM) < HALF
    for j, op in enumerate(col_ops):
        cols = slice(j * tn, (j + 1) * tn)
        acc = jnp.dot(xn, w_ref[:, cols], preferred_element_type=F32) + b_ref[:, cols]
        if "rope" in op:
            cos = cos_ref[...]
            sin = sin_ref[...]
            slabs = []
            for s in range(tn // LANES):
                a = acc[:, s * LANES:(s + 1) * LANES]
                partner = jnp.where(first_half, pltpu.roll(a, LANES - HALF, 1), pltpu.roll(a, HALF, 1))
                slabs.append(a * cos + partner * sin)
            acc = jnp.concatenate(slabs, axis=1)
        if "scale" in op:
            acc = acc * QK_SCALE
        o_ref[:, cols] = acc.astype(BF16)


def norm_proj(x, g, w, b, cos, sin, *, seq, tn, col_ops):
    n, d = x.shape
    f = w.shape[1]
    tm = min(PROJ_TM, seq)
    assert n % tm == 0 and seq % tm == 0 and f == tn * len(col_ops)
    pos_blocks = seq // tm
    return pl.pallas_call(
        functools.partial(_norm_proj_kernel, tn=tn, col_ops=col_ops),
        out_shape=jax.ShapeDtypeStruct((n, f), BF16),
        grid=(n // tm,),
        in_specs=[
            pl.BlockSpec((tm, d), lambda i: (i, 0)),
            pl.BlockSpec((1, d), lambda i: (0, 0)),
            pl.BlockSpec((d, f), lambda i: (0, 0)),
            pl.BlockSpec((1, f), lambda i: (0, 0)),
            pl.BlockSpec((tm, LANES), lambda i: (i % pos_blocks, 0)),
            pl.BlockSpec((tm, LANES), lambda i: (i % pos_blocks, 0)),
        ],
        out_specs=pl.BlockSpec((tm, f), lambda i: (i, 0)),
        compiler_params=_cparams("arbitrary"),
        name="norm_proj",
    )(x, g.reshape(1, d), w, b.reshape(1, f), cos, sin)


def rope_lane_tables(seq):
    pos = jnp.arange(seq, dtype=F32)
    inv = ROPE_THETA ** (-jnp.arange(0, HEAD_DIM, 2, dtype=F32) / HEAD_DIM)
    ang = pos[:, None] * inv[None, :]
    cos, sin = jnp.cos(ang), jnp.sin(ang)
    return jnp.tile(cos, (1, 4)), jnp.tile(jnp.concatenate([-sin, sin], axis=1), (1, 2))


SB_PAIRS = 4


def _sb_kernel(q_ref, k_ref, v_ref, t_ref, o_ref, acc_ref, carry_ref, *, tile):
    qi = pl.program_id(2)
    m0 = _head0_mask()
    zero = jnp.zeros((tile, LANES), BF16)
    row = lax.broadcasted_iota(I32, (tile, tile), 0)
    col = lax.broadcasted_iota(I32, (tile, tile), 1)
    strict = col < row

    def key_tile(kb, diag):
        start = pl.multiple_of(kb * tile, tile)
        for p in range(SB_PAIRS):
            lanes = slice(p * LANES, (p + 1) * LANES)
            q = q_ref[:, lanes]
            k = k_ref[pl.ds(start, tile), lanes]
            v = v_ref[pl.ds(start, tile), lanes]
            out = None
            for h in range(2):
                qh = jnp.where(m0, q, zero) if h == 0 else jnp.where(m0, zero, q)
                vh = jnp.where(m0, v, zero) if h == 0 else jnp.where(m0, zero, v)
                z = lax.dot_general(qh, k, _NT, preferred_element_type=F32)
                log_fail = -(jnp.maximum(z, 0.0) + jnp.log(1.0 + jnp.exp(-jnp.abs(z))))
                if diag:
                    log_fail = jnp.where(strict, log_fail, 0.0)
                after = jnp.dot(log_fail.astype(BF16), t_ref[...], preferred_element_type=F32)
                logw = z + log_fail + after
                if not diag:
                    logw = logw + carry_ref[2 * p + h]
                w = jnp.exp(logw)
                if diag:
                    w = jnp.where(strict, w, 0.0)
                pv = jnp.dot(w.astype(BF16), vh, preferred_element_type=F32)
                out = pv if out is None else out + pv
                tile_sum = jnp.sum(log_fail, axis=-1, keepdims=True)
                if diag:
                    carry_ref[2 * p + h] = tile_sum
                else:
                    carry_ref[2 * p + h] += tile_sum
            if diag:
                acc_ref[:, lanes] = out
            else:
                acc_ref[:, lanes] += out

    key_tile(qi, True)

    def body(j, c):
        key_tile(qi - 1 - j, False)
        return c

    lax.fori_loop(0, qi, body, 0)
    o_ref[...] = acc_ref[...].astype(BF16)


def sb_attention(proj, later_mat, *, batch, seq, q_blk, k_blk, v_blk):
    n = proj.shape[0]
    tile = min(SB_TILE, seq)
    nq = seq // tile
    width = SB_PAIRS * LANES
    steps = N_HEADS_SB * HEAD_DIM // width
    assert q_blk % SB_PAIRS == 0 and k_blk % SB_PAIRS == 0 and v_blk % SB_PAIRS == 0
    col = lambda blk: blk // SB_PAIRS
    return pl.pallas_call(
        functools.partial(_sb_kernel, tile=tile),
        out_shape=jax.ShapeDtypeStruct((n, steps * width), BF16),
        grid=(batch, steps, nq),
        in_specs=[
            pl.BlockSpec((tile, width), lambda b, p, i: (b * nq + i, col(q_blk) + p)),
            pl.BlockSpec((seq, width), lambda b, p, i: (b, col(k_blk) + p)),
            pl.BlockSpec((seq, width), lambda b, p, i: (b, col(v_blk) + p)),
            pl.BlockSpec((tile, tile), lambda b, p, i: (0, 0)),
        ],
        out_specs=pl.BlockSpec((tile, width), lambda b, p, i: (b * nq + i, p)),
        scratch_shapes=[pltpu.VMEM((tile, width), F32), pltpu.VMEM((2 * SB_PAIRS, tile, 1), F32)],
        compiler_params=_cparams("arbitrary", "arbitrary", "arbitrary"),
        name="sb_attention",
    )(proj, proj, proj, later_mat)


RET_PAIRS = 4


def _retention_kernel(q_ref, k_ref, v_ref, gate_ref, gain_ref, dec_ref, xi_ref, zeta_ref, gch_ref,
                      o_ref, state_ref):
    c = pl.program_id(2)

    @pl.when(c == 0)
    def _():
        state_ref[...] = jnp.zeros_like(state_ref)

    m0 = _head0_mask()
    zero = jnp.zeros((q_ref.shape[0], LANES), BF16)
    r = lax.broadcasted_iota(I32, (LANES, LANES), 0) < HEAD_DIM
    cc = lax.broadcasted_iota(I32, (LANES, LANES), 1) < HEAD_DIM
    same_head = r == cc

    def head_mean(t):
        s0 = jnp.sum(jnp.where(m0, t, 0.0), axis=-1, keepdims=True)
        s1 = jnp.sum(jnp.where(m0, 0.0, t), axis=-1, keepdims=True)
        return jnp.where(m0, s0, s1) * (1.0 / HEAD_DIM)

    for p in range(RET_PAIRS):
        lanes = slice(p * LANES, (p + 1) * LANES)
        q = q_ref[:, lanes]
        k = k_ref[:, lanes]
        v = v_ref[:, lanes]
        state = state_ref[p]
        y = jnp.dot(q, state.astype(BF16), preferred_element_type=F32) * xi_ref[:, lanes]
        for h in range(2):
            qh = jnp.where(m0, q, zero) if h == 0 else jnp.where(m0, zero, q)
            vh = jnp.where(m0, v, zero) if h == 0 else jnp.where(m0, zero, v)
            s = lax.dot_general(qh, k, _NT, preferred_element_type=F32)
            inner = (s * dec_ref[2 * p + h]).astype(BF16)
            y = y + jnp.dot(inner, vh, preferred_element_type=F32)

        kz = (k.astype(F32) * zeta_ref[:, lanes]).astype(BF16)
        upd = lax.dot_general(kz, v, _TN, preferred_element_type=F32)
        state_ref[p] = state * gch_ref[:, lanes] + jnp.where(same_head, upd, 0.0)

        d = y - head_mean(y)
        yn = d * lax.rsqrt(head_mean(d * d) + EPS) * gain_ref[:, lanes]
        g = gate_ref[:, lanes].astype(F32)
        o_ref[:, lanes] = (yn * (g * (1.0 / (1.0 + jnp.exp(-g))))).astype(BF16)


def retention_tables(chunk):
    h = N_HEADS_RET
    log_g = jnp.log(1.0 - 2.0 ** (-5.0 - jnp.arange(h, dtype=F32)))
    idx = jnp.arange(chunk, dtype=F32)
    diff = idx[:, None] - idx[None, :]
    dec = jnp.where(diff[None] >= 0, jnp.exp(log_g[:, None, None] * jnp.maximum(diff, 0.0)[None]), 0.0)
    xi = jnp.exp(log_g[:, None] * (idx[None, :] + 1.0))
    zeta = jnp.exp(log_g[:, None] * (chunk - 1.0 - idx[None, :]))
    gch = jnp.exp(log_g * chunk)
    lanes = lambda t: jnp.repeat(t.T, HEAD_DIM, axis=1)
    return dec, lanes(xi), lanes(zeta), jnp.repeat(gch, HEAD_DIM)[None, :]


def retention(proj, gain, *, batch, seq, q_blk, k_blk, v_blk, g_blk):
    n = proj.shape[0]
    chunk = min(RET_CHUNK, seq)
    nc = seq // chunk
    width = RET_PAIRS * LANES
    steps = N_HEADS_RET * HEAD_DIM // width
    assert all(b % RET_PAIRS == 0 for b in (q_blk, k_blk, v_blk, g_blk))
    dec, xi, zeta, gch = retention_tables(chunk)
    blk = lambda off: pl.BlockSpec((chunk, width), lambda b, p, c: (b * nc + c, off // RET_PAIRS + p))
    return pl.pallas_call(
        _retention_kernel,
        out_shape=jax.ShapeDtypeStruct((n, steps * width), BF16),
        grid=(batch, steps, nc),
        in_specs=[
            blk(q_blk), blk(k_blk), blk(v_blk), blk(g_blk),
            pl.BlockSpec((1, width), lambda b, p, c: (0, p)),
            pl.BlockSpec((2 * RET_PAIRS, chunk, chunk), lambda b, p, c: (p, 0, 0)),
            pl.BlockSpec((chunk, width), lambda b, p, c: (0, p)),
            pl.BlockSpec((chunk, width), lambda b, p, c: (0, p)),
            pl.BlockSpec((1, width), lambda b, p, c: (0, p)),
        ],
        out_specs=pl.BlockSpec((chunk, width), lambda b, p, c: (b * nc + c, p)),
        scratch_shapes=[pltpu.VMEM((RET_PAIRS, LANES, LANES), F32)],
        compiler_params=_cparams("arbitrary", "arbitrary", "arbitrary"),
        name="retention",
    )(proj, proj, proj, proj, gain.reshape(1, -1).astype(F32), dec, xi, zeta, gch)


def _swa_kernel(sink_ref, q_ref, kp_ref, kc_ref, vp_ref, vc_ref, o_ref):
    blk = pl.program_id(1)
    m0 = _head0_mask()
    w = WINDOW
    qi = lax.broadcasted_iota(I32, (w, 2 * w), 0)
    kj = lax.broadcasted_iota(I32, (w, 2 * w), 1)
    rel = qi + w - kj
    valid = (rel >= 0) & (rel < w) & ((blk > 0) | (kj >= w))
    kband = jnp.concatenate([kp_ref[...], kc_ref[...]], axis=0)
    vband = jnp.concatenate([vp_ref[...], vc_ref[...]], axis=0)
    group = N_HEADS_SWA // N_KV_SWA
    slabs = group * HEAD_DIM // LANES
    swap = lambda t: jnp.concatenate([t[:, HEAD_DIM:], t[:, :HEAD_DIM]], axis=1)
    zero_kv = jnp.zeros((2 * w, LANES), BF16)
    zero_q = jnp.zeros((slabs * w, LANES), BF16)
    ones_kv = jnp.ones((2 * w, LANES), BF16)

    def half_heads(q_stack, k_half, v_half, first_head):
        s = lax.dot_general(q_stack, k_half, _NT, preferred_element_type=F32)
        probs, sink_terms = [], []
        for slab in range(slabs):
            sink = sink_ref[first_head + 2 * slab]
            sh = jnp.where(valid, s[slab * w:(slab + 1) * w], -jnp.inf)
            m = jnp.maximum(jnp.max(sh, axis=-1, keepdims=True), sink)
            probs.append(jnp.exp(sh - m).astype(BF16))
            sink_terms.append(jnp.exp(sink - jnp.broadcast_to(m, (w, LANES))))
        p = jnp.concatenate(probs, axis=0)
        pv = jnp.dot(p, jnp.concatenate([v_half, ones_kv], axis=1), preferred_element_type=F32)
        denom = pv[:, LANES:] + jnp.concatenate(sink_terms, axis=0)
        return pv[:, :LANES] / denom

    for pair in range(N_KV_SWA * HEAD_DIM // LANES):
        k2 = kband[:, pair * LANES:(pair + 1) * LANES]
        v2 = vband[:, pair * LANES:(pair + 1) * LANES]
        k2s, v2s = swap(k2), swap(v2)
        for c in range(2):
            kvh = 2 * pair + c
            k_even, k_odd = (k2, k2s) if c == 0 else (k2s, k2)
            v_even, v_odd = (v2, v2s) if c == 0 else (v2s, v2)
            base = kvh * group * HEAD_DIM
            q_stack = jnp.concatenate([q_ref[:, base + t * LANES:base + (t + 1) * LANES] for t in range(slabs)],
                                      axis=0)
            out = (half_heads(jnp.where(m0, q_stack, zero_q), k_even, jnp.where(m0, v_even, zero_kv), kvh * group)
                   + half_heads(jnp.where(m0, zero_q, q_stack), k_odd, jnp.where(m0, zero_kv, v_odd),
                                kvh * group + 1))
            for t in range(slabs):
                o_ref[:, base + t * LANES:base + (t + 1) * LANES] = out[t * w:(t + 1) * w].astype(BF16)


def swa_attention(proj, sinks, *, batch, seq):
    n = proj.shape[0]
    w = WINDOW
    nb = seq // w
    wq = N_HEADS_SWA * HEAD_DIM
    wkv = N_KV_SWA * HEAD_DIM
    k_blk = wq // wkv
    cur = lambda off: pl.BlockSpec((w, wkv), lambda b, i: (b * nb + i, off))
    prev = lambda off: pl.BlockSpec((w, wkv), lambda b, i: (b * nb + jnp.maximum(i - 1, 0), off))
    return pl.pallas_call(
        _swa_kernel,
        out_shape=jax.ShapeDtypeStruct((n, wq), BF16),
        grid=(batch, nb),
        in_specs=[
            pl.BlockSpec(memory_space=pltpu.SMEM),
            pl.BlockSpec((w, wq), lambda b, i: (b * nb + i, 0)),
            prev(k_blk), cur(k_blk), prev(k_blk + 1), cur(k_blk + 1),
        ],
        out_specs=pl.BlockSpec((w, wq), lambda b, i: (b * nb + i, 0)),
        compiler_params=_cparams("arbitrary", "arbitrary"),
        name="swa_attention",
    )(sinks.astype(F32), proj, proj, proj, proj, proj)


ROUTE_EID = 0
GROUP_LANE = N_EXPERTS


def _split_bf16(t):
    hi = t.astype(BF16)
    return hi, (t - hi.astype(F32)).astype(BF16)


def _outproj_router_kernel(*refs, n_lhs):
    lhs = refs[:n_lhs]
    ws = refs[n_lhs:2 * n_lhs]
    b_ref, h_ref, g_ref, wr_ref = refs[2 * n_lhs:2 * n_lhs + 4]
    h1_ref, xn_ref, route_ref, gate_ref, cnt_ref, carry_ref = refs[2 * n_lhs + 4:]
    step = pl.program_id(0)

    @pl.when(step == 0)
    def _():
        carry_ref[...] = jnp.zeros_like(carry_ref)

    w_split = jnp.concatenate(_split_bf16(wr_ref[...]), axis=1)
    for sub in range(h_ref.shape[0] // ROUTER_SUB):
        rows = slice(sub * ROUTER_SUB, (sub + 1) * ROUTER_SUB)
        mix = b_ref[...]
        for a_ref, w_ref in zip(lhs, ws):
            mix = mix + jnp.dot(a_ref[rows], w_ref[...], preferred_element_type=F32)
        h1 = h_ref[rows] + mix
        h1_ref[rows] = h1
        ms = jnp.mean(h1 * h1, axis=-1, keepdims=True)
        xn = h1 * lax.rsqrt(ms + EPS) * g_ref[...]
        xn_ref[rows] = xn
        x_hi, x_lo = _split_bf16(xn)
        parts = (jnp.dot(x_hi, w_split, preferred_element_type=F32)
                 + jnp.dot(x_lo, w_split, preferred_element_type=F32))
        logits = parts[:, :LANES] + parts[:, LANES:]
        route, gate, cnt = _route(logits)
        route_ref[rows] = route
        gate_ref[rows] = gate
        carry_ref[...] += cnt
    cnt_ref[...] = jnp.broadcast_to(carry_ref[...], cnt_ref.shape).astype(I32)


def _route(logits):
    tm = logits.shape[0]
    lane = lax.broadcasted_iota(I32, (tm, LANES), 1)
    lane_f = lane.astype(F32)
    neg = -jnp.inf

    def lane_max(t):
        return jnp.max(t, axis=-1, keepdims=True)

    def lane_sum(t):
        return jnp.sum(t, axis=-1, keepdims=True)

    def first_lane_of(t, value, mask):
        return jnp.min(jnp.where(mask & (t == value), lane_f, float(LANES)), axis=-1, keepdims=True)

    is_group = (lane >= GROUP_LANE) & (lane < GROUP_LANE + N_GROUPS)
    gl = jnp.where(is_group, logits, neg)
    ge = jnp.exp(gl - lane_max(gl))
    gp = ge / lane_sum(ge)
    g_prob = lane_max(gp)
    g_idx = first_lane_of(gp, g_prob, is_group) - float(GROUP_LANE)
    group_of_lane = lax.shift_right_logical(lane, int(np.log2(EXPERTS_PER_GROUP))).astype(F32)
    in_group = (lane < N_EXPERTS) & (group_of_lane == g_idx)
    el = jnp.where(in_group, logits, neg)
    ee = jnp.exp(el - lane_max(el))
    ep = ee / lane_sum(ee)
    p1 = lane_max(ep)
    i1 = first_lane_of(ep, p1, in_group)
    rest = in_group & (lane_f != i1)
    ep2 = jnp.where(rest, ep, neg)
    p2 = lane_max(ep2)
    i2 = first_lane_of(ep2, p2, rest)
    denom = p1 + p2
    gate1 = g_prob * (p1 / denom)
    gate2 = g_prob * (p2 / denom)

    cnt = jnp.where((lane_f == i1) | (lane_f == i2), 1.0, 0.0)
    route = jnp.where(lane == ROUTE_EID, i1, 0.0)
    route = jnp.where(lane == ROUTE_EID + 1, i2, route)
    gate = jnp.where(lane == 0, gate1, jnp.where(lane == 1, gate2, 0.0))
    return route.astype(I32), gate, jnp.sum(cnt, axis=0, keepdims=True)


def outproj_router(lhs, ws, bias, h, g, w_router):
    n, d = h.shape
    tm = ROUTER_TM
    assert n % tm == 0
    row_blk = lambda width: pl.BlockSpec((tm, width), lambda i: (i, 0))
    full = lambda a: pl.BlockSpec(a.shape, lambda i: (0, 0))
    bias2, g2 = bias.reshape(1, d), g.reshape(1, d)
    args = [*lhs, *ws, bias2, h, g2, w_router]
    in_specs = ([row_blk(a.shape[1]) for a in lhs] + [full(w) for w in ws]
                + [full(bias2), row_blk(d), full(g2), full(w_router)])
    return pl.pallas_call(
        functools.partial(_outproj_router_kernel, n_lhs=len(lhs)),
        out_shape=(
            jax.ShapeDtypeStruct((n, d), F32), jax.ShapeDtypeStruct((n, d), F32),
            jax.ShapeDtypeStruct((n, LANES), I32), jax.ShapeDtypeStruct((n, LANES), F32),
            jax.ShapeDtypeStruct((8, LANES), I32),
        ),
        grid=(n // tm,),
        in_specs=in_specs,
        out_specs=(row_blk(d), row_blk(d), row_blk(LANES), row_blk(LANES),
                   pl.BlockSpec((8, LANES), lambda i: (0, 0))),
        scratch_shapes=[pltpu.VMEM((1, LANES), F32)],
        compiler_params=_cparams("arbitrary"),
        name="outproj_router",
    )(*args)


def router_weights(w_group, w_expert_router):
    d = w_group.shape[0]
    pad = jnp.zeros((d, LANES - N_EXPERTS - N_GROUPS), F32)
    return jnp.concatenate([w_expert_router.astype(F32), w_group.astype(F32), pad], axis=1)


IDX_CHUNK = 1024
IDX_RING = 4
DMA_UNROLL = 8


def _row_copy(src, src_row, dst, dst_row, sem):
    return pltpu.make_async_copy(src.at[pl.ds(src_row, 1), :], dst.at[pl.ds(dst_row, 1), :], sem)


def _moe_kernel(blk_exp, q0s, nvalids, nused, order_hbm, xn_hbm, wg_ref, wu_ref, wd_ref, out_hbm,
                ibuf, xbuf, ybuf, wg_bf, wu_bf, wd_bf, isem, gsem, ssem):
    i = pl.program_id(0)
    nu = nused[0]
    m = MOE_BLOCK
    n_tok = xn_hbm.shape[0]

    window = 2 * IDX_CHUNK

    def idx_copy(blk):
        base = pl.multiple_of(q0s[blk] & ~(IDX_CHUNK - 1), IDX_CHUNK)
        slot = blk & (IDX_RING - 1)
        dst = ibuf.at[pl.ds(pl.multiple_of(slot * window, window), window)]
        return pltpu.make_async_copy(order_hbm.at[pl.ds(base, window)], dst, isem.at[slot])

    def pair_base(blk):
        return (blk & (IDX_RING - 1)) * window + (q0s[blk] & (IDX_CHUNK - 1))

    def tile_rows(buf, g):
        return buf.at[pl.ds(pl.multiple_of(g * DMA_UNROLL, DMA_UNROLL), DMA_UNROLL), :]

    def gather_group(blk_base, dst, sem, g):
        rows = tile_rows(dst, g)
        for u in range(DMA_UNROLL):
            pair = ibuf[blk_base + g * DMA_UNROLL + u]
            tok = jnp.where(pair >= n_tok, pair - n_tok, pair)
            _row_copy(xn_hbm, tok, rows, u, sem).start()

    def scatter_group(blk_base, src, sem, g):
        rows = tile_rows(src, g)
        for u in range(DMA_UNROLL):
            _row_copy(rows, u, out_hbm, ibuf[blk_base + g * DMA_UNROLL + u], sem).start()

    def issue_gathers(blk, unrolled=False):
        slot = blk & 1
        args = (pair_base(blk), xbuf.at[slot], gsem.at[slot])
        if unrolled:
            for g in range(m // DMA_UNROLL):
                gather_group(*args, g)
        else:
            lax.fori_loop(0, m // DMA_UNROLL, lambda g, c: (gather_group(*args, g), c)[1], 0)

    def wait_gathers(blk):
        slot = blk & 1
        pltpu.make_async_copy(xn_hbm.at[pl.ds(0, m), :], xbuf.at[slot], gsem.at[slot]).wait()

    def issue_scatters(blk, unrolled=False):
        slot = blk & 1
        base = pair_base(blk)
        src = ybuf.at[slot]
        sem = ssem.at[slot]
        if unrolled:
            for g in range(m // DMA_UNROLL):
                scatter_group(base, src, sem, g)
            return
        nv = nvalids[blk]
        groups = lax.shift_right_logical(nv, DMA_UNROLL.bit_length() - 1)

        def tail(r, c):
            _row_copy(src, r, out_hbm, ibuf[base + r], sem).start()
            return c

        lax.fori_loop(0, groups, lambda g, c: (scatter_group(base, src, sem, g), c)[1], 0)
        lax.fori_loop(groups * DMA_UNROLL, nv, tail, 0)

    def wait_scatters(blk):
        slot = blk & 1
        nv = nvalids[blk]
        whole = pl.multiple_of(nv & ~(DMA_UNROLL - 1), DMA_UNROLL)

        @pl.when(whole > 0)
        def _():
            pltpu.make_async_copy(ybuf.at[slot].at[pl.ds(0, whole), :], out_hbm.at[pl.ds(0, whole), :],
                                  ssem.at[slot]).wait()

        def tail(r, c):
            _row_copy(ybuf.at[slot], 0, out_hbm, 0, ssem.at[slot]).wait()
            return c

        lax.fori_loop(whole, nv, tail, 0)

    @pl.when(i == 0)
    def _():
        first = idx_copy(0)
        first.start()
        first.wait()
        issue_gathers(0)

        @pl.when(nu > 1)
        def _():
            idx_copy(1).start()

    def expert():
        x = xbuf[i & 1].astype(BF16)
        gate = jnp.dot(x, wg_bf[...], preferred_element_type=F32)
        up = jnp.dot(x, wu_bf[...], preferred_element_type=F32)
        hidden = (gate * (1.0 / (1.0 + jnp.exp(-gate))) * up).astype(BF16)
        ybuf[i & 1] = jnp.dot(hidden, wd_bf[...], preferred_element_type=F32)

    @pl.when(i < nu)
    def _():
        @pl.when(i + 2 < nu)
        def _():
            idx_copy(i + 2).start()

        @pl.when(i + 1 < nu)
        def _():
            idx_copy(i + 1).wait()

        wait_gathers(i)

        @pl.when((i == 0) | (blk_exp[i] != blk_exp[jnp.maximum(i - 1, 0)]))
        def _():
            wg_bf[...] = wg_ref[0, 0].astype(BF16)
            wu_bf[...] = wu_ref[0, 0].astype(BF16)
            wd_bf[...] = wd_ref[0, 0].astype(BF16)

        @pl.when(i >= 2)
        def _():
            wait_scatters(i - 2)

        prev = jnp.maximum(i - 1, 0)
        steady = (i >= 1) & (i + 1 < nu) & (nvalids[prev] == m)

        @pl.when(steady)
        def _():
            issue_gathers(i + 1, unrolled=True)
            issue_scatters(i - 1, unrolled=True)
            expert()

        @pl.when(jnp.logical_not(steady))
        def _():
            @pl.when(i + 1 < nu)
            def _():
                issue_gathers(i + 1)

            @pl.when(i >= 1)
            def _():
                issue_scatters(i - 1)

            expert()

        @pl.when(i == nu - 1)
        def _():
            issue_scatters(i)

            @pl.when(i >= 1)
            def _():
                wait_scatters(i - 1)

            wait_scatters(i)


def moe_experts(order, plan, xn, w_gate, w_up, w_down, layer):
    blk_exp, q0s, nvalids, nused = plan
    n, d = xn.shape
    m = MOE_BLOCK
    de = w_gate.shape[3]
    w_in_spec = pl.BlockSpec((1, 1, d, de), lambda i, be, q0, nv, nu: (layer, be[i], 0, 0))
    return pl.pallas_call(
        _moe_kernel,
        out_shape=jax.ShapeDtypeStruct((2 * n, d), F32),
        grid_spec=pltpu.PrefetchScalarGridSpec(
            num_scalar_prefetch=4,
            grid=(blk_exp.shape[0],),
            in_specs=[
                pl.BlockSpec(memory_space=pl.ANY), pl.BlockSpec(memory_space=pl.ANY),
                w_in_spec, w_in_spec,
                pl.BlockSpec((1, 1, de, d), lambda i, be, q0, nv, nu: (layer, be[i], 0, 0)),
            ],
            out_specs=pl.BlockSpec(memory_space=pl.ANY),
            scratch_shapes=[
                pltpu.SMEM((IDX_RING * 2 * IDX_CHUNK,), I32),
                pltpu.VMEM((2, m, d), F32), pltpu.VMEM((2, m, d), F32),
                pltpu.VMEM((d, de), BF16), pltpu.VMEM((d, de), BF16), pltpu.VMEM((de, d), BF16),
                pltpu.SemaphoreType.DMA((IDX_RING,)), pltpu.SemaphoreType.DMA((2,)), pltpu.SemaphoreType.DMA((2,)),
            ],
        ),
        compiler_params=_cparams("arbitrary"),
        name="moe_experts",
    )(blk_exp, q0s, nvalids, nused, order, xn, w_gate, w_up, w_down)


def _moe_finish_kernel(y0_ref, y1_ref, gate_ref, h_ref, g_ref, o_ref, *, final_norm):
    gate = gate_ref[...]
    out = h_ref[...] + (y0_ref[...] * gate[:, 0:1] + y1_ref[...] * gate[:, 1:2])
    if final_norm:
        ms = jnp.mean(out * out, axis=-1, keepdims=True)
        out = out * lax.rsqrt(ms + EPS) * g_ref[...]
    o_ref[...] = out


def moe_finish(ys, gate, h, g, *, final_norm):
    n, d = h.shape
    tm = TOK_TM
    nt = n // tm
    row_blk = lambda width: pl.BlockSpec((tm, width), lambda i: (i, 0))
    return pl.pallas_call(
        functools.partial(_moe_finish_kernel, final_norm=final_norm),
        out_shape=jax.ShapeDtypeStruct((n, d), F32),
        grid=(nt,),
        in_specs=[row_blk(d), pl.BlockSpec((tm, d), lambda i: (i + nt, 0)), row_blk(LANES), row_blk(d),
                  pl.BlockSpec((1, d), lambda i: (0, 0))],
        out_specs=row_blk(d),
        compiler_params=_cparams("arbitrary"),
        name="moe_finish",
    )(ys, ys, gate, h, g.reshape(1, d))


def dispatch_plan(route, counts):
    m = MOE_BLOCK
    n = route.shape[0]
    assert (2 * n) % IDX_CHUNK == 0
    counts = counts[0, :N_EXPERTS]
    padded = ((counts + m - 1) // m) * m
    pend = jnp.cumsum(padded)
    pstart = pend - padded
    start = jnp.cumsum(counts) - counts
    eid = jnp.concatenate([route[:, ROUTE_EID], route[:, ROUTE_EID + 1]])
    order = jnp.argsort(eid, stable=True).astype(I32)
    order = jnp.concatenate([order, jnp.zeros((2 * IDX_CHUNK,), I32)])
    nblk = (2 * n + N_EXPERTS * m) // m
    blk_start = jnp.arange(nblk, dtype=I32) * m
    nused = pend[-1] // m
    used = jnp.arange(nblk) < nused
    e = jnp.minimum(jnp.sum(blk_start[:, None] >= pend[None, :], axis=1), N_EXPERTS - 1)
    is_e = e[:, None] == jnp.arange(N_EXPERTS)[None, :]
    of_block = lambda table: jnp.sum(jnp.where(is_e, table[None, :], 0), axis=1)
    r0 = blk_start - of_block(pstart)
    q0s = jnp.where(used, of_block(start) + r0, 0)
    nvalids = jnp.where(used, jnp.clip(of_block(counts) - r0, 0, m), 0)
    last_exp = jnp.sum(jnp.where(jnp.arange(nblk) == nused - 1, e, 0))
    blk_exp = jnp.where(used, e, last_exp)
    as_i32 = lambda t: t.astype(I32)
    return order, (as_i32(blk_exp), as_i32(q0s), as_i32(nvalids), as_i32(nused).reshape(1))


def moe_layer(h1, xn, route, gate, counts, w_gate, w_up, w_down, layer, norm_g, *, final_norm):
    order, plan = dispatch_plan(route, counts)
    ys = moe_experts(order, plan, xn, w_gate, w_up, w_down, layer)
    return moe_finish(ys, gate, h1, norm_g, final_norm=final_norm)


def kernel(x, attn_norm_g, ffn_norm_g, w_in_ab, w_out_ab, ret_gn_g, w_in_c, b_in_c, sinks, w_out_c, b_out_c,
           w_group, w_expert_router, w_gate, w_up, w_down, final_norm_g):
    batch, seq, d = x.shape
    n = batch * seq
    depth = attn_norm_g.shape[0]
    cos, sin = rope_lane_tables(seq)
    later = (jnp.arange(min(SB_TILE, seq))[:, None] > jnp.arange(min(SB_TILE, seq))[None, :]).astype(BF16)
    w_sb = N_HEADS_SB * HEAD_DIM
    w_ret = N_HEADS_RET * HEAD_DIM
    h = x.reshape(n, d)
    for layer in range(depth):
        i = layer // 2
        last = layer == depth - 1
        if layer % 2 == 0:
            proj = norm_proj(h, attn_norm_g[layer], w_in_ab[i].astype(BF16), jnp.zeros((w_in_ab.shape[2],), F32),
                             cos, sin, seq=seq, tn=w_sb,
                             col_ops=("scale", "", "", "rope", "rope scale", "", ""))
            per = w_sb // LANES
            a = sb_attention(proj, later, batch=batch, seq=seq, q_blk=0, k_blk=per, v_blk=2 * per)
            r = retention(proj, ret_gn_g[i], batch=batch, seq=seq,
                          q_blk=3 * per, k_blk=4 * per, v_blk=5 * per, g_blk=6 * per)
            w_out = w_out_ab[i].astype(BF16)
            lhs, ws = [a, r], [w_out[:w_sb], w_out[w_sb:]]
            bias = jnp.zeros((d,), F32)
        else:
            wkv = N_KV_SWA * HEAD_DIM
            nq = N_HEADS_SWA * HEAD_DIM // wkv
            proj = norm_proj(h, attn_norm_g[layer], w_in_c[i].astype(BF16), b_in_c[i].astype(F32),
                             cos, sin, seq=seq, tn=wkv,
                             col_ops=("rope scale",) * nq + ("rope", ""))
            o = swa_attention(proj, sinks[i], batch=batch, seq=seq)
            lhs, ws = [o], [w_out_c[i].astype(BF16)]
            bias = b_out_c[i].astype(F32)
        h1, xn, route, gate, counts = outproj_router(
            lhs, ws, bias, h, ffn_norm_g[layer], router_weights(w_group[layer], w_expert_router[layer]))
        h = moe_layer(h1, xn, route, gate, counts, w_gate, w_up, w_down, layer, final_norm_g, final_norm=last)
    return h.reshape(batch, seq, d)
```

```python
import functools

import jax
import jax.numpy as jnp
import numpy as np
from jax import lax
from jax.experimental import pallas as pl
from jax.experimental.pallas import tpu as pltpu

F32 = jnp.float32
BF16 = jnp.bfloat16
I32 = jnp.int32

HEAD_DIM = 64
N_HEADS_SB = 8
N_HEADS_RET = 8
N_HEADS_SWA = 16
N_KV_SWA = 4
WINDOW = 128
ROPE_THETA = 10000.0
N_GROUPS = 4
EXPERTS_PER_GROUP = 8
N_EXPERTS = N_GROUPS * EXPERTS_PER_GROUP
MOE_BLOCK = 256
EPS = 1e-6

LANES = 128
HALF = HEAD_DIM // 2
QK_SCALE = HEAD_DIM ** -0.5

SB_TILE = 256
RET_CHUNK = 256
PROJ_TM = 512
TOK_TM = 512
ROUTER_TM = 512
ROUTER_SUB = 256
VMEM_LIMIT = 48 * 1024 * 1024

_NT = (((1,), (1,)), ((), ()))
_TN = (((0,), (0,)), ((), ()))


def _cparams(*sem):
    return pltpu.CompilerParams(dimension_semantics=sem, vmem_limit_bytes=VMEM_LIMIT)


def _head0_mask():
    return lax.broadcasted_iota(I32, (1, LANES), 1) < HEAD_DIM


def _norm_proj_kernel(x_ref, g_ref, w_ref, b_ref, cos_ref, sin_ref, o_ref, *, tn, col_ops):
    x = x_ref[...]
    ms = jnp.mean(x * x, axis=-1, keepdims=True)
    xn = (x * lax.rsqrt(ms + EPS) * g_ref[...]).astype(BF16)
    lane = lax.broadcasted_iota(I32, (1, LANES), 1)
    first_half = (lane % HEAD_DIM) < HALF
    for j, op in enumerate(col_ops):
        cols = slice(j * tn, (j + 1) * tn)
        acc = jnp.dot(xn, w_ref[:, cols], preferred_element_type=F32) + b_ref[:, cols]
        if "rope" in op:
            cos = cos_ref[...]
            sin = sin_ref[...]
            slabs = []
            for s in range(tn // LANES):
                a = acc[:, s * LANES:(s + 1) * LANES]
                partner = jnp.where(first_half, pltpu.roll(a, LANES - HALF, 1), pltpu.roll(a, HALF, 1))
                slabs.append(a * cos + partner * sin)
            acc = jnp.concatenate(slabs, axis=1)
        if "scale" in op:
            acc = acc * QK_SCALE
        o_ref[:, cols] = acc.astype(BF16)


def norm_proj(x, g, w, b, cos, sin, *, seq, tn, col_ops):
    n, d = x.shape
    f = w.shape[1]
    tm = min(PROJ_TM, seq)
    assert n % tm == 0 and seq % tm == 0 and f == tn * len(col_ops)
    pos_blocks = seq // tm
    return pl.pallas_call(
        functools.partial(_norm_proj_kernel, tn=tn, col_ops=col_ops),
        out_shape=jax.ShapeDtypeStruct((n, f), BF16),
        grid=(n // tm,),
        in_specs=[
            pl.BlockSpec((tm, d), lambda i: (i, 0)),
            pl.BlockSpec((1, d), lambda i: (0, 0)),
            pl.BlockSpec((d, f), lambda i: (0, 0)),
            pl.BlockSpec((1, f), lambda i: (0, 0)),
            pl.BlockSpec((tm, LANES), lambda i: (i % pos_blocks, 0)),
            pl.BlockSpec((tm, LANES), lambda i: (i % pos_blocks, 0)),
        ],
        out_specs=pl.BlockSpec((tm, f), lambda i: (i, 0)),
        compiler_params=_cparams("arbitrary"),
        name="norm_proj",
    )(x, g.reshape(1, d), w, b.reshape(1, f), cos, sin)


def rope_lane_tables(seq):
    pos = jnp.arange(seq, dtype=F32)
    inv = ROPE_THETA ** (-jnp.arange(0, HEAD_DIM, 2, dtype=F32) / HEAD_DIM)
    ang = pos[:, None] * inv[None, :]
    cos, sin = jnp.cos(ang), jnp.sin(ang)
    return jnp.tile(cos, (1, 4)), jnp.tile(jnp.concatenate([-sin, sin], axis=1), (1, 2))


SB_PAIRS = 4


def _sb_kernel(q_ref, k_ref, v_ref, t_ref, o_ref, acc_ref, carry_ref, *, tile):
    qi = pl.program_id(2)
    m0 = _head0_mask()
    zero = jnp.zeros((tile, LANES), BF16)
    row = lax.broadcasted_iota(I32, (tile, tile), 0)
    col = lax.broadcasted_iota(I32, (tile, tile), 1)
    strict = col < row

    def key_tile(kb, diag):
        start = pl.multiple_of(kb * tile, tile)
        for p in range(SB_PAIRS):
            lanes = slice(p * LANES, (p + 1) * LANES)
            q = q_ref[:, lanes]
            k = k_ref[pl.ds(start, tile), lanes]
            v = v_ref[pl.ds(start, tile), lanes]
            out = None
            for h in range(2):
                qh = jnp.where(m0, q, zero) if h == 0 else jnp.where(m0, zero, q)
                vh = jnp.where(m0, v, zero) if h == 0 else jnp.where(m0, zero, v)
                z = lax.dot_general(qh, k, _NT, preferred_element_type=F32)
                log_fail = -(jnp.maximum(z, 0.0) + jnp.log(1.0 + jnp.exp(-jnp.abs(z))))
                if diag:
                    log_fail = jnp.where(strict, log_fail, 0.0)
                after = jnp.dot(log_fail.astype(BF16), t_ref[...], preferred_element_type=F32)
                logw = z + log_fail + after
                if not diag:
                    logw = logw + carry_ref[2 * p + h]
                w = jnp.exp(logw)
                if diag:
                    w = jnp.where(strict, w, 0.0)
                pv = jnp.dot(w.astype(BF16), vh, preferred_element_type=F32)
                out = pv if out is None else out + pv
                tile_sum = jnp.sum(log_fail, axis=-1, keepdims=True)
                if diag:
                    carry_ref[2 * p + h] = tile_sum
                else:
                    carry_ref[2 * p + h] += tile_sum
            if diag:
                acc_ref[:, lanes] = out
            else:
                acc_ref[:, lanes] += out

    key_tile(qi, True)

    def body(j, c):
        key_tile(qi - 1 - j, False)
        return c

    lax.fori_loop(0, qi, body, 0)
    o_ref[...] = acc_ref[...].astype(BF16)


def sb_attention(proj, later_mat, *, batch, seq, q_blk, k_blk, v_blk):
    n = proj.shape[0]
    tile = min(SB_TILE, seq)
    nq = seq // tile
    width = SB_PAIRS * LANES
    steps = N_HEADS_SB * HEAD_DIM // width
    assert q_blk % SB_PAIRS == 0 and k_blk % SB_PAIRS == 0 and v_blk % SB_PAIRS == 0
    col = lambda blk: blk // SB_PAIRS
    return pl.pallas_call(
        functools.partial(_sb_kernel, tile=tile),
        out_shape=jax.ShapeDtypeStruct((n, steps * width), BF16),
        grid=(batch, steps, nq),
        in_specs=[
            pl.BlockSpec((tile, width), lambda b, p, i: (b * nq + i, col(q_blk) + p)),
            pl.BlockSpec((seq, width), lambda b, p, i: (b, col(k_blk) + p)),
            pl.BlockSpec((seq, width), lambda b, p, i: (b, col(v_blk) + p)),
            pl.BlockSpec((tile, tile), lambda b, p, i: (0, 0)),
        ],
        out_specs=pl.BlockSpec((tile, width), lambda b, p, i: (b * nq + i, p)),
        scratch_shapes=[pltpu.VMEM((tile, width), F32), pltpu.VMEM((2 * SB_PAIRS, tile, 1), F32)],
        compiler_params=_cparams("arbitrary", "arbitrary", "arbitrary"),
        name="sb_attention",
    )(proj, proj, proj, later_mat)


RET_PAIRS = 4


def _retention_kernel(q_ref, k_ref, v_ref, gate_ref, gain_ref, dec_ref, xi_ref, zeta_ref, gch_ref,
                      o_ref, state_ref):
    c = pl.program_id(2)

    @pl.when(c == 0)
    def _():
        state_ref[...] = jnp.zeros_like(state_ref)

    m0 = _head0_mask()
    zero = jnp.zeros((q_ref.shape[0], LANES), BF16)
    r = lax.broadcasted_iota(I32, (LANES, LANES), 0) < HEAD_DIM
    cc = lax.broadcasted_iota(I32, (LANES, LANES), 1) < HEAD_DIM
    same_head = r == cc

    def head_mean(t):
        s0 = jnp.sum(jnp.where(m0, t, 0.0), axis=-1, keepdims=True)
        s1 = jnp.sum(jnp.where(m0, 0.0, t), axis=-1, keepdims=True)
        return jnp.where(m0, s0, s1) * (1.0 / HEAD_DIM)

    for p in range(RET_PAIRS):
        lanes = slice(p * LANES, (p + 1) * LANES)
        q = q_ref[:, lanes]
        k = k_ref[:, lanes]
        v = v_ref[:, lanes]
        state = state_ref[p]
        y = jnp.dot(q, state.astype(BF16), preferred_element_type=F32) * xi_ref[:, lanes]
        for h in range(2):
            qh = jnp.where(m0, q, zero) if h == 0 else jnp.where(m0, zero, q)
            vh = jnp.where(m0, v, zero) if h == 0 else jnp.where(m0, zero, v)
            s = lax.dot_general(qh, k, _NT, preferred_element_type=F32)
            inner = (s * dec_ref[2 * p + h]).astype(BF16)
            y = y + jnp.dot(inner, vh, preferred_element_type=F32)

        kz = (k.astype(F32) * zeta_ref[:, lanes]).astype(BF16)
        upd = lax.dot_general(kz, v, _TN, preferred_element_type=F32)
        state_ref[p] = state * gch_ref[:, lanes] + jnp.where(same_head, upd, 0.0)

        d = y - head_mean(y)
        yn = d * lax.rsqrt(head_mean(d * d) + EPS) * gain_ref[:, lanes]
        g = gate_ref[:, lanes].astype(F32)
        o_ref[:, lanes] = (yn * (g * (1.0 / (1.0 + jnp.exp(-g))))).astype(BF16)


def retention_tables(chunk):
    h = N_HEADS_RET
    log_g = jnp.log(1.0 - 2.0 ** (-5.0 - jnp.arange(h, dtype=F32)))
    idx = jnp.arange(chunk, dtype=F32)
    diff = idx[:, None] - idx[None, :]
    dec = jnp.where(diff[None] >= 0, jnp.exp(log_g[:, None, None] * jnp.maximum(diff, 0.0)[None]), 0.0)
    xi = jnp.exp(log_g[:, None] * (idx[None, :] + 1.0))
    zeta = jnp.exp(log_g[:, None] * (chunk - 1.0 - idx[None, :]))
    gch = jnp.exp(log_g * chunk)
    lanes = lambda t: jnp.repeat(t.T, HEAD_DIM, axis=1)
    return dec, lanes(xi), lanes(zeta), jnp.repeat(gch, HEAD_DIM)[None, :]


def retention(proj, gain, *, batch, seq, q_blk, k_blk, v_blk, g_blk):
    n = proj.shape[0]
    chunk = min(RET_CHUNK, seq)
    nc = seq // chunk
    width = RET_PAIRS * LANES
    steps = N_HEADS_RET * HEAD_DIM // width
    assert all(b % RET_PAIRS == 0 for b in (q_blk, k_blk, v_blk, g_blk))
    dec, xi, zeta, gch = retention_tables(chunk)
    blk = lambda off: pl.BlockSpec((chunk, width), lambda b, p, c: (b * nc + c, off // RET_PAIRS + p))
    return pl.pallas_call(
        _retention_kernel,
        out_shape=jax.ShapeDtypeStruct((n, steps * width), BF16),
        grid=(batch, steps, nc),
        in_specs=[
            blk(q_blk), blk(k_blk), blk(v_blk), blk(g_blk),
            pl.BlockSpec((1, width), lambda b, p, c: (0, p)),
            pl.BlockSpec((2 * RET_PAIRS, chunk, chunk), lambda b, p, c: (p, 0, 0)),
            pl.BlockSpec((chunk, width), lambda b, p, c: (0, p)),
            pl.BlockSpec((chunk, width), lambda b, p, c: (0, p)),
            pl.BlockSpec((1, width), lambda b, p, c: (0, p)),
        ],
        out_specs=pl.BlockSpec((chunk, width), lambda b, p, c: (b * nc + c, p)),
        scratch_shapes=[pltpu.VMEM((RET_PAIRS, LANES, LANES), F32)],
        compiler_params=_cparams("arbitrary", "arbitrary", "arbitrary"),
        name="retention",
    )(proj, proj, proj, proj, gain.reshape(1, -1).astype(F32), dec, xi, zeta, gch)


def _swa_kernel(sink_ref, q_ref, kp_ref, kc_ref, vp_ref, vc_ref, o_ref):
    blk = pl.program_id(1)
    m0 = _head0_mask()
    w = WINDOW
    qi = lax.broadcasted_iota(I32, (w, 2 * w), 0)
    kj = lax.broadcasted_iota(I32, (w, 2 * w), 1)
    rel = qi + w - kj
    valid = (rel >= 0) & (rel < w) & ((blk > 0) | (kj >= w))
    kband = jnp.concatenate([kp_ref[...], kc_ref[...]], axis=0)
    vband = jnp.concatenate([vp_ref[...], vc_ref[...]], axis=0)
    group = N_HEADS_SWA // N_KV_SWA
    slabs = group * HEAD_DIM // LANES
    swap = lambda t: jnp.concatenate([t[:, HEAD_DIM:], t[:, :HEAD_DIM]], axis=1)
    zero_kv = jnp.zeros((2 * w, LANES), BF16)
    zero_q = jnp.zeros((slabs * w, LANES), BF16)
    ones_kv = jnp.ones((2 * w, LANES), BF16)

    def half_heads(q_stack, k_half, v_half, first_head):
        s = lax.dot_general(q_stack, k_half, _NT, preferred_element_type=F32)
        probs, sink_terms = [], []
        for slab in range(slabs):
            sink = sink_ref[first_head + 2 * slab]
            sh = jnp.where(valid, s[slab * w:(slab + 1) * w], -jnp.inf)
            m = jnp.maximum(jnp.max(sh, axis=-1, keepdims=True), sink)
            probs.append(jnp.exp(sh - m).astype(BF16))
            sink_terms.append(jnp.exp(sink - jnp.broadcast_to(m, (w, LANES))))
        p = jnp.concatenate(probs, axis=0)
        pv = jnp.dot(p, jnp.concatenate([v_half, ones_kv], axis=1), preferred_element_type=F32)
        denom = pv[:, LANES:] + jnp.concatenate(sink_terms, axis=0)
        return pv[:, :LANES] / denom

    for pair in range(N_KV_SWA * HEAD_DIM // LANES):
        k2 = kband[:, pair * LANES:(pair + 1) * LANES]
        v2 = vband[:, pair * LANES:(pair + 1) * LANES]
        k2s, v2s = swap(k2), swap(v2)
        for c in range(2):
            kvh = 2 * pair + c
            k_even, k_odd = (k2, k2s) if c == 0 else (k2s, k2)
            v_even, v_odd = (v2, v2s) if c == 0 else (v2s, v2)
            base = kvh * group * HEAD_DIM
            q_stack = jnp.concatenate([q_ref[:, base + t * LANES:base + (t + 1) * LANES] for t in range(slabs)],
                                      axis=0)
            out = (half_heads(jnp.where(m0, q_stack, zero_q), k_even, jnp.where(m0, v_even, zero_kv), kvh * group)
                   + half_heads(jnp.where(m0, zero_q, q_stack), k_odd, jnp.where(m0, zero_kv, v_odd),
                                kvh * group + 1))
            for t in range(slabs):
                o_ref[:, base + t * LANES:base + (t + 1) * LANES] = out[t * w:(t + 1) * w].astype(BF16)


def swa_attention(proj, sinks, *, batch, seq):
    n = proj.shape[0]
    w = WINDOW
    nb = seq // w
    wq = N_HEADS_SWA * HEAD_DIM
    wkv = N_KV_SWA * HEAD_DIM
    k_blk = wq // wkv
    cur = lambda off: pl.BlockSpec((w, wkv), lambda b, i: (b * nb + i, off))
    prev = lambda off: pl.BlockSpec((w, wkv), lambda b, i: (b * nb + jnp.maximum(i - 1, 0), off))
    return pl.pallas_call(
        _swa_kernel,
        out_shape=jax.ShapeDtypeStruct((n, wq), BF16),
        grid=(batch, nb),
        in_specs=[
            pl.BlockSpec(memory_space=pltpu.SMEM),
            pl.BlockSpec((w, wq), lambda b, i: (b * nb + i, 0)),
            prev(k_blk), cur(k_blk), prev(k_blk + 1), cur(k_blk + 1),
        ],
        out_specs=pl.BlockSpec((w, wq), lambda b, i: (b * nb + i, 0)),
        compiler_params=_cparams("arbitrary", "arbitrary"),
        name="swa_attention",
    )(sinks.astype(F32), proj, proj, proj, proj, proj)


ROUTE_EID = 0
GROUP_LANE = N_EXPERTS


def _split_bf16(t):
    hi = t.astype(BF16)
    return hi, (t - hi.astype(F32)).astype(BF16)


def _outproj_router_kernel(*refs, n_lhs):
    lhs = refs[:n_lhs]
    ws = refs[n_lhs:2 * n_lhs]
    b_ref, h_ref, g_ref, wr_ref = refs[2 * n_lhs:2 * n_lhs + 4]
    h1_ref, xn_ref, route_ref, gate_ref, cnt_ref, carry_ref = refs[2 * n_lhs + 4:]
    step = pl.program_id(0)

    @pl.when(step == 0)
    def _():
        carry_ref[...] = jnp.zeros_like(carry_ref)

    w_split = jnp.concatenate(_split_bf16(wr_ref[...]), axis=1)
    for sub in range(h_ref.shape[0] // ROUTER_SUB):
        rows = slice(sub * ROUTER_SUB, (sub + 1) * ROUTER_SUB)
        mix = b_ref[...]
        for a_ref, w_ref in zip(lhs, ws):
            mix = mix + jnp.dot(a_ref[rows], w_ref[...], preferred_element_type=F32)
        h1 = h_ref[rows] + mix
        h1_ref[rows] = h1
        ms = jnp.mean(h1 * h1, axis=-1, keepdims=True)
        xn = h1 * lax.rsqrt(ms + EPS) * g_ref[...]
        xn_ref[rows] = xn
        x_hi, x_lo = _split_bf16(xn)
        parts = (jnp.dot(x_hi, w_split, preferred_element_type=F32)
                 + jnp.dot(x_lo, w_split, preferred_element_type=F32))
        logits = parts[:, :LANES] + parts[:, LANES:]
        route, gate, cnt = _route(logits)
        route_ref[rows] = route
        gate_ref[rows] = gate
        carry_ref[...] += cnt
    cnt_ref[...] = jnp.broadcast_to(carry_ref[...], cnt_ref.shape).astype(I32)


def _route(logits):
    tm = logits.shape[0]
    lane = lax.broadcasted_iota(I32, (tm, LANES), 1)
    lane_f = lane.astype(F32)
    neg = -jnp.inf

    def lane_max(t):
        return jnp.max(t, axis=-1, keepdims=True)

    def lane_sum(t):
        return jnp.sum(t, axis=-1, keepdims=True)

    def first_lane_of(t, value, mask):
        return jnp.min(jnp.where(mask & (t == value), lane_f, float(LANES)), axis=-1, keepdims=True)

    is_group = (lane >= GROUP_LANE) & (lane < GROUP_LANE + N_GROUPS)
    gl = jnp.where(is_group, logits, neg)
    ge = jnp.exp(gl - lane_max(gl))
    gp = ge / lane_sum(ge)
    g_prob = lane_max(gp)
    g_idx = first_lane_of(gp, g_prob, is_group) - float(GROUP_LANE)
    group_of_lane = lax.shift_right_logical(lane, int(np.log2(EXPERTS_PER_GROUP))).astype(F32)
    in_group = (lane < N_EXPERTS) & (group_of_lane == g_idx)
    el = jnp.where(in_group, logits, neg)
    ee = jnp.exp(el - lane_max(el))
    ep = ee / lane_sum(ee)
    p1 = lane_max(ep)
    i1 = first_lane_of(ep, p1, in_group)
    rest = in_group & (lane_f != i1)
    ep2 = jnp.where(rest, ep, neg)
    p2 = lane_max(ep2)
    i2 = first_lane_of(ep2, p2, rest)
    denom = p1 + p2
    gate1 = g_prob * (p1 / denom)
    gate2 = g_prob * (p2 / denom)

    cnt = jnp.where((lane_f == i1) | (lane_f == i2), 1.0, 0.0)
    route = jnp.where(lane == ROUTE_EID, i1, 0.0)
    route = jnp.where(lane == ROUTE_EID + 1, i2, route)
    gate = jnp.where(lane == 0, gate1, jnp.where(lane == 1, gate2, 0.0))
    return route.astype(I32), gate, jnp.sum(cnt, axis=0, keepdims=True)


def outproj_router(lhs, ws, bias, h, g, w_router):
    n, d = h.shape
    tm = ROUTER_TM
    assert n % tm == 0
    row_blk = lambda width: pl.BlockSpec((tm, width), lambda i: (i, 0))
    full = lambda a: pl.BlockSpec(a.shape, lambda i: (0, 0))
    bias2, g2 = bias.reshape(1, d), g.reshape(1, d)
    args = [*lhs, *ws, bias2, h, g2, w_router]
    in_specs = ([row_blk(a.shape[1]) for a in lhs] + [full(w) for w in ws]
                + [full(bias2), row_blk(d), full(g2), full(w_router)])
    return pl.pallas_call(
        functools.partial(_outproj_router_kernel, n_lhs=len(lhs)),
        out_shape=(
            jax.ShapeDtypeStruct((n, d), F32), jax.ShapeDtypeStruct((n, d), F32),
            jax.ShapeDtypeStruct((n, LANES), I32), jax.ShapeDtypeStruct((n, LANES), F32),
            jax.ShapeDtypeStruct((8, LANES), I32),
        ),
        grid=(n // tm,),
        in_specs=in_specs,
        out_specs=(row_blk(d), row_blk(d), row_blk(LANES), row_blk(LANES),
                   pl.BlockSpec((8, LANES), lambda i: (0, 0))),
        scratch_shapes=[pltpu.VMEM((1, LANES), F32)],
        compiler_params=_cparams("arbitrary"),
        name="outproj_router",
    )(*args)


def router_weights(w_group, w_expert_router):
    d = w_group.shape[0]
    pad = jnp.zeros((d, LANES - N_EXPERTS - N_GROUPS), F32)
    return jnp.concatenate([w_expert_router.astype(F32), w_group.astype(F32), pad], axis=1)


IDX_CHUNK = 1024
IDX_RING = 4
DMA_UNROLL = 8


def _row_copy(src, src_row, dst, dst_row, sem):
    return pltpu.make_async_copy(src.at[pl.ds(src_row, 1), :], dst.at[pl.ds(dst_row, 1), :], sem)


def _moe_kernel(blk_exp, q0s, nvalids, nused, order_hbm, xn_hbm, wg_ref, wu_ref, wd_ref, out_hbm,
                ibuf, xbuf, ybuf, wg_bf, wu_bf, wd_bf, isem, gsem, ssem, *, gather_threads):
    i = pl.program_id(0)
    nu = nused[0]
    m = MOE_BLOCK
    n_tok = xn_hbm.shape[0]

    window = 2 * IDX_CHUNK

    def idx_copy(blk):
        base = pl.multiple_of(q0s[blk] & ~(IDX_CHUNK - 1), IDX_CHUNK)
        slot = blk & (IDX_RING - 1)
        dst = ibuf.at[pl.ds(pl.multiple_of(slot * window, window), window)]
        return pltpu.make_async_copy(order_hbm.at[pl.ds(base, window)], dst, isem.at[slot])

    def pair_base(blk):
        return (blk & (IDX_RING - 1)) * window + (q0s[blk] & (IDX_CHUNK - 1))

    def tile_rows(buf, g):
        return buf.at[pl.ds(pl.multiple_of(g * DMA_UNROLL, DMA_UNROLL), DMA_UNROLL), :]

    def gather_group(blk_base, dst, sem, g):
        rows = tile_rows(dst, g)
        for u in range(DMA_UNROLL):
            pair = ibuf[blk_base + g * DMA_UNROLL + u]
            tok = jnp.where(pair >= n_tok, pair - n_tok, pair)
            _row_copy(xn_hbm, tok, rows, u, sem).start(priority=u % gather_threads)

    def scatter_group(blk_base, src, sem, g):
        rows = tile_rows(src, g)
        for u in range(DMA_UNROLL):
            _row_copy(rows, u, out_hbm, ibuf[blk_base + g * DMA_UNROLL + u], sem).start(priority=u % 2)

    def issue_gathers(blk, unrolled=False):
        slot = blk & 1
        args = (pair_base(blk), xbuf.at[slot], gsem.at[slot])
        if unrolled:
            for g in range(m // DMA_UNROLL):
                gather_group(*args, g)
        else:
            lax.fori_loop(0, m // DMA_UNROLL, lambda g, c: (gather_group(*args, g), c)[1], 0)

    def wait_gathers(blk):
        slot = blk & 1
        pltpu.make_async_copy(xn_hbm.at[pl.ds(0, m), :], xbuf.at[slot], gsem.at[slot]).wait()

    def issue_scatters(blk, unrolled=False):
        slot = blk & 1
        base = pair_base(blk)
        src = ybuf.at[slot]
        sem = ssem.at[slot]
        if unrolled:
            for g in range(m // DMA_UNROLL):
                scatter_group(base, src, sem, g)
            return
        nv = nvalids[blk]
        groups = lax.shift_right_logical(nv, DMA_UNROLL.bit_length() - 1)

        def tail(r, c):
            _row_copy(src, r, out_hbm, ibuf[base + r], sem).start()
            return c

        lax.fori_loop(0, groups, lambda g, c: (scatter_group(base, src, sem, g), c)[1], 0)
        lax.fori_loop(groups * DMA_UNROLL, nv, tail, 0)

    def wait_scatters(blk):
        slot = blk & 1
        nv = nvalids[blk]
        whole = pl.multiple_of(nv & ~(DMA_UNROLL - 1), DMA_UNROLL)

        @pl.when(whole > 0)
        def _():
            pltpu.make_async_copy(ybuf.at[slot].at[pl.ds(0, whole), :], out_hbm.at[pl.ds(0, whole), :],
                                  ssem.at[slot]).wait()

        def tail(r, c):
            _row_copy(ybuf.at[slot], 0, out_hbm, 0, ssem.at[slot]).wait()
            return c

        lax.fori_loop(whole, nv, tail, 0)

    @pl.when(i == 0)
    def _():
        first = idx_copy(0)
        first.start()
        first.wait()
        issue_gathers(0)

        @pl.when(nu > 1)
        def _():
            idx_copy(1).start()

    def expert():
        x = xbuf[i & 1].astype(BF16)
        gate = jnp.dot(x, wg_bf[...], preferred_element_type=F32)
        up = jnp.dot(x, wu_bf[...], preferred_element_type=F32)
        hidden = (gate * (1.0 / (1.0 + jnp.exp(-gate))) * up).astype(BF16)
        ybuf[i & 1] = jnp.dot(hidden, wd_bf[...], preferred_element_type=F32)

    @pl.when(i < nu)
    def _():
        @pl.when(i + 2 < nu)
        def _():
            idx_copy(i + 2).start()

        @pl.when(i + 1 < nu)
        def _():
            idx_copy(i + 1).wait()

        wait_gathers(i)

        @pl.when((i == 0) | (blk_exp[i] != blk_exp[jnp.maximum(i - 1, 0)]))
        def _():
            wg_bf[...] = wg_ref[0, 0].astype(BF16)
            wu_bf[...] = wu_ref[0, 0].astype(BF16)
            wd_bf[...] = wd_ref[0, 0].astype(BF16)

        @pl.when(i >= 2)
        def _():
            wait_scatters(i - 2)

        prev = jnp.maximum(i - 1, 0)
        steady = (i >= 1) & (i + 1 < nu) & (nvalids[prev] == m)

        @pl.when(steady)
        def _():
            issue_gathers(i + 1, unrolled=True)
            issue_scatters(i - 1, unrolled=True)
            expert()

        @pl.when(jnp.logical_not(steady))
        def _():
            @pl.when(i + 1 < nu)
            def _():
                issue_gathers(i + 1)

            @pl.when(i >= 1)
            def _():
                issue_scatters(i - 1)

            expert()

        @pl.when(i == nu - 1)
        def _():
            issue_scatters(i)

            @pl.when(i >= 1)
            def _():
                wait_scatters(i - 1)

            wait_scatters(i)


def moe_experts(order, plan, xn, w_gate, w_up, w_down, layer):
    blk_exp, q0s, nvalids, nused = plan
    n, d = xn.shape
    m = MOE_BLOCK
    de = w_gate.shape[3]
    w_in_spec = pl.BlockSpec((1, 1, d, de), lambda i, be, q0, nv, nu: (layer, be[i], 0, 0))
    return pl.pallas_call(
        functools.partial(_moe_kernel, gather_threads=1 + layer % 2),
        out_shape=jax.ShapeDtypeStruct((2 * n, d), F32),
        grid_spec=pltpu.PrefetchScalarGridSpec(
            num_scalar_prefetch=4,
            grid=(blk_exp.shape[0],),
            in_specs=[
                pl.BlockSpec(memory_space=pl.ANY), pl.BlockSpec(memory_space=pl.ANY),
                w_in_spec, w_in_spec,
                pl.BlockSpec((1, 1, de, d), lambda i, be, q0, nv, nu: (layer, be[i], 0, 0)),
            ],
            out_specs=pl.BlockSpec(memory_space=pl.ANY),
            scratch_shapes=[
                pltpu.SMEM((IDX_RING * 2 * IDX_CHUNK,), I32),
                pltpu.VMEM((2, m, d), F32), pltpu.VMEM((2, m, d), F32),
                pltpu.VMEM((d, de), BF16), pltpu.VMEM((d, de), BF16), pltpu.VMEM((de, d), BF16),
                pltpu.SemaphoreType.DMA((IDX_RING,)), pltpu.SemaphoreType.DMA((2,)), pltpu.SemaphoreType.DMA((2,)),
            ],
        ),
        compiler_params=_cparams("arbitrary"),
        name="moe_experts",
    )(blk_exp, q0s, nvalids, nused, order, xn, w_gate, w_up, w_down)


def _moe_finish_kernel(y0_ref, y1_ref, gate_ref, h_ref, g_ref, o_ref, *, final_norm):
    gate = gate_ref[...]
    out = h_ref[...] + (y0_ref[...] * gate[:, 0:1] + y1_ref[...] * gate[:, 1:2])
    if final_norm:
        ms = jnp.mean(out * out, axis=-1, keepdims=True)
        out = out * lax.rsqrt(ms + EPS) * g_ref[...]
    o_ref[...] = out


def moe_finish(ys, gate, h, g, *, final_norm):
    n, d = h.shape
    tm = TOK_TM
    nt = n // tm
    row_blk = lambda width: pl.BlockSpec((tm, width), lambda i: (i, 0))
    return pl.pallas_call(
        functools.partial(_moe_finish_kernel, final_norm=final_norm),
        out_shape=jax.ShapeDtypeStruct((n, d), F32),
        grid=(nt,),
        in_specs=[row_blk(d), pl.BlockSpec((tm, d), lambda i: (i + nt, 0)), row_blk(LANES), row_blk(d),
                  pl.BlockSpec((1, d), lambda i: (0, 0))],
        out_specs=row_blk(d),
        compiler_params=_cparams("arbitrary"),
        name="moe_finish",
    )(ys, ys, gate, h, g.reshape(1, d))


def dispatch_plan(route, counts):
    m = MOE_BLOCK
    n = route.shape[0]
    assert (2 * n) % IDX_CHUNK == 0
    counts = counts[0, :N_EXPERTS]
    padded = ((counts + m - 1) // m) * m
    pend = jnp.cumsum(padded)
    pstart = pend - padded
    start = jnp.cumsum(counts) - counts
    eid = jnp.concatenate([route[:, ROUTE_EID], route[:, ROUTE_EID + 1]])
    order = jnp.argsort(eid, stable=True).astype(I32)
    order = jnp.concatenate([order, jnp.zeros((2 * IDX_CHUNK,), I32)])
    nblk = (2 * n + N_EXPERTS * m) // m
    blk_start = jnp.arange(nblk, dtype=I32) * m
    nused = pend[-1] // m
    used = jnp.arange(nblk) < nused
    e = jnp.minimum(jnp.sum(blk_start[:, None] >= pend[None, :], axis=1), N_EXPERTS - 1)
    is_e = e[:, None] == jnp.arange(N_EXPERTS)[None, :]
    of_block = lambda table: jnp.sum(jnp.where(is_e, table[None, :], 0), axis=1)
    r0 = blk_start - of_block(pstart)
    q0s = jnp.where(used, of_block(start) + r0, 0)
    nvalids = jnp.where(used, jnp.clip(of_block(counts) - r0, 0, m), 0)
    last_exp = jnp.sum(jnp.where(jnp.arange(nblk) == nused - 1, e, 0))
    blk_exp = jnp.where(used, e, last_exp)
    as_i32 = lambda t: t.astype(I32)
    return order, (as_i32(blk_exp), as_i32(q0s), as_i32(nvalids), as_i32(nused).reshape(1))


def moe_layer(h1, xn, route, gate, counts, w_gate, w_up, w_down, layer, norm_g, *, final_norm):
    order, plan = dispatch_plan(route, counts)
    ys = moe_experts(order, plan, xn, w_gate, w_up, w_down, layer)
    return moe_finish(ys, gate, h1, norm_g, final_norm=final_norm)


def kernel(x, attn_norm_g, ffn_norm_g, w_in_ab, w_out_ab, ret_gn_g, w_in_c, b_in_c, sinks, w_out_c, b_out_c,
           w_group, w_expert_router, w_gate, w_up, w_down, final_norm_g):
    batch, seq, d = x.shape
    n = batch * seq
    depth = attn_norm_g.shape[0]
    cos, sin = rope_lane_tables(seq)
    later = (jnp.arange(min(SB_TILE, seq))[:, None] > jnp.arange(min(SB_TILE, seq))[None, :]).astype(BF16)
    w_sb = N_HEADS_SB * HEAD_DIM
    w_ret = N_HEADS_RET * HEAD_DIM
    h = x.reshape(n, d)
    for layer in range(depth):
        i = layer // 2
        last = layer == depth - 1
        if layer % 2 == 0:
            proj = norm_proj(h, attn_norm_g[layer], w_in_ab[i].astype(BF16), jnp.zeros((w_in_ab.shape[2],), F32),
                             cos, sin, seq=seq, tn=w_sb,
                             col_ops=("scale", "", "", "rope", "rope scale", "", ""))
            per = w_sb // LANES
            a = sb_attention(proj, later, batch=batch, seq=seq, q_blk=0, k_blk=per, v_blk=2 * per)
            r = retention(proj, ret_gn_g[i], batch=batch, seq=seq,
                          q_blk=3 * per, k_blk=4 * per, v_blk=5 * per, g_blk=6 * per)
            w_out = w_out_ab[i].astype(BF16)
            lhs, ws = [a, r], [w_out[:w_sb], w_out[w_sb:]]
            bias = jnp.zeros((d,), F32)
        else:
            wkv = N_KV_SWA * HEAD_DIM
            nq = N_HEADS_SWA * HEAD_DIM // wkv
            proj = norm_proj(h, attn_norm_g[layer], w_in_c[i].astype(BF16), b_in_c[i].astype(F32),
                             cos, sin, seq=seq, tn=wkv,
                             col_ops=("rope scale",) * nq + ("rope", ""))
            o = swa_attention(proj, sinks[i], batch=batch, seq=seq)
            lhs, ws = [o], [w_out_c[i].astype(BF16)]
            bias = b_out_c[i].astype(F32)
        h1, xn, route, gate, counts = outproj_router(
            lhs, ws, bias, h, ffn_norm_g[layer], router_weights(w_group[layer], w_expert_router[layer]))
        h = moe_layer(h1, xn, route, gate, counts, w_gate, w_up, w_down, layer, final_norm_g, final_norm=last)
    return h.reshape(batch, seq, d)
```

```python
import functools

import jax
import jax.numpy as jnp
import numpy as np
from jax import lax
from jax.experimental import pallas as pl
from jax.experimental.pallas import tpu as pltpu

F32 = jnp.float32
BF16 = jnp.bfloat16
I32 = jnp.int32

HEAD_DIM = 64
N_HEADS_SB = 8
N_HEADS_RET = 8
N_HEADS_SWA = 16
N_KV_SWA = 4
WINDOW = 128
ROPE_THETA = 10000.0
N_GROUPS = 4
EXPERTS_PER_GROUP = 8
N_EXPERTS = N_GROUPS * EXPERTS_PER_GROUP
MOE_BLOCK = 256
EPS = 1e-6

LANES = 128
HALF = HEAD_DIM // 2
QK_SCALE = HEAD_DIM ** -0.5

SB_TILE = 256
RET_CHUNK = 256
PROJ_TM = 512
TOK_TM = 512
ROUTER_TM = 512
ROUTER_SUB = 256
VMEM_LIMIT = 48 * 1024 * 1024

_NT = (((1,), (1,)), ((), ()))
_TN = (((0,), (0,)), ((), ()))


def _cparams(*sem):
    return pltpu.CompilerParams(dimension_semantics=sem, vmem_limit_bytes=VMEM_LIMIT)


def _head0_mask():
    return lax.broadcasted_iota(I32, (1, LANES), 1) < HEAD_DIM


def _norm_proj_kernel(x_ref, g_ref, w_ref, b_ref, cos_ref, sin_ref, o_ref, *, tn, col_ops):
    x = x_ref[...]
    ms = jnp.mean(x * x, axis=-1, keepdims=True)
    xn = (x * lax.rsqrt(ms + EPS) * g_ref[...]).astype(BF16)
    lane = lax.broadcasted_iota(I32, (1, LANES), 1)
    first_half = (lane % HEAD_DIM) < HALF
    for j, op in enumerate(col_ops):
        cols = slice(j * tn, (j + 1) * tn)
        acc = jnp.dot(xn, w_ref[:, cols], preferred_element_type=F32) + b_ref[:, cols]
        if "rope" in op:
            cos = cos_ref[...]
            sin = sin_ref[...]
            slabs = []
            for s in range(tn // LANES):
                a = acc[:, s * LANES:(s + 1) * LANES]
                partner = jnp.where(first_half, pltpu.roll(a, LANES - HALF, 1), pltpu.roll(a, HALF, 1))
                slabs.append(a * cos + partner * sin)
            acc = jnp.concatenate(slabs, axis=1)
        if "scale" in op:
            acc = acc * QK_SCALE
        o_ref[:, cols] = acc.astype(BF16)


def norm_proj(x, g, w, b, cos, sin, *, seq, tn, col_ops):
    n, d = x.shape
    f = w.shape[1]
    tm = min(PROJ_TM, seq)
    assert n % tm == 0 and seq % tm == 0 and f == tn * len(col_ops)
    pos_blocks = seq // tm
    return pl.pallas_call(
        functools.partial(_norm_proj_kernel, tn=tn, col_ops=col_ops),
        out_shape=jax.ShapeDtypeStruct((n, f), BF16),
        grid=(n // tm,),
        in_specs=[
            pl.BlockSpec((tm, d), lambda i: (i, 0)),
            pl.BlockSpec((1, d), lambda i: (0, 0)),
            pl.BlockSpec((d, f), lambda i: (0, 0)),
            pl.BlockSpec((1, f), lambda i: (0, 0)),
            pl.BlockSpec((tm, LANES), lambda i: (i % pos_blocks, 0)),
            pl.BlockSpec((tm, LANES), lambda i: (i % pos_blocks, 0)),
        ],
        out_specs=pl.BlockSpec((tm, f), lambda i: (i, 0)),
        compiler_params=_cparams("arbitrary"),
        name="norm_proj",
    )(x, g.reshape(1, d), w, b.reshape(1, f), cos, sin)


def rope_lane_tables(seq):
    pos = jnp.arange(seq, dtype=F32)
    inv = ROPE_THETA ** (-jnp.arange(0, HEAD_DIM, 2, dtype=F32) / HEAD_DIM)
    ang = pos[:, None] * inv[None, :]
    cos, sin = jnp.cos(ang), jnp.sin(ang)
    return jnp.tile(cos, (1, 4)), jnp.tile(jnp.concatenate([-sin, sin], axis=1), (1, 2))


SB_PAIRS = 4


def _sb_kernel(q_ref, k_ref, v_ref, t_ref, o_ref, acc_ref, carry_ref, *, tile):
    qi = pl.program_id(2)
    m0 = _head0_mask()
    zero = jnp.zeros((tile, LANES), BF16)
    row = lax.broadcasted_iota(I32, (tile, tile), 0)
    col = lax.broadcasted_iota(I32, (tile, tile), 1)
    strict = col < row

    def key_tile(kb, diag):
        start = pl.multiple_of(kb * tile, tile)
        for p in range(SB_PAIRS):
            lanes = slice(p * LANES, (p + 1) * LANES)
            q = q_ref[:, lanes]
            k = k_ref[pl.ds(start, tile), lanes]
            v = v_ref[pl.ds(start, tile), lanes]
            out = None
            for h in range(2):
                qh = jnp.where(m0, q, zero) if h == 0 else jnp.where(m0, zero, q)
                vh = jnp.where(m0, v, zero) if h == 0 else jnp.where(m0, zero, v)
                z = lax.dot_general(qh, k, _NT, preferred_element_type=F32)
                log_fail = -(jnp.maximum(z, 0.0) + jnp.log(1.0 + jnp.exp(-jnp.abs(z))))
                if diag:
                    log_fail = jnp.where(strict, log_fail, 0.0)
                after = jnp.dot(log_fail.astype(BF16), t_ref[...], preferred_element_type=F32)
                logw = z + log_fail + after
                if not diag:
                    logw = logw + carry_ref[2 * p + h]
                w = jnp.exp(logw)
                if diag:
                    w = jnp.where(strict, w, 0.0)
                pv = jnp.dot(w.astype(BF16), vh, preferred_element_type=F32)
                out = pv if out is None else out + pv
                tile_sum = jnp.sum(log_fail, axis=-1, keepdims=True)
                if diag:
                    carry_ref[2 * p + h] = tile_sum
                else:
                    carry_ref[2 * p + h] += tile_sum
            if diag:
                acc_ref[:, lanes] = out
            else:
                acc_ref[:, lanes] += out

    key_tile(qi, True)

    def body(j, c):
        key_tile(qi - 1 - j, False)
        return c

    lax.fori_loop(0, qi, body, 0)
    o_ref[...] = acc_ref[...].astype(BF16)


def sb_attention(proj, later_mat, *, batch, seq, q_blk, k_blk, v_blk):
    n = proj.shape[0]
    tile = min(SB_TILE, seq)
    nq = seq // tile
    width = SB_PAIRS * LANES
    steps = N_HEADS_SB * HEAD_DIM // width
    assert q_blk % SB_PAIRS == 0 and k_blk % SB_PAIRS == 0 and v_blk % SB_PAIRS == 0
    col = lambda blk: blk // SB_PAIRS
    return pl.pallas_call(
        functools.partial(_sb_kernel, tile=tile),
        out_shape=jax.ShapeDtypeStruct((n, steps * width), BF16),
        grid=(batch, steps, nq),
        in_specs=[
            pl.BlockSpec((tile, width), lambda b, p, i: (b * nq + i, col(q_blk) + p)),
            pl.BlockSpec((seq, width), lambda b, p, i: (b, col(k_blk) + p)),
            pl.BlockSpec((seq, width), lambda b, p, i: (b, col(v_blk) + p)),
            pl.BlockSpec((tile, tile), lambda b, p, i: (0, 0)),
        ],
        out_specs=pl.BlockSpec((tile, width), lambda b, p, i: (b * nq + i, p)),
        scratch_shapes=[pltpu.VMEM((tile, width), F32), pltpu.VMEM((2 * SB_PAIRS, tile, 1), F32)],
        compiler_params=_cparams("arbitrary", "arbitrary", "arbitrary"),
        name="sb_attention",
    )(proj, proj, proj, later_mat)


RET_PAIRS = 4


def _retention_kernel(q_ref, k_ref, v_ref, gate_ref, gain_ref, dec_ref, xi_ref, zeta_ref, gch_ref,
                      o_ref, state_ref):
    c = pl.program_id(2)

    @pl.when(c == 0)
    def _():
        state_ref[...] = jnp.zeros_like(state_ref)

    m0 = _head0_mask()
    zero = jnp.zeros((q_ref.shape[0], LANES), BF16)
    r = lax.broadcasted_iota(I32, (LANES, LANES), 0) < HEAD_DIM
    cc = lax.broadcasted_iota(I32, (LANES, LANES), 1) < HEAD_DIM
    same_head = r == cc

    def head_mean(t):
        s0 = jnp.sum(jnp.where(m0, t, 0.0), axis=-1, keepdims=True)
        s1 = jnp.sum(jnp.where(m0, 0.0, t), axis=-1, keepdims=True)
        return jnp.where(m0, s0, s1) * (1.0 / HEAD_DIM)

    for p in range(RET_PAIRS):
        lanes = slice(p * LANES, (p + 1) * LANES)
        q = q_ref[:, lanes]
        k = k_ref[:, lanes]
        v = v_ref[:, lanes]
        state = state_ref[p]
        y = jnp.dot(q, state.astype(BF16), preferred_element_type=F32) * xi_ref[:, lanes]
        for h in range(2):
            qh = jnp.where(m0, q, zero) if h == 0 else jnp.where(m0, zero, q)
            vh = jnp.where(m0, v, zero) if h == 0 else jnp.where(m0, zero, v)
            s = lax.dot_general(qh, k, _NT, preferred_element_type=F32)
            inner = (s * dec_ref[2 * p + h]).astype(BF16)
            y = y + jnp.dot(inner, vh, preferred_element_type=F32)

        kz = (k.astype(F32) * zeta_ref[:, lanes]).astype(BF16)
        upd = lax.dot_general(kz, v, _TN, preferred_element_type=F32)
        state_ref[p] = state * gch_ref[:, lanes] + jnp.where(same_head, upd, 0.0)

        d = y - head_mean(y)
        yn = d * lax.rsqrt(head_mean(d * d) + EPS) * gain_ref[:, lanes]
        g = gate_ref[:, lanes].astype(F32)
        o_ref[:, lanes] = (yn * (g * (1.0 / (1.0 + jnp.exp(-g))))).astype(BF16)


def retention_tables(chunk):
    h = N_HEADS_RET
    log_g = jnp.log(1.0 - 2.0 ** (-5.0 - jnp.arange(h, dtype=F32)))
    idx = jnp.arange(chunk, dtype=F32)
    diff = idx[:, None] - idx[None, :]
    dec = jnp.where(diff[None] >= 0, jnp.exp(log_g[:, None, None] * jnp.maximum(diff, 0.0)[None]), 0.0)
    xi = jnp.exp(log_g[:, None] * (idx[None, :] + 1.0))
    zeta = jnp.exp(log_g[:, None] * (chunk - 1.0 - idx[None, :]))
    gch = jnp.exp(log_g * chunk)
    lanes = lambda t: jnp.repeat(t.T, HEAD_DIM, axis=1)
    return dec, lanes(xi), lanes(zeta), jnp.repeat(gch, HEAD_DIM)[None, :]


def retention(proj, gain, *, batch, seq, q_blk, k_blk, v_blk, g_blk):
    n = proj.shape[0]
    chunk = min(RET_CHUNK, seq)
    nc = seq // chunk
    width = RET_PAIRS * LANES
    steps = N_HEADS_RET * HEAD_DIM // width
    assert all(b % RET_PAIRS == 0 for b in (q_blk, k_blk, v_blk, g_blk))
    dec, xi, zeta, gch = retention_tables(chunk)
    blk = lambda off: pl.BlockSpec((chunk, width), lambda b, p, c: (b * nc + c, off // RET_PAIRS + p))
    return pl.pallas_call(
        _retention_kernel,
        out_shape=jax.ShapeDtypeStruct((n, steps * width), BF16),
        grid=(batch, steps, nc),
        in_specs=[
            blk(q_blk), blk(k_blk), blk(v_blk), blk(g_blk),
            pl.BlockSpec((1, width), lambda b, p, c: (0, p)),
            pl.BlockSpec((2 * RET_PAIRS, chunk, chunk), lambda b, p, c: (p, 0, 0)),
            pl.BlockSpec((chunk, width), lambda b, p, c: (0, p)),
            pl.BlockSpec((chunk, width), lambda b, p, c: (0, p)),
            pl.BlockSpec((1, width), lambda b, p, c: (0, p)),
        ],
        out_specs=pl.BlockSpec((chunk, width), lambda b, p, c: (b * nc + c, p)),
        scratch_shapes=[pltpu.VMEM((RET_PAIRS, LANES, LANES), F32)],
        compiler_params=_cparams("arbitrary", "arbitrary", "arbitrary"),
        name="retention",
    )(proj, proj, proj, proj, gain.reshape(1, -1).astype(F32), dec, xi, zeta, gch)


def _swa_kernel(sink_ref, q_ref, kp_ref, kc_ref, vp_ref, vc_ref, o_ref):
    blk = pl.program_id(1)
    m0 = _head0_mask()
    w = WINDOW
    qi = lax.broadcasted_iota(I32, (w, 2 * w), 0)
    kj = lax.broadcasted_iota(I32, (w, 2 * w), 1)
    rel = qi + w - kj
    valid = (rel >= 0) & (rel < w) & ((blk > 0) | (kj >= w))
    kband = jnp.concatenate([kp_ref[...], kc_ref[...]], axis=0)
    vband = jnp.concatenate([vp_ref[...], vc_ref[...]], axis=0)
    group = N_HEADS_SWA // N_KV_SWA
    slabs = group * HEAD_DIM // LANES
    swap = lambda t: jnp.concatenate([t[:, HEAD_DIM:], t[:, :HEAD_DIM]], axis=1)
    zero_kv = jnp.zeros((2 * w, LANES), BF16)
    zero_q = jnp.zeros((slabs * w, LANES), BF16)
    ones_kv = jnp.ones((2 * w, LANES), BF16)

    def half_heads(q_stack, k_half, v_half, first_head):
        s = lax.dot_general(q_stack, k_half, _NT, preferred_element_type=F32)
        probs, sink_terms = [], []
        for slab in range(slabs):
            sink = sink_ref[first_head + 2 * slab]
            sh = jnp.where(valid, s[slab * w:(slab + 1) * w], -jnp.inf)
            m = jnp.maximum(jnp.max(sh, axis=-1, keepdims=True), sink)
            probs.append(jnp.exp(sh - m).astype(BF16))
            sink_terms.append(jnp.exp(sink - jnp.broadcast_to(m, (w, LANES))))
        p = jnp.concatenate(probs, axis=0)
        pv = jnp.dot(p, jnp.concatenate([v_half, ones_kv], axis=1), preferred_element_type=F32)
        denom = pv[:, LANES:] + jnp.concatenate(sink_terms, axis=0)
        return pv[:, :LANES] / denom

    for pair in range(N_KV_SWA * HEAD_DIM // LANES):
        k2 = kband[:, pair * LANES:(pair + 1) * LANES]
        v2 = vband[:, pair * LANES:(pair + 1) * LANES]
        k2s, v2s = swap(k2), swap(v2)
        for c in range(2):
            kvh = 2 * pair + c
            k_even, k_odd = (k2, k2s) if c == 0 else (k2s, k2)
            v_even, v_odd = (v2, v2s) if c == 0 else (v2s, v2)
            base = kvh * group * HEAD_DIM
            q_stack = jnp.concatenate([q_ref[:, base + t * LANES:base + (t + 1) * LANES] for t in range(slabs)],
                                      axis=0)
            out = (half_heads(jnp.where(m0, q_stack, zero_q), k_even, jnp.where(m0, v_even, zero_kv), kvh * group)
                   + half_heads(jnp.where(m0, zero_q, q_stack), k_odd, jnp.where(m0, zero_kv, v_odd),
                                kvh * group + 1))
            for t in range(slabs):
                o_ref[:, base + t * LANES:base + (t + 1) * LANES] = out[t * w:(t + 1) * w].astype(BF16)


def swa_attention(proj, sinks, *, batch, seq):
    n = proj.shape[0]
    w = WINDOW
    nb = seq // w
    wq = N_HEADS_SWA * HEAD_DIM
    wkv = N_KV_SWA * HEAD_DIM
    k_blk = wq // wkv
    cur = lambda off: pl.BlockSpec((w, wkv), lambda b, i: (b * nb + i, off))
    prev = lambda off: pl.BlockSpec((w, wkv), lambda b, i: (b * nb + jnp.maximum(i - 1, 0), off))
    return pl.pallas_call(
        _swa_kernel,
        out_shape=jax.ShapeDtypeStruct((n, wq), BF16),
        grid=(batch, nb),
        in_specs=[
            pl.BlockSpec(memory_space=pltpu.SMEM),
            pl.BlockSpec((w, wq), lambda b, i: (b * nb + i, 0)),
            prev(k_blk), cur(k_blk), prev(k_blk + 1), cur(k_blk + 1),
        ],
        out_specs=pl.BlockSpec((w, wq), lambda b, i: (b * nb + i, 0)),
        compiler_params=_cparams("arbitrary", "arbitrary"),
        name="swa_attention",
    )(sinks.astype(F32), proj, proj, proj, proj, proj)


ROUTE_EID = 0
GROUP_LANE = N_EXPERTS


def _split_bf16(t):
    hi = t.astype(BF16)
    return hi, (t - hi.astype(F32)).astype(BF16)


def _outproj_router_kernel(*refs, n_lhs):
    lhs = refs[:n_lhs]
    ws = refs[n_lhs:2 * n_lhs]
    b_ref, h_ref, g_ref, wr_ref = refs[2 * n_lhs:2 * n_lhs + 4]
    h1_ref, xn_ref, route_ref, gate_ref, cnt_ref, carry_ref = refs[2 * n_lhs + 4:]
    step = pl.program_id(0)

    @pl.when(step == 0)
    def _():
        carry_ref[...] = jnp.zeros_like(carry_ref)

    w_split = jnp.concatenate(_split_bf16(wr_ref[...]), axis=1)
    for sub in range(h_ref.shape[0] // ROUTER_SUB):
        rows = slice(sub * ROUTER_SUB, (sub + 1) * ROUTER_SUB)
        mix = b_ref[...]
        for a_ref, w_ref in zip(lhs, ws):
            mix = mix + jnp.dot(a_ref[rows], w_ref[...], preferred_element_type=F32)
        h1 = h_ref[rows] + mix
        h1_ref[rows] = h1
        ms = jnp.mean(h1 * h1, axis=-1, keepdims=True)
        xn = h1 * lax.rsqrt(ms + EPS) * g_ref[...]
        _tile_store(xn_ref, sub * ROUTER_SUB, ROUTER_SUB, xn)
        x_hi, x_lo = _split_bf16(xn)
        parts = (jnp.dot(x_hi, w_split, preferred_element_type=F32)
                 + jnp.dot(x_lo, w_split, preferred_element_type=F32))
        logits = parts[:, :LANES] + parts[:, LANES:]
        route, gate, cnt = _route(logits)
        route_ref[rows] = route
        gate_ref[rows] = gate
        carry_ref[...] += cnt
    cnt_ref[...] = jnp.broadcast_to(carry_ref[...], cnt_ref.shape).astype(I32)


def _route(logits):
    tm = logits.shape[0]
    lane = lax.broadcasted_iota(I32, (tm, LANES), 1)
    lane_f = lane.astype(F32)
    neg = -jnp.inf

    def lane_max(t):
        return jnp.max(t, axis=-1, keepdims=True)

    def lane_sum(t):
        return jnp.sum(t, axis=-1, keepdims=True)

    def first_lane_of(t, value, mask):
        return jnp.min(jnp.where(mask & (t == value), lane_f, float(LANES)), axis=-1, keepdims=True)

    is_group = (lane >= GROUP_LANE) & (lane < GROUP_LANE + N_GROUPS)
    gl = jnp.where(is_group, logits, neg)
    ge = jnp.exp(gl - lane_max(gl))
    gp = ge / lane_sum(ge)
    g_prob = lane_max(gp)
    g_idx = first_lane_of(gp, g_prob, is_group) - float(GROUP_LANE)
    group_of_lane = lax.shift_right_logical(lane, int(np.log2(EXPERTS_PER_GROUP))).astype(F32)
    in_group = (lane < N_EXPERTS) & (group_of_lane == g_idx)
    el = jnp.where(in_group, logits, neg)
    ee = jnp.exp(el - lane_max(el))
    ep = ee / lane_sum(ee)
    p1 = lane_max(ep)
    i1 = first_lane_of(ep, p1, in_group)
    rest = in_group & (lane_f != i1)
    ep2 = jnp.where(rest, ep, neg)
    p2 = lane_max(ep2)
    i2 = first_lane_of(ep2, p2, rest)
    denom = p1 + p2
    gate1 = g_prob * (p1 / denom)
    gate2 = g_prob * (p2 / denom)

    cnt = jnp.where((lane_f == i1) | (lane_f == i2), 1.0, 0.0)
    route = jnp.where(lane == ROUTE_EID, i1, 0.0)
    route = jnp.where(lane == ROUTE_EID + 1, i2, route)
    gate = jnp.where(lane == 0, gate1, jnp.where(lane == 1, gate2, 0.0))
    return route.astype(I32), gate, jnp.sum(cnt, axis=0, keepdims=True)


def outproj_router(lhs, ws, bias, h, g, w_router):
    n, d = h.shape
    tm = ROUTER_TM
    assert n % tm == 0
    row_blk = lambda width: pl.BlockSpec((tm, width), lambda i: (i, 0))
    full = lambda a: pl.BlockSpec(a.shape, lambda i: (0, 0))
    bias2, g2 = bias.reshape(1, d), g.reshape(1, d)
    args = [*lhs, *ws, bias2, h, g2, w_router]
    in_specs = ([row_blk(a.shape[1]) for a in lhs] + [full(w) for w in ws]
                + [full(bias2), row_blk(d), full(g2), full(w_router)])
    return pl.pallas_call(
        functools.partial(_outproj_router_kernel, n_lhs=len(lhs)),
        out_shape=(
            jax.ShapeDtypeStruct((n, d), F32), jax.ShapeDtypeStruct((n * d // LANES, LANES), F32),
            jax.ShapeDtypeStruct((n, LANES), I32), jax.ShapeDtypeStruct((n, LANES), F32),
            jax.ShapeDtypeStruct((8, LANES), I32),
        ),
        grid=(n // tm,),
        in_specs=in_specs,
        out_specs=(row_blk(d), pl.BlockSpec((tm * d // LANES, LANES), lambda i: (i, 0)),
                   row_blk(LANES), row_blk(LANES), pl.BlockSpec((8, LANES), lambda i: (0, 0))),
        scratch_shapes=[pltpu.VMEM((1, LANES), F32)],
        compiler_params=_cparams("arbitrary"),
        name="outproj_router",
    )(*args)


def router_weights(w_group, w_expert_router):
    d = w_group.shape[0]
    pad = jnp.zeros((d, LANES - N_EXPERTS - N_GROUPS), F32)
    return jnp.concatenate([w_expert_router.astype(F32), w_group.astype(F32), pad], axis=1)


IDX_CHUNK = 1024
IDX_RING = 4
DMA_UNROLL = 8


SUBLANES = 8


def _tile_store(ref, first_token, n_tokens, value):
    for c in range(value.shape[1] // LANES):
        ref[pl.ds(first_token * SUBLANES + c, n_tokens, stride=SUBLANES), :] = value[:, c * LANES:(c + 1) * LANES]


def _tile_load(ref, first_token, n_tokens, width):
    return jnp.concatenate([ref[pl.ds(first_token * SUBLANES + c, n_tokens, stride=SUBLANES), :]
                            for c in range(width // LANES)], axis=1)


def _token_rows(ref, token):
    start = token * SUBLANES
    if not isinstance(start, int):
        start = pl.multiple_of(start, SUBLANES)
    return ref.at[pl.ds(start, SUBLANES), :]


def _token_copy(src, src_token, dst, dst_token, sem):
    return pltpu.make_async_copy(_token_rows(src, src_token), _token_rows(dst, dst_token), sem)


def _moe_kernel(blk_exp, q0s, nvalids, nused, order_hbm, xn_hbm, wg_ref, wu_ref, wd_ref, out_hbm,
                ibuf, xbuf, ybuf, wg_bf, wu_bf, wd_bf, isem, gsem, ssem):
    i = pl.program_id(0)
    nu = nused[0]
    m = MOE_BLOCK
    n_tok = xn_hbm.shape[0] // SUBLANES
    d = wd_bf.shape[1]

    window = 2 * IDX_CHUNK

    def idx_copy(blk):
        base = pl.multiple_of(q0s[blk] & ~(IDX_CHUNK - 1), IDX_CHUNK)
        slot = blk & (IDX_RING - 1)
        dst = ibuf.at[pl.ds(pl.multiple_of(slot * window, window), window)]
        return pltpu.make_async_copy(order_hbm.at[pl.ds(base, window)], dst, isem.at[slot])

    def pair_base(blk):
        return (blk & (IDX_RING - 1)) * window + (q0s[blk] & (IDX_CHUNK - 1))

    def gather_group(blk_base, dst, sem, g):
        for u in range(DMA_UNROLL):
            r = g * DMA_UNROLL + u
            pair = ibuf[blk_base + r]
            tok = jnp.where(pair >= n_tok, pair - n_tok, pair)
            _token_copy(xn_hbm, tok, dst, r, sem).start()

    def scatter_group(blk_base, src, sem, g):
        for u in range(DMA_UNROLL):
            r = g * DMA_UNROLL + u
            _token_copy(src, r, out_hbm, ibuf[blk_base + r], sem).start()

    def issue_gathers(blk, unrolled=False):
        slot = blk & 1
        args = (pair_base(blk), xbuf.at[slot], gsem.at[slot])
        if unrolled:
            for g in range(m // DMA_UNROLL):
                gather_group(*args, g)
        else:
            lax.fori_loop(0, m // DMA_UNROLL, lambda g, c: (gather_group(*args, g), c)[1], 0)

    def wait_gathers(blk):
        slot = blk & 1
        pltpu.make_async_copy(xn_hbm.at[pl.ds(0, m * SUBLANES), :], xbuf.at[slot], gsem.at[slot]).wait()

    def issue_scatters(blk, unrolled=False):
        slot = blk & 1
        base = pair_base(blk)
        src = ybuf.at[slot]
        sem = ssem.at[slot]
        if unrolled:
            for g in range(m // DMA_UNROLL):
                scatter_group(base, src, sem, g)
            return
        nv = nvalids[blk]
        groups = lax.shift_right_logical(nv, DMA_UNROLL.bit_length() - 1)

        def tail(r, c):
            _token_copy(src, r, out_hbm, ibuf[base + r], sem).start()
            return c

        lax.fori_loop(0, groups, lambda g, c: (scatter_group(base, src, sem, g), c)[1], 0)
        lax.fori_loop(groups * DMA_UNROLL, nv, tail, 0)

    def wait_scatters(blk):
        slot = blk & 1
        nv = nvalids[blk]
        rows = pl.multiple_of(nv * SUBLANES, SUBLANES)

        @pl.when(nv > 0)
        def _():
            pltpu.make_async_copy(ybuf.at[slot].at[pl.ds(0, rows), :], out_hbm.at[pl.ds(0, rows), :],
                                  ssem.at[slot]).wait()

    @pl.when(i == 0)
    def _():
        first = idx_copy(0)
        first.start()
        first.wait()
        issue_gathers(0)

        @pl.when(nu > 1)
        def _():
            idx_copy(1).start()

    def expert():
        x = _tile_load(xbuf.at[i & 1], 0, m, d).astype(BF16)
        gate = jnp.dot(x, wg_bf[...], preferred_element_type=F32)
        up = jnp.dot(x, wu_bf[...], preferred_element_type=F32)
        hidden = (gate * (1.0 / (1.0 + jnp.exp(-gate))) * up).astype(BF16)
        _tile_store(ybuf.at[i & 1], 0, m, jnp.dot(hidden, wd_bf[...], preferred_element_type=F32))

    @pl.when(i < nu)
    def _():
        @pl.when(i + 2 < nu)
        def _():
            idx_copy(i + 2).start()

        @pl.when(i + 1 < nu)
        def _():
            idx_copy(i + 1).wait()

        wait_gathers(i)

        @pl.when((i == 0) | (blk_exp[i] != blk_exp[jnp.maximum(i - 1, 0)]))
        def _():
            wg_bf[...] = wg_ref[0, 0].astype(BF16)
            wu_bf[...] = wu_ref[0, 0].astype(BF16)
            wd_bf[...] = wd_ref[0, 0].astype(BF16)

        @pl.when(i >= 2)
        def _():
            wait_scatters(i - 2)

        prev = jnp.maximum(i - 1, 0)
        steady = (i >= 1) & (i + 1 < nu) & (nvalids[prev] == m)

        @pl.when(steady)
        def _():
            issue_gathers(i + 1, unrolled=True)
            issue_scatters(i - 1, unrolled=True)
            expert()

        @pl.when(jnp.logical_not(steady))
        def _():
            @pl.when(i + 1 < nu)
            def _():
                issue_gathers(i + 1)

            @pl.when(i >= 1)
            def _():
                issue_scatters(i - 1)

            expert()

        @pl.when(i == nu - 1)
        def _():
            issue_scatters(i)

            @pl.when(i >= 1)
            def _():
                wait_scatters(i - 1)

            wait_scatters(i)


def moe_experts(order, plan, xn, w_gate, w_up, w_down, layer):
    blk_exp, q0s, nvalids, nused = plan
    d, de = w_gate.shape[2], w_gate.shape[3]
    assert d == SUBLANES * LANES
    rows = xn.shape[0]
    m = MOE_BLOCK
    w_in_spec = pl.BlockSpec((1, 1, d, de), lambda i, be, q0, nv, nu: (layer, be[i], 0, 0))
    return pl.pallas_call(
        _moe_kernel,
        out_shape=jax.ShapeDtypeStruct((2 * rows, LANES), F32),
        grid_spec=pltpu.PrefetchScalarGridSpec(
            num_scalar_prefetch=4,
            grid=(blk_exp.shape[0],),
            in_specs=[
                pl.BlockSpec(memory_space=pl.ANY), pl.BlockSpec(memory_space=pl.ANY),
                w_in_spec, w_in_spec,
                pl.BlockSpec((1, 1, de, d), lambda i, be, q0, nv, nu: (layer, be[i], 0, 0)),
            ],
            out_specs=pl.BlockSpec(memory_space=pl.ANY),
            scratch_shapes=[
                pltpu.SMEM((IDX_RING * 2 * IDX_CHUNK,), I32),
                pltpu.VMEM((2, m * SUBLANES, LANES), F32), pltpu.VMEM((2, m * SUBLANES, LANES), F32),
                pltpu.VMEM((d, de), BF16), pltpu.VMEM((d, de), BF16), pltpu.VMEM((de, d), BF16),
                pltpu.SemaphoreType.DMA((IDX_RING,)), pltpu.SemaphoreType.DMA((2,)), pltpu.SemaphoreType.DMA((2,)),
            ],
        ),
        compiler_params=_cparams("arbitrary"),
        name="moe_experts",
    )(blk_exp, q0s, nvalids, nused, order, xn, w_gate, w_up, w_down)


def _moe_finish_kernel(y0_ref, y1_ref, gate_ref, h_ref, g_ref, o_ref, *, final_norm):
    gate = gate_ref[...]
    tm, d = h_ref.shape
    y0 = _tile_load(y0_ref, 0, tm, d)
    y1 = _tile_load(y1_ref, 0, tm, d)
    out = h_ref[...] + (y0 * gate[:, 0:1] + y1 * gate[:, 1:2])
    if final_norm:
        ms = jnp.mean(out * out, axis=-1, keepdims=True)
        out = out * lax.rsqrt(ms + EPS) * g_ref[...]
    o_ref[...] = out


def moe_finish(ys, gate, h, g, *, final_norm):
    n, d = h.shape
    tm = TOK_TM
    nt = n // tm
    row_blk = lambda width: pl.BlockSpec((tm, width), lambda i: (i, 0))
    tiles = lambda first: pl.BlockSpec((tm * SUBLANES, LANES), lambda i: (i + first, 0))
    return pl.pallas_call(
        functools.partial(_moe_finish_kernel, final_norm=final_norm),
        out_shape=jax.ShapeDtypeStruct((n, d), F32),
        grid=(nt,),
        in_specs=[tiles(0), tiles(nt), row_blk(LANES), row_blk(d), pl.BlockSpec((1, d), lambda i: (0, 0))],
        out_specs=row_blk(d),
        compiler_params=_cparams("arbitrary"),
        name="moe_finish",
    )(ys, ys, gate, h, g.reshape(1, d))


def dispatch_plan(route, counts):
    m = MOE_BLOCK
    n = route.shape[0]
    assert (2 * n) % IDX_CHUNK == 0
    counts = counts[0, :N_EXPERTS]
    padded = ((counts + m - 1) // m) * m
    pend = jnp.cumsum(padded)
    pstart = pend - padded
    start = jnp.cumsum(counts) - counts
    eid = jnp.concatenate([route[:, ROUTE_EID], route[:, ROUTE_EID + 1]])
    order = jnp.argsort(eid, stable=True).astype(I32)
    order = jnp.concatenate([order, jnp.zeros((2 * IDX_CHUNK,), I32)])
    nblk = (2 * n + N_EXPERTS * m) // m
    blk_start = jnp.arange(nblk, dtype=I32) * m
    nused = pend[-1] // m
    used = jnp.arange(nblk) < nused
    e = jnp.minimum(jnp.sum(blk_start[:, None] >= pend[None, :], axis=1), N_EXPERTS - 1)
    is_e = e[:, None] == jnp.arange(N_EXPERTS)[None, :]
    of_block = lambda table: jnp.sum(jnp.where(is_e, table[None, :], 0), axis=1)
    r0 = blk_start - of_block(pstart)
    q0s = jnp.where(used, of_block(start) + r0, 0)
    nvalids = jnp.where(used, jnp.clip(of_block(counts) - r0, 0, m), 0)
    last_exp = jnp.sum(jnp.where(jnp.arange(nblk) == nused - 1, e, 0))
    blk_exp = jnp.where(used, e, last_exp)
    as_i32 = lambda t: t.astype(I32)
    return order, (as_i32(blk_exp), as_i32(q0s), as_i32(nvalids), as_i32(nused).reshape(1))


def moe_layer(h1, xn, route, gate, counts, w_gate, w_up, w_down, layer, norm_g, *, final_norm):
    order, plan = dispatch_plan(route, counts)
    ys = moe_experts(order, plan, xn, w_gate, w_up, w_down, layer)
    return moe_finish(ys, gate, h1, norm_g, final_norm=final_norm)


def kernel(x, attn_norm_g, ffn_norm_g, w_in_ab, w_out_ab, ret_gn_g, w_in_c, b_in_c, sinks, w_out_c, b_out_c,
           w_group, w_expert_router, w_gate, w_up, w_down, final_norm_g):
    batch, seq, d = x.shape
    n = batch * seq
    depth = attn_norm_g.shape[0]
    cos, sin = rope_lane_tables(seq)
    later = (jnp.arange(min(SB_TILE, seq))[:, None] > jnp.arange(min(SB_TILE, seq))[None, :]).astype(BF16)
    w_sb = N_HEADS_SB * HEAD_DIM
    w_ret = N_HEADS_RET * HEAD_DIM
    h = x.reshape(n, d)
    for layer in range(depth):
        i = layer // 2
        last = layer == depth - 1
        if layer % 2 == 0:
            proj = norm_proj(h, attn_norm_g[layer], w_in_ab[i].astype(BF16), jnp.zeros((w_in_ab.shape[2],), F32),
                             cos, sin, seq=seq, tn=w_sb,
                             col_ops=("scale", "", "", "rope", "rope scale", "", ""))
            per = w_sb // LANES
            a = sb_attention(proj, later, batch=batch, seq=seq, q_blk=0, k_blk=per, v_blk=2 * per)
            r = retention(proj, ret_gn_g[i], batch=batch, seq=seq,
                          q_blk=3 * per, k_blk=4 * per, v_blk=5 * per, g_blk=6 * per)
            w_out = w_out_ab[i].astype(BF16)
            lhs, ws = [a, r], [w_out[:w_sb], w_out[w_sb:]]
            bias = jnp.zeros((d,), F32)
        else:
            wkv = N_KV_SWA * HEAD_DIM
            nq = N_HEADS_SWA * HEAD_DIM // wkv
            proj = norm_proj(h, attn_norm_g[layer], w_in_c[i].astype(BF16), b_in_c[i].astype(F32),
                             cos, sin, seq=seq, tn=wkv,
                             col_ops=("rope scale",) * nq + ("rope", ""))
            o = swa_attention(proj, sinks[i], batch=batch, seq=seq)
            lhs, ws = [o], [w_out_c[i].astype(BF16)]
            bias = b_out_c[i].astype(F32)
        h1, xn, route, gate, counts = outproj_router(
            lhs, ws, bias, h, ffn_norm_g[layer], router_weights(w_group[layer], w_expert_router[layer]))
        h = moe_layer(h1, xn, route, gate, counts, w_gate, w_up, w_down, layer, final_norm_g, final_norm=last)
    return h.reshape(batch, seq, d)
```

```python
import functools

import jax
import jax.numpy as jnp
import numpy as np
from jax import lax
from jax.experimental import pallas as pl
from jax.experimental.pallas import tpu as pltpu

F32 = jnp.float32
BF16 = jnp.bfloat16
I32 = jnp.int32

HEAD_DIM = 64
N_HEADS_SB = 8
N_HEADS_RET = 8
N_HEADS_SWA = 16
N_KV_SWA = 4
WINDOW = 128
ROPE_THETA = 10000.0
N_GROUPS = 4
EXPERTS_PER_GROUP = 8
N_EXPERTS = N_GROUPS * EXPERTS_PER_GROUP
MOE_BLOCK = 256
EPS = 1e-6

LANES = 128
HALF = HEAD_DIM // 2
QK_SCALE = HEAD_DIM ** -0.5

SB_TILE = 256
RET_CHUNK = 256
PROJ_TM = 512
TOK_TM = 512
ROUTER_TM = 512
ROUTER_SUB = 256
VMEM_LIMIT = 48 * 1024 * 1024

_NT = (((1,), (1,)), ((), ()))
_TN = (((0,), (0,)), ((), ()))


def _cparams(*sem):
    return pltpu.CompilerParams(dimension_semantics=sem, vmem_limit_bytes=VMEM_LIMIT)


def _head0_mask():
    return lax.broadcasted_iota(I32, (1, LANES), 1) < HEAD_DIM


def _norm_proj_kernel(x_ref, g_ref, w_ref, b_ref, cos_ref, sin_ref, o_ref, *, tn, col_ops):
    x = x_ref[...]
    ms = jnp.mean(x * x, axis=-1, keepdims=True)
    xn = (x * lax.rsqrt(ms + EPS) * g_ref[...]).astype(BF16)
    lane = lax.broadcasted_iota(I32, (1, LANES), 1)
    first_half = (lane % HEAD_DIM) < HALF
    for j, op in enumerate(col_ops):
        cols = slice(j * tn, (j + 1) * tn)
        acc = jnp.dot(xn, w_ref[:, cols], preferred_element_type=F32) + b_ref[:, cols]
        if "rope" in op:
            cos = cos_ref[...]
            sin = sin_ref[...]
            slabs = []
            for s in range(tn // LANES):
                a = acc[:, s * LANES:(s + 1) * LANES]
                partner = jnp.where(first_half, pltpu.roll(a, LANES - HALF, 1), pltpu.roll(a, HALF, 1))
                slabs.append(a * cos + partner * sin)
            acc = jnp.concatenate(slabs, axis=1)
        if "scale" in op:
            acc = acc * QK_SCALE
        o_ref[:, cols] = acc.astype(BF16)


def norm_proj(x, g, w, b, cos, sin, *, seq, tn, col_ops):
    n, d = x.shape
    f = w.shape[1]
    tm = min(PROJ_TM, seq)
    assert n % tm == 0 and seq % tm == 0 and f == tn * len(col_ops)
    pos_blocks = seq // tm
    return pl.pallas_call(
        functools.partial(_norm_proj_kernel, tn=tn, col_ops=col_ops),
        out_shape=jax.ShapeDtypeStruct((n, f), BF16),
        grid=(n // tm,),
        in_specs=[
            pl.BlockSpec((tm, d), lambda i: (i, 0)),
            pl.BlockSpec((1, d), lambda i: (0, 0)),
            pl.BlockSpec((d, f), lambda i: (0, 0)),
            pl.BlockSpec((1, f), lambda i: (0, 0)),
            pl.BlockSpec((tm, LANES), lambda i: (i % pos_blocks, 0)),
            pl.BlockSpec((tm, LANES), lambda i: (i % pos_blocks, 0)),
        ],
        out_specs=pl.BlockSpec((tm, f), lambda i: (i, 0)),
        compiler_params=_cparams("arbitrary"),
        name="norm_proj",
    )(x, g.reshape(1, d), w, b.reshape(1, f), cos, sin)


def rope_lane_tables(seq):
    pos = jnp.arange(seq, dtype=F32)
    inv = ROPE_THETA ** (-jnp.arange(0, HEAD_DIM, 2, dtype=F32) / HEAD_DIM)
    ang = pos[:, None] * inv[None, :]
    cos, sin = jnp.cos(ang), jnp.sin(ang)
    return jnp.tile(cos, (1, 4)), jnp.tile(jnp.concatenate([-sin, sin], axis=1), (1, 2))


SB_PAIRS = 4


def _sb_kernel(q_ref, k_ref, v_ref, t_ref, o_ref, acc_ref, carry_ref, *, tile):
    qi = pl.program_id(2)
    m0 = _head0_mask()
    zero = jnp.zeros((tile, LANES), BF16)
    row = lax.broadcasted_iota(I32, (tile, tile), 0)
    col = lax.broadcasted_iota(I32, (tile, tile), 1)
    strict = col < row

    def key_tile(kb, diag):
        start = pl.multiple_of(kb * tile, tile)
        for p in range(SB_PAIRS):
            lanes = slice(p * LANES, (p + 1) * LANES)
            q = q_ref[:, lanes]
            k = k_ref[pl.ds(start, tile), lanes]
            v = v_ref[pl.ds(start, tile), lanes]
            out = None
            for h in range(2):
                qh = jnp.where(m0, q, zero) if h == 0 else jnp.where(m0, zero, q)
                vh = jnp.where(m0, v, zero) if h == 0 else jnp.where(m0, zero, v)
                z = lax.dot_general(qh, k, _NT, preferred_element_type=F32)
                log_fail = -(jnp.maximum(z, 0.0) + jnp.log(1.0 + jnp.exp(-jnp.abs(z))))
                if diag:
                    log_fail = jnp.where(strict, log_fail, 0.0)
                after = jnp.dot(log_fail.astype(BF16), t_ref[...], preferred_element_type=F32)
                logw = z + log_fail + after
                if not diag:
                    logw = logw + carry_ref[2 * p + h]
                w = jnp.exp(logw)
                if diag:
                    w = jnp.where(strict, w, 0.0)
                pv = jnp.dot(w.astype(BF16), vh, preferred_element_type=F32)
                out = pv if out is None else out + pv
                tile_sum = jnp.sum(log_fail, axis=-1, keepdims=True)
                if diag:
                    carry_ref[2 * p + h] = tile_sum
                else:
                    carry_ref[2 * p + h] += tile_sum
            if diag:
                acc_ref[:, lanes] = out
            else:
                acc_ref[:, lanes] += out

    key_tile(qi, True)

    def body(j, c):
        key_tile(qi - 1 - j, False)
        return c

    lax.fori_loop(0, qi, body, 0)
    o_ref[...] = acc_ref[...].astype(BF16)


def sb_attention(proj, later_mat, *, batch, seq, q_blk, k_blk, v_blk):
    n = proj.shape[0]
    tile = min(SB_TILE, seq)
    nq = seq // tile
    width = SB_PAIRS * LANES
    steps = N_HEADS_SB * HEAD_DIM // width
    assert q_blk % SB_PAIRS == 0 and k_blk % SB_PAIRS == 0 and v_blk % SB_PAIRS == 0
    col = lambda blk: blk // SB_PAIRS
    return pl.pallas_call(
        functools.partial(_sb_kernel, tile=tile),
        out_shape=jax.ShapeDtypeStruct((n, steps * width), BF16),
        grid=(batch, steps, nq),
        in_specs=[
            pl.BlockSpec((tile, width), lambda b, p, i: (b * nq + i, col(q_blk) + p)),
            pl.BlockSpec((seq, width), lambda b, p, i: (b, col(k_blk) + p)),
            pl.BlockSpec((seq, width), lambda b, p, i: (b, col(v_blk) + p)),
            pl.BlockSpec((tile, tile), lambda b, p, i: (0, 0)),
        ],
        out_specs=pl.BlockSpec((tile, width), lambda b, p, i: (b * nq + i, p)),
        scratch_shapes=[pltpu.VMEM((tile, width), F32), pltpu.VMEM((2 * SB_PAIRS, tile, 1), F32)],
        compiler_params=_cparams("arbitrary", "arbitrary", "arbitrary"),
        name="sb_attention",
    )(proj, proj, proj, later_mat)


RET_PAIRS = 4


def _retention_kernel(q_ref, k_ref, v_ref, gate_ref, gain_ref, dec_ref, xi_ref, zeta_ref, gch_ref,
                      o_ref, state_ref):
    c = pl.program_id(2)

    @pl.when(c == 0)
    def _():
        state_ref[...] = jnp.zeros_like(state_ref)

    m0 = _head0_mask()
    zero = jnp.zeros((q_ref.shape[0], LANES), BF16)
    r = lax.broadcasted_iota(I32, (LANES, LANES), 0) < HEAD_DIM
    cc = lax.broadcasted_iota(I32, (LANES, LANES), 1) < HEAD_DIM
    same_head = r == cc

    def head_mean(t):
        s0 = jnp.sum(jnp.where(m0, t, 0.0), axis=-1, keepdims=True)
        s1 = jnp.sum(jnp.where(m0, 0.0, t), axis=-1, keepdims=True)
        return jnp.where(m0, s0, s1) * (1.0 / HEAD_DIM)

    for p in range(RET_PAIRS):
        lanes = slice(p * LANES, (p + 1) * LANES)
        q = q_ref[:, lanes]
        k = k_ref[:, lanes]
        v = v_ref[:, lanes]
        state = state_ref[p]
        y = jnp.dot(q, state.astype(BF16), preferred_element_type=F32) * xi_ref[:, lanes]
        for h in range(2):
            qh = jnp.where(m0, q, zero) if h == 0 else jnp.where(m0, zero, q)
            vh = jnp.where(m0, v, zero) if h == 0 else jnp.where(m0, zero, v)
            s = lax.dot_general(qh, k, _NT, preferred_element_type=F32)
            inner = (s * dec_ref[2 * p + h]).astype(BF16)
            y = y + jnp.dot(inner, vh, preferred_element_type=F32)

        kz = (k.astype(F32) * zeta_ref[:, lanes]).astype(BF16)
        upd = lax.dot_general(kz, v, _TN, preferred_element_type=F32)
        state_ref[p] = state * gch_ref[:, lanes] + jnp.where(same_head, upd, 0.0)

        d = y - head_mean(y)
        yn = d * lax.rsqrt(head_mean(d * d) + EPS) * gain_ref[:, lanes]
        g = gate_ref[:, lanes].astype(F32)
        o_ref[:, lanes] = (yn * (g * (1.0 / (1.0 + jnp.exp(-g))))).astype(BF16)


def retention_tables(chunk):
    h = N_HEADS_RET
    log_g = jnp.log(1.0 - 2.0 ** (-5.0 - jnp.arange(h, dtype=F32)))
    idx = jnp.arange(chunk, dtype=F32)
    diff = idx[:, None] - idx[None, :]
    dec = jnp.where(diff[None] >= 0, jnp.exp(log_g[:, None, None] * jnp.maximum(diff, 0.0)[None]), 0.0)
    xi = jnp.exp(log_g[:, None] * (idx[None, :] + 1.0))
    zeta = jnp.exp(log_g[:, None] * (chunk - 1.0 - idx[None, :]))
    gch = jnp.exp(log_g * chunk)
    lanes = lambda t: jnp.repeat(t.T, HEAD_DIM, axis=1)
    return dec, lanes(xi), lanes(zeta), jnp.repeat(gch, HEAD_DIM)[None, :]


def retention(proj, gain, *, batch, seq, q_blk, k_blk, v_blk, g_blk):
    n = proj.shape[0]
    chunk = min(RET_CHUNK, seq)
    nc = seq // chunk
    width = RET_PAIRS * LANES
    steps = N_HEADS_RET * HEAD_DIM // width
    assert all(b % RET_PAIRS == 0 for b in (q_blk, k_blk, v_blk, g_blk))
    dec, xi, zeta, gch = retention_tables(chunk)
    blk = lambda off: pl.BlockSpec((chunk, width), lambda b, p, c: (b * nc + c, off // RET_PAIRS + p))
    return pl.pallas_call(
        _retention_kernel,
        out_shape=jax.ShapeDtypeStruct((n, steps * width), BF16),
        grid=(batch, steps, nc),
        in_specs=[
            blk(q_blk), blk(k_blk), blk(v_blk), blk(g_blk),
            pl.BlockSpec((1, width), lambda b, p, c: (0, p)),
            pl.BlockSpec((2 * RET_PAIRS, chunk, chunk), lambda b, p, c: (p, 0, 0)),
            pl.BlockSpec((chunk, width), lambda b, p, c: (0, p)),
            pl.BlockSpec((chunk, width), lambda b, p, c: (0, p)),
            pl.BlockSpec((1, width), lambda b, p, c: (0, p)),
        ],
        out_specs=pl.BlockSpec((chunk, width), lambda b, p, c: (b * nc + c, p)),
        scratch_shapes=[pltpu.VMEM((RET_PAIRS, LANES, LANES), F32)],
        compiler_params=_cparams("arbitrary", "arbitrary", "arbitrary"),
        name="retention",
    )(proj, proj, proj, proj, gain.reshape(1, -1).astype(F32), dec, xi, zeta, gch)


def _swa_kernel(sink_ref, q_ref, kp_ref, kc_ref, vp_ref, vc_ref, o_ref):
    blk = pl.program_id(1)
    m0 = _head0_mask()
    w = WINDOW
    qi = lax.broadcasted_iota(I32, (w, 2 * w), 0)
    kj = lax.broadcasted_iota(I32, (w, 2 * w), 1)
    rel = qi + w - kj
    valid = (rel >= 0) & (rel < w) & ((blk > 0) | (kj >= w))
    kband = jnp.concatenate([kp_ref[...], kc_ref[...]], axis=0)
    vband = jnp.concatenate([vp_ref[...], vc_ref[...]], axis=0)
    group = N_HEADS_SWA // N_KV_SWA
    slabs = group * HEAD_DIM // LANES
    swap = lambda t: jnp.concatenate([t[:, HEAD_DIM:], t[:, :HEAD_DIM]], axis=1)
    zero_kv = jnp.zeros((2 * w, LANES), BF16)
    zero_q = jnp.zeros((slabs * w, LANES), BF16)
    ones_kv = jnp.ones((2 * w, LANES), BF16)

    def half_heads(q_stack, k_half, v_half, first_head):
        s = lax.dot_general(q_stack, k_half, _NT, preferred_element_type=F32)
        probs, sink_terms = [], []
        for slab in range(slabs):
            sink = sink_ref[first_head + 2 * slab]
            sh = jnp.where(valid, s[slab * w:(slab + 1) * w], -jnp.inf)
            m = jnp.maximum(jnp.max(sh, axis=-1, keepdims=True), sink)
            probs.append(jnp.exp(sh - m).astype(BF16))
            sink_terms.append(jnp.exp(sink - jnp.broadcast_to(m, (w, LANES))))
        p = jnp.concatenate(probs, axis=0)
        pv = jnp.dot(p, jnp.concatenate([v_half, ones_kv], axis=1), preferred_element_type=F32)
        denom = pv[:, LANES:] + jnp.concatenate(sink_terms, axis=0)
        return pv[:, :LANES] / denom

    for pair in range(N_KV_SWA * HEAD_DIM // LANES):
        k2 = kband[:, pair * LANES:(pair + 1) * LANES]
        v2 = vband[:, pair * LANES:(pair + 1) * LANES]
        k2s, v2s = swap(k2), swap(v2)
        for c in range(2):
            kvh = 2 * pair + c
            k_even, k_odd = (k2, k2s) if c == 0 else (k2s, k2)
            v_even, v_odd = (v2, v2s) if c == 0 else (v2s, v2)
            base = kvh * group * HEAD_DIM
            q_stack = jnp.concatenate([q_ref[:, base + t * LANES:base + (t + 1) * LANES] for t in range(slabs)],
                                      axis=0)
            out = (half_heads(jnp.where(m0, q_stack, zero_q), k_even, jnp.where(m0, v_even, zero_kv), kvh * group)
                   + half_heads(jnp.where(m0, zero_q, q_stack), k_odd, jnp.where(m0, zero_kv, v_odd),
                                kvh * group + 1))
            for t in range(slabs):
                o_ref[:, base + t * LANES:base + (t + 1) * LANES] = out[t * w:(t + 1) * w].astype(BF16)


def swa_attention(proj, sinks, *, batch, seq):
    n = proj.shape[0]
    w = WINDOW
    nb = seq // w
    wq = N_HEADS_SWA * HEAD_DIM
    wkv = N_KV_SWA * HEAD_DIM
    k_blk = wq // wkv
    cur = lambda off: pl.BlockSpec((w, wkv), lambda b, i: (b * nb + i, off))
    prev = lambda off: pl.BlockSpec((w, wkv), lambda b, i: (b * nb + jnp.maximum(i - 1, 0), off))
    return pl.pallas_call(
        _swa_kernel,
        out_shape=jax.ShapeDtypeStruct((n, wq), BF16),
        grid=(batch, nb),
        in_specs=[
            pl.BlockSpec(memory_space=pltpu.SMEM),
            pl.BlockSpec((w, wq), lambda b, i: (b * nb + i, 0)),
            prev(k_blk), cur(k_blk), prev(k_blk + 1), cur(k_blk + 1),
        ],
        out_specs=pl.BlockSpec((w, wq), lambda b, i: (b * nb + i, 0)),
        compiler_params=_cparams("arbitrary", "arbitrary"),
        name="swa_attention",
    )(sinks.astype(F32), proj, proj, proj, proj, proj)


ROUTE_EID = 0
GROUP_LANE = N_EXPERTS


def _split_bf16(t):
    hi = t.astype(BF16)
    return hi, (t - hi.astype(F32)).astype(BF16)


def _outproj_router_kernel(*refs, n_lhs):
    lhs = refs[:n_lhs]
    ws = refs[n_lhs:2 * n_lhs]
    b_ref, h_ref, g_ref, wr_ref = refs[2 * n_lhs:2 * n_lhs + 4]
    h1_ref, xn_ref, route_ref, gate_ref, cnt_ref, carry_ref = refs[2 * n_lhs + 4:]
    step = pl.program_id(0)

    @pl.when(step == 0)
    def _():
        carry_ref[...] = jnp.zeros_like(carry_ref)

    w_split = jnp.concatenate(_split_bf16(wr_ref[...]), axis=1)
    for sub in range(h_ref.shape[0] // ROUTER_SUB):
        rows = slice(sub * ROUTER_SUB, (sub + 1) * ROUTER_SUB)
        mix = b_ref[...]
        for a_ref, w_ref in zip(lhs, ws):
            mix = mix + jnp.dot(a_ref[rows], w_ref[...], preferred_element_type=F32)
        h1 = h_ref[rows] + mix
        h1_ref[rows] = h1
        ms = jnp.mean(h1 * h1, axis=-1, keepdims=True)
        xn = h1 * lax.rsqrt(ms + EPS) * g_ref[...]
        _tile_store(xn_ref, sub * ROUTER_SUB * SUBLANES, ROUTER_SUB, xn)
        x_hi, x_lo = _split_bf16(xn)
        parts = (jnp.dot(x_hi, w_split, preferred_element_type=F32)
                 + jnp.dot(x_lo, w_split, preferred_element_type=F32))
        logits = parts[:, :LANES] + parts[:, LANES:]
        route, gate, cnt = _route(logits)
        route_ref[rows] = route
        gate_ref[rows] = gate
        carry_ref[...] += cnt
    cnt_ref[...] = jnp.broadcast_to(carry_ref[...], cnt_ref.shape).astype(I32)


def _route(logits):
    tm = logits.shape[0]
    lane = lax.broadcasted_iota(I32, (tm, LANES), 1)
    lane_f = lane.astype(F32)
    neg = -jnp.inf

    def lane_max(t):
        return jnp.max(t, axis=-1, keepdims=True)

    def lane_sum(t):
        return jnp.sum(t, axis=-1, keepdims=True)

    def first_lane_of(t, value, mask):
        return jnp.min(jnp.where(mask & (t == value), lane_f, float(LANES)), axis=-1, keepdims=True)

    is_group = (lane >= GROUP_LANE) & (lane < GROUP_LANE + N_GROUPS)
    gl = jnp.where(is_group, logits, neg)
    ge = jnp.exp(gl - lane_max(gl))
    gp = ge / lane_sum(ge)
    g_prob = lane_max(gp)
    g_idx = first_lane_of(gp, g_prob, is_group) - float(GROUP_LANE)
    group_of_lane = lax.shift_right_logical(lane, int(np.log2(EXPERTS_PER_GROUP))).astype(F32)
    in_group = (lane < N_EXPERTS) & (group_of_lane == g_idx)
    el = jnp.where(in_group, logits, neg)
    ee = jnp.exp(el - lane_max(el))
    ep = ee / lane_sum(ee)
    p1 = lane_max(ep)
    i1 = first_lane_of(ep, p1, in_group)
    rest = in_group & (lane_f != i1)
    ep2 = jnp.where(rest, ep, neg)
    p2 = lane_max(ep2)
    i2 = first_lane_of(ep2, p2, rest)
    denom = p1 + p2
    gate1 = g_prob * (p1 / denom)
    gate2 = g_prob * (p2 / denom)

    cnt = jnp.where((lane_f == i1) | (lane_f == i2), 1.0, 0.0)
    route = jnp.where(lane == ROUTE_EID, i1, 0.0)
    route = jnp.where(lane == ROUTE_EID + 1, i2, route)
    gate = jnp.where(lane == 0, gate1, jnp.where(lane == 1, gate2, 0.0))
    return route.astype(I32), gate, jnp.sum(cnt, axis=0, keepdims=True)


def outproj_router(lhs, ws, bias, h, g, w_router):
    n, d = h.shape
    tm = ROUTER_TM
    assert n % tm == 0
    row_blk = lambda width: pl.BlockSpec((tm, width), lambda i: (i, 0))
    full = lambda a: pl.BlockSpec(a.shape, lambda i: (0, 0))
    bias2, g2 = bias.reshape(1, d), g.reshape(1, d)
    args = [*lhs, *ws, bias2, h, g2, w_router]
    in_specs = ([row_blk(a.shape[1]) for a in lhs] + [full(w) for w in ws]
                + [full(bias2), row_blk(d), full(g2), full(w_router)])
    return pl.pallas_call(
        functools.partial(_outproj_router_kernel, n_lhs=len(lhs)),
        out_shape=(
            jax.ShapeDtypeStruct((n, d), F32), jax.ShapeDtypeStruct((n * d // LANES, LANES), F32),
            jax.ShapeDtypeStruct((n, LANES), I32), jax.ShapeDtypeStruct((n, LANES), F32),
            jax.ShapeDtypeStruct((8, LANES), I32),
        ),
        grid=(n // tm,),
        in_specs=in_specs,
        out_specs=(row_blk(d), pl.BlockSpec((tm * d // LANES, LANES), lambda i: (i, 0)),
                   row_blk(LANES), row_blk(LANES), pl.BlockSpec((8, LANES), lambda i: (0, 0))),
        scratch_shapes=[pltpu.VMEM((1, LANES), F32)],
        compiler_params=_cparams("arbitrary"),
        name="outproj_router",
    )(*args)


def router_weights(w_group, w_expert_router):
    d = w_group.shape[0]
    pad = jnp.zeros((d, LANES - N_EXPERTS - N_GROUPS), F32)
    return jnp.concatenate([w_expert_router.astype(F32), w_group.astype(F32), pad], axis=1)


IDX_CHUNK = 1024
IDX_RING = 4
DMA_UNROLL = 8


SUBLANES = 8


def _tile_store(ref, first_row, n_tokens, value):
    for c in range(value.shape[1] // LANES):
        ref[pl.ds(first_row + c, n_tokens, stride=SUBLANES), :] = value[:, c * LANES:(c + 1) * LANES]


def _tile_load(ref, first_row, n_tokens, width, pitch=SUBLANES):
    return jnp.concatenate([ref[pl.ds(first_row + c, n_tokens, stride=pitch), :]
                            for c in range(width // LANES)], axis=1)


def _token_rows(ref, token):
    start = token * SUBLANES
    if not isinstance(start, int):
        start = pl.multiple_of(start, SUBLANES)
    return ref.at[pl.ds(start, SUBLANES), :]


def _token_copy(src, src_token, dst, dst_token, sem):
    return pltpu.make_async_copy(_token_rows(src, src_token), _token_rows(dst, dst_token), sem)


def _moe_kernel(blk_exp, q0s, nvalids, nused, order_hbm, xn_hbm, wg_ref, wu_ref, wd_ref, out_hbm,
                ibuf, xbuf, ybuf, wg_bf, wu_bf, wd_bf, isem, gsem, ssem):
    i = pl.program_id(0)
    nu = nused[0]
    m = MOE_BLOCK
    d = wd_bf.shape[1]

    window = 2 * IDX_CHUNK

    def idx_copy(blk):
        base = pl.multiple_of(q0s[blk] & ~(IDX_CHUNK - 1), IDX_CHUNK)
        slot = blk & (IDX_RING - 1)
        dst = ibuf.at[pl.ds(pl.multiple_of(slot * window, window), window)]
        return pltpu.make_async_copy(order_hbm.at[pl.ds(base, window)], dst, isem.at[slot])

    def pair_base(blk):
        return (blk & (IDX_RING - 1)) * window + (q0s[blk] & (IDX_CHUNK - 1))

    def gather_group(blk_base, dst, sem, g):
        for u in range(DMA_UNROLL):
            r = g * DMA_UNROLL + u
            pair = ibuf[blk_base + r]
            first_row = pl.multiple_of((pair & ~1) * (SUBLANES // 2), SUBLANES)
            pltpu.make_async_copy(xn_hbm.at[pl.ds(first_row, SUBLANES), :], _token_rows(dst, r), sem).start()

    def scatter_group(blk_base, src, sem, g):
        for u in range(DMA_UNROLL):
            r = g * DMA_UNROLL + u
            _token_copy(src, r, out_hbm, ibuf[blk_base + r], sem).start()

    def issue_gathers(blk, unrolled=False):
        slot = blk & 1
        args = (pair_base(blk), xbuf.at[slot], gsem.at[slot])
        if unrolled:
            for g in range(m // DMA_UNROLL):
                gather_group(*args, g)
        else:
            lax.fori_loop(0, m // DMA_UNROLL, lambda g, c: (gather_group(*args, g), c)[1], 0)

    def wait_gathers(blk):
        slot = blk & 1
        pltpu.make_async_copy(xn_hbm.at[pl.ds(0, m * SUBLANES), :], xbuf.at[slot], gsem.at[slot]).wait()

    def issue_scatters(blk, unrolled=False):
        slot = blk & 1
        base = pair_base(blk)
        src = ybuf.at[slot]
        sem = ssem.at[slot]
        if unrolled:
            for g in range(m // DMA_UNROLL):
                scatter_group(base, src, sem, g)
            return
        nv = nvalids[blk]
        groups = lax.shift_right_logical(nv, DMA_UNROLL.bit_length() - 1)

        def tail(r, c):
            _token_copy(src, r, out_hbm, ibuf[base + r], sem).start()
            return c

        lax.fori_loop(0, groups, lambda g, c: (scatter_group(base, src, sem, g), c)[1], 0)
        lax.fori_loop(groups * DMA_UNROLL, nv, tail, 0)

    def wait_scatters(blk):
        slot = blk & 1
        nv = nvalids[blk]
        rows = pl.multiple_of(nv * SUBLANES, SUBLANES)

        @pl.when(nv > 0)
        def _():
            pltpu.make_async_copy(ybuf.at[slot].at[pl.ds(0, rows), :], out_hbm.at[pl.ds(0, rows), :],
                                  ssem.at[slot]).wait()

    @pl.when(i == 0)
    def _():
        first = idx_copy(0)
        first.start()
        first.wait()
        issue_gathers(0)

        @pl.when(nu > 1)
        def _():
            idx_copy(1).start()

    def expert(slot):
        x = _tile_load(xbuf.at[slot], 0, m, d).astype(BF16)
        gate = jnp.dot(x, wg_bf[...], preferred_element_type=F32)
        up = jnp.dot(x, wu_bf[...], preferred_element_type=F32)
        hidden = (gate * (1.0 / (1.0 + jnp.exp(-gate))) * up).astype(BF16)
        _tile_store(ybuf.at[slot], 0, m, jnp.dot(hidden, wd_bf[...], preferred_element_type=F32))

    @pl.when(i < nu)
    def _():
        @pl.when(i + 2 < nu)
        def _():
            idx_copy(i + 2).start()

        @pl.when(i + 1 < nu)
        def _():
            idx_copy(i + 1).wait()

        wait_gathers(i)

        @pl.when((i == 0) | (blk_exp[i] != blk_exp[jnp.maximum(i - 1, 0)]))
        def _():
            wg_bf[...] = wg_ref[0, 0].astype(BF16)
            wu_bf[...] = wu_ref[0, 0].astype(BF16)
            wd_bf[...] = wd_ref[0, 0].astype(BF16)

        @pl.when(i >= 2)
        def _():
            wait_scatters(i - 2)

        prev = jnp.maximum(i - 1, 0)
        steady = (i >= 1) & (i + 1 < nu) & (nvalids[prev] == m)

        @pl.when(steady)
        def _():
            issue_gathers(i + 1, unrolled=True)
            issue_scatters(i - 1, unrolled=True)
            expert(i & 1)

        @pl.when(jnp.logical_not(steady))
        def _():
            @pl.when(i + 1 < nu)
            def _():
                issue_gathers(i + 1)

            @pl.when(i >= 1)
            def _():
                issue_scatters(i - 1)

            expert(i & 1)

        @pl.when(i == nu - 1)
        def _():
            issue_scatters(i)

            @pl.when(i >= 1)
            def _():
                wait_scatters(i - 1)

            wait_scatters(i)


def moe_experts(order, plan, xn, w_gate, w_up, w_down, layer):
    blk_exp, q0s, nvalids, nused = plan
    d, de = w_gate.shape[2], w_gate.shape[3]
    assert d == SUBLANES * LANES
    rows = xn.shape[0]
    m = MOE_BLOCK
    w_in_spec = pl.BlockSpec((1, 1, d, de), lambda i, be, q0, nv, nu: (layer, be[i], 0, 0))
    return pl.pallas_call(
        _moe_kernel,
        out_shape=jax.ShapeDtypeStruct((2 * rows, LANES), F32),
        grid_spec=pltpu.PrefetchScalarGridSpec(
            num_scalar_prefetch=4,
            grid=(blk_exp.shape[0],),
            in_specs=[
                pl.BlockSpec(memory_space=pl.ANY), pl.BlockSpec(memory_space=pl.ANY),
                w_in_spec, w_in_spec,
                pl.BlockSpec((1, 1, de, d), lambda i, be, q0, nv, nu: (layer, be[i], 0, 0)),
            ],
            out_specs=pl.BlockSpec(memory_space=pl.ANY),
            scratch_shapes=[
                pltpu.SMEM((IDX_RING * 2 * IDX_CHUNK,), I32),
                pltpu.VMEM((2, m * SUBLANES, LANES), F32), pltpu.VMEM((2, m * SUBLANES, LANES), F32),
                pltpu.VMEM((d, de), BF16), pltpu.VMEM((d, de), BF16), pltpu.VMEM((de, d), BF16),
                pltpu.SemaphoreType.DMA((IDX_RING,)), pltpu.SemaphoreType.DMA((2,)), pltpu.SemaphoreType.DMA((2,)),
            ],
        ),
        compiler_params=_cparams("arbitrary"),
        name="moe_experts",
    )(blk_exp, q0s, nvalids, nused, order, xn, w_gate, w_up, w_down)


def _moe_finish_kernel(y_ref, gate_ref, h_ref, g_ref, o_ref, *, final_norm):
    gate = gate_ref[...]
    tm, d = h_ref.shape
    y0 = _tile_load(y_ref, 0, tm, d, pitch=2 * SUBLANES)
    y1 = _tile_load(y_ref, SUBLANES, tm, d, pitch=2 * SUBLANES)
    out = h_ref[...] + (y0 * gate[:, 0:1] + y1 * gate[:, 1:2])
    if final_norm:
        ms = jnp.mean(out * out, axis=-1, keepdims=True)
        out = out * lax.rsqrt(ms + EPS) * g_ref[...]
    o_ref[...] = out


def moe_finish(ys, gate, h, g, *, final_norm):
    n, d = h.shape
    tm = TOK_TM
    row_blk = lambda width: pl.BlockSpec((tm, width), lambda i: (i, 0))
    return pl.pallas_call(
        functools.partial(_moe_finish_kernel, final_norm=final_norm),
        out_shape=jax.ShapeDtypeStruct((n, d), F32),
        grid=(n // tm,),
        in_specs=[pl.BlockSpec((2 * tm * SUBLANES, LANES), lambda i: (i, 0)),
                  row_blk(LANES), row_blk(d), pl.BlockSpec((1, d), lambda i: (0, 0))],
        out_specs=row_blk(d),
        compiler_params=_cparams("arbitrary"),
        name="moe_finish",
    )(ys, gate, h, g.reshape(1, d))


def dispatch_plan(route, counts):
    m = MOE_BLOCK
    n = route.shape[0]
    assert (2 * n) % IDX_CHUNK == 0
    counts = counts[0, :N_EXPERTS]
    padded = ((counts + m - 1) // m) * m
    pend = jnp.cumsum(padded)
    pstart = pend - padded
    start = jnp.cumsum(counts) - counts
    eid = route[:, ROUTE_EID:ROUTE_EID + 2].reshape(-1)
    order = jnp.argsort(eid, stable=True).astype(I32)
    order = jnp.concatenate([order, jnp.zeros((2 * IDX_CHUNK,), I32)])
    nblk = (2 * n + N_EXPERTS * m) // m
    blk_start = jnp.arange(nblk, dtype=I32) * m
    nused = pend[-1] // m
    used = jnp.arange(nblk) < nused
    e = jnp.minimum(jnp.sum(blk_start[:, None] >= pend[None, :], axis=1), N_EXPERTS - 1)
    is_e = e[:, None] == jnp.arange(N_EXPERTS)[None, :]
    of_block = lambda table: jnp.sum(jnp.where(is_e, table[None, :], 0), axis=1)
    r0 = blk_start - of_block(pstart)
    q0s = jnp.where(used, of_block(start) + r0, 0)
    nvalids = jnp.where(used, jnp.clip(of_block(counts) - r0, 0, m), 0)
    last_exp = jnp.sum(jnp.where(jnp.arange(nblk) == nused - 1, e, 0))
    blk_exp = jnp.where(used, e, last_exp)
    as_i32 = lambda t: t.astype(I32)
    return order, (as_i32(blk_exp), as_i32(q0s), as_i32(nvalids), as_i32(nused).reshape(1))


def moe_layer(h1, xn, route, gate, counts, w_gate, w_up, w_down, layer, norm_g, *, final_norm):
    order, plan = dispatch_plan(route, counts)
    ys = moe_experts(order, plan, xn, w_gate, w_up, w_down, layer)
    return moe_finish(ys, gate, h1, norm_g, final_norm=final_norm)


def kernel(x, attn_norm_g, ffn_norm_g, w_in_ab, w_out_ab, ret_gn_g, w_in_c, b_in_c, sinks, w_out_c, b_out_c,
           w_group, w_expert_router, w_gate, w_up, w_down, final_norm_g):
    batch, seq, d = x.shape
    n = batch * seq
    depth = attn_norm_g.shape[0]
    cos, sin = rope_lane_tables(seq)
    later = (jnp.arange(min(SB_TILE, seq))[:, None] > jnp.arange(min(SB_TILE, seq))[None, :]).astype(BF16)
    w_sb = N_HEADS_SB * HEAD_DIM
    w_ret = N_HEADS_RET * HEAD_DIM
    h = x.reshape(n, d)
    for layer in range(depth):
        i = layer // 2
        last = layer == depth - 1
        if layer % 2 == 0:
            proj = norm_proj(h, attn_norm_g[layer], w_in_ab[i].astype(BF16), jnp.zeros((w_in_ab.shape[2],), F32),
                             cos, sin, seq=seq, tn=w_sb,
                             col_ops=("scale", "", "", "rope", "rope scale", "", ""))
            per = w_sb // LANES
            a = sb_attention(proj, later, batch=batch, seq=seq, q_blk=0, k_blk=per, v_blk=2 * per)
            r = retention(proj, ret_gn_g[i], batch=batch, seq=seq,
                          q_blk=3 * per, k_blk=4 * per, v_blk=5 * per, g_blk=6 * per)
            w_out = w_out_ab[i].astype(BF16)
            lhs, ws = [a, r], [w_out[:w_sb], w_out[w_sb:]]
            bias = jnp.zeros((d,), F32)
        else:
            wkv = N_KV_SWA * HEAD_DIM
            nq = N_HEADS_SWA * HEAD_DIM // wkv
            proj = norm_proj(h, attn_norm_g[layer], w_in_c[i].astype(BF16), b_in_c[i].astype(F32),
                             cos, sin, seq=seq, tn=wkv,
                             col_ops=("rope scale",) * nq + ("rope", ""))
            o = swa_attention(proj, sinks[i], batch=batch, seq=seq)
            lhs, ws = [o], [w_out_c[i].astype(BF16)]
            bias = b_out_c[i].astype(F32)
        h1, xn, route, gate, counts = outproj_router(
            lhs, ws, bias, h, ffn_norm_g[layer], router_weights(w_group[layer], w_expert_router[layer]))
        h = moe_layer(h1, xn, route, gate, counts, w_gate, w_up, w_down, layer, final_norm_g, final_norm=last)
    return h.reshape(batch, seq, d)
```

```python
import functools

import jax
import jax.numpy as jnp
import numpy as np
from jax import lax
from jax.experimental import pallas as pl
from jax.experimental.pallas import tpu as pltpu

F32 = jnp.float32
BF16 = jnp.bfloat16
I32 = jnp.int32

HEAD_DIM = 64
N_HEADS_SB = 8
N_HEADS_RET = 8
N_HEADS_SWA = 16
N_KV_SWA = 4
WINDOW = 128
ROPE_THETA = 10000.0
N_GROUPS = 4
EXPERTS_PER_GROUP = 8
N_EXPERTS = N_GROUPS * EXPERTS_PER_GROUP
MOE_BLOCK = 256
EPS = 1e-6

LANES = 128
HALF = HEAD_DIM // 2
QK_SCALE = HEAD_DIM ** -0.5

SB_TILE = 256
RET_CHUNK = 256
PROJ_TM = 512
TOK_TM = 512
ROUTER_TM = 512
ROUTER_SUB = 256
VMEM_LIMIT = 48 * 1024 * 1024

_NT = (((1,), (1,)), ((), ()))
_TN = (((0,), (0,)), ((), ()))


def _cparams(*sem):
    return pltpu.CompilerParams(dimension_semantics=sem, vmem_limit_bytes=VMEM_LIMIT)


def _head0_mask():
    return lax.broadcasted_iota(I32, (1, LANES), 1) < HEAD_DIM


def _norm_proj_kernel(x_ref, g_ref, w_ref, b_ref, cos_ref, sin_ref, o_ref, *, tn, col_ops):
    x = x_ref[...]
    ms = jnp.mean(x * x, axis=-1, keepdims=True)
    xn = (x * lax.rsqrt(ms + EPS) * g_ref[...]).astype(BF16)
    lane = lax.broadcasted_iota(I32, (1, LANES), 1)
    first_half = (lane % HEAD_DIM) < HALF
    for j, op in enumerate(col_ops):
        cols = slice(j * tn, (j + 1) * tn)
        acc = jnp.dot(xn, w_ref[:, cols], preferred_element_type=F32) + b_ref[:, cols]
        if "rope" in op:
            cos = cos_ref[...]
            sin = sin_ref[...]
            slabs = []
            for s in range(tn // LANES):
                a = acc[:, s * LANES:(s + 1) * LANES]
                partner = jnp.where(first_half, pltpu.roll(a, LANES - HALF, 1), pltpu.roll(a, HALF, 1))
                slabs.append(a * cos + partner * sin)
            acc = jnp.concatenate(slabs, axis=1)
        if "scale" in op:
            acc = acc * QK_SCALE
        o_ref[:, cols] = acc.astype(BF16)


def norm_proj(x, g, w, b, cos, sin, *, seq, tn, col_ops):
    n, d = x.shape
    f = w.shape[1]
    tm = min(PROJ_TM, seq)
    assert n % tm == 0 and seq % tm == 0 and f == tn * len(col_ops)
    pos_blocks = seq // tm
    return pl.pallas_call(
        functools.partial(_norm_proj_kernel, tn=tn, col_ops=col_ops),
        out_shape=jax.ShapeDtypeStruct((n, f), BF16),
        grid=(n // tm,),
        in_specs=[
            pl.BlockSpec((tm, d), lambda i: (i, 0)),
            pl.BlockSpec((1, d), lambda i: (0, 0)),
            pl.BlockSpec((d, f), lambda i: (0, 0)),
            pl.BlockSpec((1, f), lambda i: (0, 0)),
            pl.BlockSpec((tm, LANES), lambda i: (i % pos_blocks, 0)),
            pl.BlockSpec((tm, LANES), lambda i: (i % pos_blocks, 0)),
        ],
        out_specs=pl.BlockSpec((tm, f), lambda i: (i, 0)),
        compiler_params=_cparams("arbitrary"),
        name="norm_proj",
    )(x, g.reshape(1, d), w, b.reshape(1, f), cos, sin)


def rope_lane_tables(seq):
    pos = jnp.arange(seq, dtype=F32)
    inv = ROPE_THETA ** (-jnp.arange(0, HEAD_DIM, 2, dtype=F32) / HEAD_DIM)
    ang = pos[:, None] * inv[None, :]
    cos, sin = jnp.cos(ang), jnp.sin(ang)
    return jnp.tile(cos, (1, 4)), jnp.tile(jnp.concatenate([-sin, sin], axis=1), (1, 2))


SB_PAIRS = 4


def _sb_kernel(q_ref, k_ref, v_ref, t_ref, o_ref, acc_ref, carry_ref, *, tile):
    qi = pl.program_id(2)
    m0 = _head0_mask()
    zero = jnp.zeros((tile, LANES), BF16)
    row = lax.broadcasted_iota(I32, (tile, tile), 0)
    col = lax.broadcasted_iota(I32, (tile, tile), 1)
    strict = col < row

    def key_tile(kb, diag):
        start = pl.multiple_of(kb * tile, tile)
        for p in range(SB_PAIRS):
            lanes = slice(p * LANES, (p + 1) * LANES)
            q = q_ref[:, lanes]
            k = k_ref[pl.ds(start, tile), lanes]
            v = v_ref[pl.ds(start, tile), lanes]
            out = None
            for h in range(2):
                qh = jnp.where(m0, q, zero) if h == 0 else jnp.where(m0, zero, q)
                vh = jnp.where(m0, v, zero) if h == 0 else jnp.where(m0, zero, v)
                z = lax.dot_general(qh, k, _NT, preferred_element_type=F32)
                log_fail = -(jnp.maximum(z, 0.0) + jnp.log(1.0 + jnp.exp(-jnp.abs(z))))
                if diag:
                    log_fail = jnp.where(strict, log_fail, 0.0)
                after = jnp.dot(log_fail.astype(BF16), t_ref[...], preferred_element_type=F32)
                logw = z + log_fail + after
                if not diag:
                    logw = logw + carry_ref[2 * p + h]
                w = jnp.exp(logw)
                if diag:
                    w = jnp.where(strict, w, 0.0)
                pv = jnp.dot(w.astype(BF16), vh, preferred_element_type=F32)
                out = pv if out is None else out + pv
                tile_sum = jnp.sum(log_fail, axis=-1, keepdims=True)
                if diag:
                    carry_ref[2 * p + h] = tile_sum
                else:
                    carry_ref[2 * p + h] += tile_sum
            if diag:
                acc_ref[:, lanes] = out
            else:
                acc_ref[:, lanes] += out

    key_tile(qi, True)

    def body(j, c):
        key_tile(qi - 1 - j, False)
        return c

    lax.fori_loop(0, qi, body, 0)
    o_ref[...] = acc_ref[...].astype(BF16)


def sb_attention(proj, later_mat, *, batch, seq, q_blk, k_blk, v_blk):
    n = proj.shape[0]
    tile = min(SB_TILE, seq)
    nq = seq // tile
    width = SB_PAIRS * LANES
    steps = N_HEADS_SB * HEAD_DIM // width
    assert q_blk % SB_PAIRS == 0 and k_blk % SB_PAIRS == 0 and v_blk % SB_PAIRS == 0
    col = lambda blk: blk // SB_PAIRS
    return pl.pallas_call(
        functools.partial(_sb_kernel, tile=tile),
        out_shape=jax.ShapeDtypeStruct((n, steps * width), BF16),
        grid=(batch, steps, nq),
        in_specs=[
            pl.BlockSpec((tile, width), lambda b, p, i: (b * nq + i, col(q_blk) + p)),
            pl.BlockSpec((seq, width), lambda b, p, i: (b, col(k_blk) + p)),
            pl.BlockSpec((seq, width), lambda b, p, i: (b, col(v_blk) + p)),
            pl.BlockSpec((tile, tile), lambda b, p, i: (0, 0)),
        ],
        out_specs=pl.BlockSpec((tile, width), lambda b, p, i: (b * nq + i, p)),
        scratch_shapes=[pltpu.VMEM((tile, width), F32), pltpu.VMEM((2 * SB_PAIRS, tile, 1), F32)],
        compiler_params=_cparams("arbitrary", "arbitrary", "arbitrary"),
        name="sb_attention",
    )(proj, proj, proj, later_mat)


RET_PAIRS = 4


def _retention_kernel(q_ref, k_ref, v_ref, gate_ref, gain_ref, dec_ref, xi_ref, zeta_ref, gch_ref,
                      o_ref, state_ref):
    c = pl.program_id(2)

    @pl.when(c == 0)
    def _():
        state_ref[...] = jnp.zeros_like(state_ref)

    m0 = _head0_mask()
    zero = jnp.zeros((q_ref.shape[0], LANES), BF16)
    r = lax.broadcasted_iota(I32, (LANES, LANES), 0) < HEAD_DIM
    cc = lax.broadcasted_iota(I32, (LANES, LANES), 1) < HEAD_DIM
    same_head = r == cc

    def head_mean(t):
        s0 = jnp.sum(jnp.where(m0, t, 0.0), axis=-1, keepdims=True)
        s1 = jnp.sum(jnp.where(m0, 0.0, t), axis=-1, keepdims=True)
        return jnp.where(m0, s0, s1) * (1.0 / HEAD_DIM)

    for p in range(RET_PAIRS):
        lanes = slice(p * LANES, (p + 1) * LANES)
        q = q_ref[:, lanes]
        k = k_ref[:, lanes]
        v = v_ref[:, lanes]
        state = state_ref[p]
        y = jnp.dot(q, state.astype(BF16), preferred_element_type=F32) * xi_ref[:, lanes]
        for h in range(2):
            qh = jnp.where(m0, q, zero) if h == 0 else jnp.where(m0, zero, q)
            vh = jnp.where(m0, v, zero) if h == 0 else jnp.where(m0, zero, v)
            s = lax.dot_general(qh, k, _NT, preferred_element_type=F32)
            inner = (s * dec_ref[2 * p + h]).astype(BF16)
            y = y + jnp.dot(inner, vh, preferred_element_type=F32)

        kz = (k.astype(F32) * zeta_ref[:, lanes]).astype(BF16)
        upd = lax.dot_general(kz, v, _TN, preferred_element_type=F32)
        state_ref[p] = state * gch_ref[:, lanes] + jnp.where(same_head, upd, 0.0)

        d = y - head_mean(y)
        yn = d * lax.rsqrt(head_mean(d * d) + EPS) * gain_ref[:, lanes]
        g = gate_ref[:, lanes].astype(F32)
        o_ref[:, lanes] = (yn * (g * (1.0 / (1.0 + jnp.exp(-g))))).astype(BF16)


def retention_tables(chunk):
    h = N_HEADS_RET
    log_g = jnp.log(1.0 - 2.0 ** (-5.0 - jnp.arange(h, dtype=F32)))
    idx = jnp.arange(chunk, dtype=F32)
    diff = idx[:, None] - idx[None, :]
    dec = jnp.where(diff[None] >= 0, jnp.exp(log_g[:, None, None] * jnp.maximum(diff, 0.0)[None]), 0.0)
    xi = jnp.exp(log_g[:, None] * (idx[None, :] + 1.0))
    zeta = jnp.exp(log_g[:, None] * (chunk - 1.0 - idx[None, :]))
    gch = jnp.exp(log_g * chunk)
    lanes = lambda t: jnp.repeat(t.T, HEAD_DIM, axis=1)
    return dec, lanes(xi), lanes(zeta), jnp.repeat(gch, HEAD_DIM)[None, :]


def retention(proj, gain, *, batch, seq, q_blk, k_blk, v_blk, g_blk):
    n = proj.shape[0]
    chunk = min(RET_CHUNK, seq)
    nc = seq // chunk
    width = RET_PAIRS * LANES
    steps = N_HEADS_RET * HEAD_DIM // width
    assert all(b % RET_PAIRS == 0 for b in (q_blk, k_blk, v_blk, g_blk))
    dec, xi, zeta, gch = retention_tables(chunk)
    blk = lambda off: pl.BlockSpec((chunk, width), lambda b, p, c: (b * nc + c, off // RET_PAIRS + p))
    return pl.pallas_call(
        _retention_kernel,
        out_shape=jax.ShapeDtypeStruct((n, steps * width), BF16),
        grid=(batch, steps, nc),
        in_specs=[
            blk(q_blk), blk(k_blk), blk(v_blk), blk(g_blk),
            pl.BlockSpec((1, width), lambda b, p, c: (0, p)),
            pl.BlockSpec((2 * RET_PAIRS, chunk, chunk), lambda b, p, c: (p, 0, 0)),
            pl.BlockSpec((chunk, width), lambda b, p, c: (0, p)),
            pl.BlockSpec((chunk, width), lambda b, p, c: (0, p)),
            pl.BlockSpec((1, width), lambda b, p, c: (0, p)),
        ],
        out_specs=pl.BlockSpec((chunk, width), lambda b, p, c: (b * nc + c, p)),
        scratch_shapes=[pltpu.VMEM((RET_PAIRS, LANES, LANES), F32)],
        compiler_params=_cparams("arbitrary", "arbitrary", "arbitrary"),
        name="retention",
    )(proj, proj, proj, proj, gain.reshape(1, -1).astype(F32), dec, xi, zeta, gch)


def _swa_kernel(sink_ref, q_ref, kp_ref, kc_ref, vp_ref, vc_ref, o_ref):
    blk = pl.program_id(1)
    m0 = _head0_mask()
    w = WINDOW
    qi = lax.broadcasted_iota(I32, (w, 2 * w), 0)
    kj = lax.broadcasted_iota(I32, (w, 2 * w), 1)
    rel = qi + w - kj
    valid = (rel >= 0) & (rel < w) & ((blk > 0) | (kj >= w))
    kband = jnp.concatenate([kp_ref[...], kc_ref[...]], axis=0)
    vband = jnp.concatenate([vp_ref[...], vc_ref[...]], axis=0)
    group = N_HEADS_SWA // N_KV_SWA
    slabs = group * HEAD_DIM // LANES
    swap = lambda t: jnp.concatenate([t[:, HEAD_DIM:], t[:, :HEAD_DIM]], axis=1)
    zero_kv = jnp.zeros((2 * w, LANES), BF16)
    zero_q = jnp.zeros((slabs * w, LANES), BF16)
    ones_kv = jnp.ones((2 * w, LANES), BF16)

    def half_heads(q_stack, k_half, v_half, first_head):
        s = lax.dot_general(q_stack, k_half, _NT, preferred_element_type=F32)
        probs, sink_terms = [], []
        for slab in range(slabs):
            sink = sink_ref[first_head + 2 * slab]
            sh = jnp.where(valid, s[slab * w:(slab + 1) * w], -jnp.inf)
            m = jnp.maximum(jnp.max(sh, axis=-1, keepdims=True), sink)
            probs.append(jnp.exp(sh - m).astype(BF16))
            sink_terms.append(jnp.exp(sink - jnp.broadcast_to(m, (w, LANES))))
        p = jnp.concatenate(probs, axis=0)
        pv = jnp.dot(p, jnp.concatenate([v_half, ones_kv], axis=1), preferred_element_type=F32)
        denom = pv[:, LANES:] + jnp.concatenate(sink_terms, axis=0)
        return pv[:, :LANES] / denom

    for pair in range(N_KV_SWA * HEAD_DIM // LANES):
        k2 = kband[:, pair * LANES:(pair + 1) * LANES]
        v2 = vband[:, pair * LANES:(pair + 1) * LANES]
        k2s, v2s = swap(k2), swap(v2)
        for c in range(2):
            kvh = 2 * pair + c
            k_even, k_odd = (k2, k2s) if c == 0 else (k2s, k2)
            v_even, v_odd = (v2, v2s) if c == 0 else (v2s, v2)
            base = kvh * group * HEAD_DIM
            q_stack = jnp.concatenate([q_ref[:, base + t * LANES:base + (t + 1) * LANES] for t in range(slabs)],
                                      axis=0)
            out = (half_heads(jnp.where(m0, q_stack, zero_q), k_even, jnp.where(m0, v_even, zero_kv), kvh * group)
                   + half_heads(jnp.where(m0, zero_q, q_stack), k_odd, jnp.where(m0, zero_kv, v_odd),
                                kvh * group + 1))
            for t in range(slabs):
                o_ref[:, base + t * LANES:base + (t + 1) * LANES] = out[t * w:(t + 1) * w].astype(BF16)


def swa_attention(proj, sinks, *, batch, seq):
    n = proj.shape[0]
    w = WINDOW
    nb = seq // w
    wq = N_HEADS_SWA * HEAD_DIM
    wkv = N_KV_SWA * HEAD_DIM
    k_blk = wq // wkv
    cur = lambda off: pl.BlockSpec((w, wkv), lambda b, i: (b * nb + i, off))
    prev = lambda off: pl.BlockSpec((w, wkv), lambda b, i: (b * nb + jnp.maximum(i - 1, 0), off))
    return pl.pallas_call(
        _swa_kernel,
        out_shape=jax.ShapeDtypeStruct((n, wq), BF16),
        grid=(batch, nb),
        in_specs=[
            pl.BlockSpec(memory_space=pltpu.SMEM),
            pl.BlockSpec((w, wq), lambda b, i: (b * nb + i, 0)),
            prev(k_blk), cur(k_blk), prev(k_blk + 1), cur(k_blk + 1),
        ],
        out_specs=pl.BlockSpec((w, wq), lambda b, i: (b * nb + i, 0)),
        compiler_params=_cparams("arbitrary", "arbitrary"),
        name="swa_attention",
    )(sinks.astype(F32), proj, proj, proj, proj, proj)


ROUTE_EID = 0
GROUP_LANE = N_EXPERTS


def _split_bf16(t):
    hi = t.astype(BF16)
    return hi, (t - hi.astype(F32)).astype(BF16)


def _outproj_router_kernel(*refs, n_lhs):
    lhs = refs[:n_lhs]
    ws = refs[n_lhs:2 * n_lhs]
    b_ref, h_ref, g_ref, wr_ref = refs[2 * n_lhs:2 * n_lhs + 4]
    h1_ref, xn_ref, route_ref, gate_ref, cnt_ref, carry_ref = refs[2 * n_lhs + 4:]
    step = pl.program_id(0)

    @pl.when(step == 0)
    def _():
        carry_ref[...] = jnp.zeros_like(carry_ref)

    w_split = jnp.concatenate(_split_bf16(wr_ref[...]), axis=1)
    for sub in range(h_ref.shape[0] // ROUTER_SUB):
        rows = slice(sub * ROUTER_SUB, (sub + 1) * ROUTER_SUB)
        mix = b_ref[...]
        for a_ref, w_ref in zip(lhs, ws):
            mix = mix + jnp.dot(a_ref[rows], w_ref[...], preferred_element_type=F32)
        h1 = h_ref[rows] + mix
        h1_ref[rows] = h1
        ms = jnp.mean(h1 * h1, axis=-1, keepdims=True)
        xn = h1 * lax.rsqrt(ms + EPS) * g_ref[...]
        _tile_store(xn_ref, sub * ROUTER_SUB * SUBLANES, ROUTER_SUB, xn)
        x_hi, x_lo = _split_bf16(xn)
        parts = (jnp.dot(x_hi, w_split, preferred_element_type=F32)
                 + jnp.dot(x_lo, w_split, preferred_element_type=F32))
        logits = parts[:, :LANES] + parts[:, LANES:]
        route, gate, cnt = _route(logits)
        route_ref[rows] = route
        gate_ref[rows] = gate
        carry_ref[...] += cnt
    cnt_ref[...] = jnp.broadcast_to(carry_ref[...], cnt_ref.shape).astype(I32)


def _route(logits):
    tm = logits.shape[0]
    lane = lax.broadcasted_iota(I32, (tm, LANES), 1)
    lane_f = lane.astype(F32)
    neg = -jnp.inf

    def lane_max(t):
        return jnp.max(t, axis=-1, keepdims=True)

    def lane_sum(t):
        return jnp.sum(t, axis=-1, keepdims=True)

    def first_lane_of(t, value, mask):
        return jnp.min(jnp.where(mask & (t == value), lane_f, float(LANES)), axis=-1, keepdims=True)

    is_group = (lane >= GROUP_LANE) & (lane < GROUP_LANE + N_GROUPS)
    gl = jnp.where(is_group, logits, neg)
    ge = jnp.exp(gl - lane_max(gl))
    gp = ge / lane_sum(ge)
    g_prob = lane_max(gp)
    g_idx = first_lane_of(gp, g_prob, is_group) - float(GROUP_LANE)
    group_of_lane = lax.shift_right_logical(lane, int(np.log2(EXPERTS_PER_GROUP))).astype(F32)
    in_group = (lane < N_EXPERTS) & (group_of_lane == g_idx)
    el = jnp.where(in_group, logits, neg)
    ee = jnp.exp(el - lane_max(el))
    ep = ee / lane_sum(ee)
    p1 = lane_max(ep)
    i1 = first_lane_of(ep, p1, in_group)
    rest = in_group & (lane_f != i1)
    ep2 = jnp.where(rest, ep, neg)
    p2 = lane_max(ep2)
    i2 = first_lane_of(ep2, p2, rest)
    denom = p1 + p2
    gate1 = g_prob * (p1 / denom)
    gate2 = g_prob * (p2 / denom)

    cnt = jnp.where((lane_f == i1) | (lane_f == i2), 1.0, 0.0)
    route = jnp.where(lane == ROUTE_EID, i1, 0.0)
    route = jnp.where(lane == ROUTE_EID + 1, i2, route)
    gate = jnp.where(lane == 0, gate1, jnp.where(lane == 1, gate2, 0.0))
    return route.astype(I32), gate, jnp.sum(cnt, axis=0, keepdims=True)


def outproj_router(lhs, ws, bias, h, g, w_router):
    n, d = h.shape
    tm = ROUTER_TM
    assert n % tm == 0
    row_blk = lambda width: pl.BlockSpec((tm, width), lambda i: (i, 0))
    full = lambda a: pl.BlockSpec(a.shape, lambda i: (0, 0))
    bias2, g2 = bias.reshape(1, d), g.reshape(1, d)
    args = [*lhs, *ws, bias2, h, g2, w_router]
    in_specs = ([row_blk(a.shape[1]) for a in lhs] + [full(w) for w in ws]
                + [full(bias2), row_blk(d), full(g2), full(w_router)])
    return pl.pallas_call(
        functools.partial(_outproj_router_kernel, n_lhs=len(lhs)),
        out_shape=(
            jax.ShapeDtypeStruct((n, d), F32), jax.ShapeDtypeStruct((n * d // LANES, LANES), F32),
            jax.ShapeDtypeStruct((n, LANES), I32), jax.ShapeDtypeStruct((n, LANES), F32),
            jax.ShapeDtypeStruct((8, LANES), I32),
        ),
        grid=(n // tm,),
        in_specs=in_specs,
        out_specs=(row_blk(d), pl.BlockSpec((tm * d // LANES, LANES), lambda i: (i, 0)),
                   row_blk(LANES), row_blk(LANES), pl.BlockSpec((8, LANES), lambda i: (0, 0))),
        scratch_shapes=[pltpu.VMEM((1, LANES), F32)],
        compiler_params=_cparams("arbitrary"),
        name="outproj_router",
    )(*args)


def router_weights(w_group, w_expert_router):
    d = w_group.shape[0]
    pad = jnp.zeros((d, LANES - N_EXPERTS - N_GROUPS), F32)
    return jnp.concatenate([w_expert_router.astype(F32), w_group.astype(F32), pad], axis=1)


IDX_CHUNK = 1024
IDX_RING = 4
OUT_RING = 4
DMA_UNROLL = 8


SUBLANES = 8


def _tile_store(ref, first_row, n_tokens, value):
    for c in range(value.shape[1] // LANES):
        ref[pl.ds(first_row + c, n_tokens, stride=SUBLANES), :] = value[:, c * LANES:(c + 1) * LANES]


def _tile_load(ref, first_row, n_tokens, width, pitch=SUBLANES):
    return jnp.concatenate([ref[pl.ds(first_row + c, n_tokens, stride=pitch), :]
                            for c in range(width // LANES)], axis=1)


def _token_rows(ref, token):
    start = token * SUBLANES
    if not isinstance(start, int):
        start = pl.multiple_of(start, SUBLANES)
    return ref.at[pl.ds(start, SUBLANES), :]


def _token_copy(src, src_token, dst, dst_token, sem):
    return pltpu.make_async_copy(_token_rows(src, src_token), _token_rows(dst, dst_token), sem)


def _moe_kernel(blk_exp, q0s, nvalids, nused, order_hbm, xn_hbm, wg_ref, wu_ref, wd_ref, out_hbm,
                ibuf, xbuf, ybuf, wg_bf, wu_bf, wd_bf, isem, gsem, ssem):
    i = pl.program_id(0)
    nu = nused[0]
    m = MOE_BLOCK
    d = wd_bf.shape[1]

    window = 2 * IDX_CHUNK

    def idx_copy(blk):
        base = pl.multiple_of(q0s[blk] & ~(IDX_CHUNK - 1), IDX_CHUNK)
        slot = blk & (IDX_RING - 1)
        dst = ibuf.at[pl.ds(pl.multiple_of(slot * window, window), window)]
        return pltpu.make_async_copy(order_hbm.at[pl.ds(base, window)], dst, isem.at[slot])

    def pair_base(blk):
        return (blk & (IDX_RING - 1)) * window + (q0s[blk] & (IDX_CHUNK - 1))

    def gather_group(blk_base, dst, sem, g):
        for u in range(DMA_UNROLL):
            r = g * DMA_UNROLL + u
            pair = ibuf[blk_base + r]
            first_row = pl.multiple_of((pair & ~1) * (SUBLANES // 2), SUBLANES)
            pltpu.make_async_copy(xn_hbm.at[pl.ds(first_row, SUBLANES), :], _token_rows(dst, r), sem).start()

    def scatter_group(blk_base, src, sem, g):
        for u in range(DMA_UNROLL):
            r = g * DMA_UNROLL + u
            _token_copy(src, r, out_hbm, ibuf[blk_base + r], sem).start()

    def issue_gathers(blk, unrolled=False):
        slot = blk & 1
        args = (pair_base(blk), xbuf.at[slot], gsem.at[slot])
        if unrolled:
            for g in range(m // DMA_UNROLL):
                gather_group(*args, g)
        else:
            lax.fori_loop(0, m // DMA_UNROLL, lambda g, c: (gather_group(*args, g), c)[1], 0)

    def wait_gathers(blk):
        slot = blk & 1
        pltpu.make_async_copy(xn_hbm.at[pl.ds(0, m * SUBLANES), :], xbuf.at[slot], gsem.at[slot]).wait()

    def issue_scatters(blk, unrolled=False):
        slot = blk & (OUT_RING - 1)
        base = pair_base(blk)
        src = ybuf.at[slot]
        sem = ssem.at[slot]
        if unrolled:
            for g in range(m // DMA_UNROLL):
                scatter_group(base, src, sem, g)
            return
        nv = nvalids[blk]
        groups = lax.shift_right_logical(nv, DMA_UNROLL.bit_length() - 1)

        def tail(r, c):
            _token_copy(src, r, out_hbm, ibuf[base + r], sem).start()
            return c

        lax.fori_loop(0, groups, lambda g, c: (scatter_group(base, src, sem, g), c)[1], 0)
        lax.fori_loop(groups * DMA_UNROLL, nv, tail, 0)

    def wait_scatters(blk):
        slot = blk & (OUT_RING - 1)
        nv = nvalids[blk]
        rows = pl.multiple_of(nv * SUBLANES, SUBLANES)

        @pl.when(nv > 0)
        def _():
            pltpu.make_async_copy(ybuf.at[slot].at[pl.ds(0, rows), :], out_hbm.at[pl.ds(0, rows), :],
                                  ssem.at[slot]).wait()

    @pl.when(i == 0)
    def _():
        first = idx_copy(0)
        first.start()
        first.wait()
        issue_gathers(0)

        @pl.when(nu > 1)
        def _():
            idx_copy(1).start()

    def expert():
        x = _tile_load(xbuf.at[i & 1], 0, m, d).astype(BF16)
        gate = jnp.dot(x, wg_bf[...], preferred_element_type=F32)
        up = jnp.dot(x, wu_bf[...], preferred_element_type=F32)
        hidden = (gate * (1.0 / (1.0 + jnp.exp(-gate))) * up).astype(BF16)
        _tile_store(ybuf.at[i & (OUT_RING - 1)], 0, m, jnp.dot(hidden, wd_bf[...], preferred_element_type=F32))

    @pl.when(i < nu)
    def _():
        @pl.when(i + 2 < nu)
        def _():
            idx_copy(i + 2).start()

        @pl.when(i + 1 < nu)
        def _():
            idx_copy(i + 1).wait()

        wait_gathers(i)

        @pl.when((i == 0) | (blk_exp[i] != blk_exp[jnp.maximum(i - 1, 0)]))
        def _():
            wg_bf[...] = wg_ref[0, 0].astype(BF16)
            wu_bf[...] = wu_ref[0, 0].astype(BF16)
            wd_bf[...] = wd_ref[0, 0].astype(BF16)

        @pl.when(i >= OUT_RING - 1)
        def _():
            wait_scatters(i - (OUT_RING - 1))

        prev = jnp.maximum(i - 1, 0)
        steady = (i >= 1) & (i + 1 < nu) & (nvalids[prev] == m)

        @pl.when(steady)
        def _():
            issue_gathers(i + 1, unrolled=True)
            issue_scatters(i - 1, unrolled=True)
            expert()

        @pl.when(jnp.logical_not(steady))
        def _():
            @pl.when(i + 1 < nu)
            def _():
                issue_gathers(i + 1)

            @pl.when(i >= 1)
            def _():
                issue_scatters(i - 1)

            expert()

        @pl.when(i == nu - 1)
        def _():
            issue_scatters(i)
            for back in range(OUT_RING - 2, -1, -1):
                @pl.when(i >= back)
                def _():
                    wait_scatters(i - back)


def moe_experts(order, plan, xn, w_gate, w_up, w_down, layer):
    blk_exp, q0s, nvalids, nused = plan
    d, de = w_gate.shape[2], w_gate.shape[3]
    assert d == SUBLANES * LANES
    rows = xn.shape[0]
    m = MOE_BLOCK
    w_in_spec = pl.BlockSpec((1, 1, d, de), lambda i, be, q0, nv, nu: (layer, be[i], 0, 0))
    return pl.pallas_call(
        _moe_kernel,
        out_shape=jax.ShapeDtypeStruct((2 * rows, LANES), F32),
        grid_spec=pltpu.PrefetchScalarGridSpec(
            num_scalar_prefetch=4,
            grid=(blk_exp.shape[0],),
            in_specs=[
                pl.BlockSpec(memory_space=pl.ANY), pl.BlockSpec(memory_space=pl.ANY),
                w_in_spec, w_in_spec,
                pl.BlockSpec((1, 1, de, d), lambda i, be, q0, nv, nu: (layer, be[i], 0, 0)),
            ],
            out_specs=pl.BlockSpec(memory_space=pl.ANY),
            scratch_shapes=[
                pltpu.SMEM((IDX_RING * 2 * IDX_CHUNK,), I32),
                pltpu.VMEM((2, m * SUBLANES, LANES), F32), pltpu.VMEM((OUT_RING, m * SUBLANES, LANES), F32),
                pltpu.VMEM((d, de), BF16), pltpu.VMEM((d, de), BF16), pltpu.VMEM((de, d), BF16),
                pltpu.SemaphoreType.DMA((IDX_RING,)), pltpu.SemaphoreType.DMA((2,)),
                pltpu.SemaphoreType.DMA((OUT_RING,)),
            ],
        ),
        compiler_params=_cparams("arbitrary"),
        name="moe_experts",
    )(blk_exp, q0s, nvalids, nused, order, xn, w_gate, w_up, w_down)


def _moe_finish_kernel(y_ref, gate_ref, h_ref, g_ref, o_ref, *, final_norm):
    gate = gate_ref[...]
    tm, d = h_ref.shape
    y0 = _tile_load(y_ref, 0, tm, d, pitch=2 * SUBLANES)
    y1 = _tile_load(y_ref, SUBLANES, tm, d, pitch=2 * SUBLANES)
    out = h_ref[...] + (y0 * gate[:, 0:1] + y1 * gate[:, 1:2])
    if final_norm:
        ms = jnp.mean(out * out, axis=-1, keepdims=True)
        out = out * lax.rsqrt(ms + EPS) * g_ref[...]
    o_ref[...] = out


def moe_finish(ys, gate, h, g, *, final_norm):
    n, d = h.shape
    tm = TOK_TM
    row_blk = lambda width: pl.BlockSpec((tm, width), lambda i: (i, 0))
    return pl.pallas_call(
        functools.partial(_moe_finish_kernel, final_norm=final_norm),
        out_shape=jax.ShapeDtypeStruct((n, d), F32),
        grid=(n // tm,),
        in_specs=[pl.BlockSpec((2 * tm * SUBLANES, LANES), lambda i: (i, 0)),
                  row_blk(LANES), row_blk(d), pl.BlockSpec((1, d), lambda i: (0, 0))],
        out_specs=row_blk(d),
        compiler_params=_cparams("arbitrary"),
        name="moe_finish",
    )(ys, gate, h, g.reshape(1, d))


def dispatch_plan(route, counts):
    m = MOE_BLOCK
    n = route.shape[0]
    assert (2 * n) % IDX_CHUNK == 0
    counts = counts[0, :N_EXPERTS]
    padded = ((counts + m - 1) // m) * m
    pend = jnp.cumsum(padded)
    pstart = pend - padded
    start = jnp.cumsum(counts) - counts
    eid = route[:, ROUTE_EID:ROUTE_EID + 2].reshape(-1)
    order = jnp.argsort(eid, stable=True).astype(I32)
    order = jnp.concatenate([order, jnp.zeros((2 * IDX_CHUNK,), I32)])
    nblk = (2 * n + N_EXPERTS * m) // m
    blk_start = jnp.arange(nblk, dtype=I32) * m
    nused = pend[-1] // m
    used = jnp.arange(nblk) < nused
    e = jnp.minimum(jnp.sum(blk_start[:, None] >= pend[None, :], axis=1), N_EXPERTS - 1)
    is_e = e[:, None] == jnp.arange(N_EXPERTS)[None, :]
    of_block = lambda table: jnp.sum(jnp.where(is_e, table[None, :], 0), axis=1)
    r0 = blk_start - of_block(pstart)
    q0s = jnp.where(used, of_block(start) + r0, 0)
    nvalids = jnp.where(used, jnp.clip(of_block(counts) - r0, 0, m), 0)
    last_exp = jnp.sum(jnp.where(jnp.arange(nblk) == nused - 1, e, 0))
    blk_exp = jnp.where(used, e, last_exp)
    as_i32 = lambda t: t.astype(I32)
    return order, (as_i32(blk_exp), as_i32(q0s), as_i32(nvalids), as_i32(nused).reshape(1))


def moe_layer(h1, xn, route, gate, counts, w_gate, w_up, w_down, layer, norm_g, *, final_norm):
    order, plan = dispatch_plan(route, counts)
    ys = moe_experts(order, plan, xn, w_gate, w_up, w_down, layer)
    return moe_finish(ys, gate, h1, norm_g, final_norm=final_norm)


def kernel(x, attn_norm_g, ffn_norm_g, w_in_ab, w_out_ab, ret_gn_g, w_in_c, b_in_c, sinks, w_out_c, b_out_c,
           w_group, w_expert_router, w_gate, w_up, w_down, final_norm_g):
    batch, seq, d = x.shape
    n = batch * seq
    depth = attn_norm_g.shape[0]
    cos, sin = rope_lane_tables(seq)
    later = (jnp.arange(min(SB_TILE, seq))[:, None] > jnp.arange(min(SB_TILE, seq))[None, :]).astype(BF16)
    w_sb = N_HEADS_SB * HEAD_DIM
    w_ret = N_HEADS_RET * HEAD_DIM
    h = x.reshape(n, d)
    for layer in range(depth):
        i = layer // 2
        last = layer == depth - 1
        if layer % 2 == 0:
            proj = norm_proj(h, attn_norm_g[layer], w_in_ab[i].astype(BF16), jnp.zeros((w_in_ab.shape[2],), F32),
                             cos, sin, seq=seq, tn=w_sb,
                             col_ops=("scale", "", "", "rope", "rope scale", "", ""))
            per = w_sb // LANES
            a = sb_attention(proj, later, batch=batch, seq=seq, q_blk=0, k_blk=per, v_blk=2 * per)
            r = retention(proj, ret_gn_g[i], batch=batch, seq=seq,
                          q_blk=3 * per, k_blk=4 * per, v_blk=5 * per, g_blk=6 * per)
            w_out = w_out_ab[i].astype(BF16)
            lhs, ws = [a, r], [w_out[:w_sb], w_out[w_sb:]]
            bias = jnp.zeros((d,), F32)
        else:
            wkv = N_KV_SWA * HEAD_DIM
            nq = N_HEADS_SWA * HEAD_DIM // wkv
            proj = norm_proj(h, attn_norm_g[layer], w_in_c[i].astype(BF16), b_in_c[i].astype(F32),
                             cos, sin, seq=seq, tn=wkv,
                             col_ops=("rope scale",) * nq + ("rope", ""))
            o = swa_attention(proj, sinks[i], batch=batch, seq=seq)
            lhs, ws = [o], [w_out_c[i].astype(BF16)]
            bias = b_out_c[i].astype(F32)
        h1, xn, route, gate, counts = outproj_router(
            lhs, ws, bias, h, ffn_norm_g[layer], router_weights(w_group[layer], w_expert_router[layer]))
        h = moe_layer(h1, xn, route, gate, counts, w_gate, w_up, w_down, layer, final_norm_g, final_norm=last)
    return h.reshape(batch, seq, d)
```

```python
import functools

import jax
import jax.numpy as jnp
import numpy as np
from jax import lax
from jax.experimental import pallas as pl
from jax.experimental.pallas import tpu as pltpu

F32 = jnp.float32
BF16 = jnp.bfloat16
I32 = jnp.int32

HEAD_DIM = 64
N_HEADS_SB = 8
N_HEADS_RET = 8
N_HEADS_SWA = 16
N_KV_SWA = 4
WINDOW = 128
ROPE_THETA = 10000.0
N_GROUPS = 4
EXPERTS_PER_GROUP = 8
N_EXPERTS = N_GROUPS * EXPERTS_PER_GROUP
MOE_BLOCK = 256
EPS = 1e-6

LANES = 128
HALF = HEAD_DIM // 2
QK_SCALE = HEAD_DIM ** -0.5

SB_TILE = 256
RET_CHUNK = 256
PROJ_TM = 512
TOK_TM = 512
ROUTER_TM = 1024
ROUTER_SUB = 256
VMEM_LIMIT = 48 * 1024 * 1024

_NT = (((1,), (1,)), ((), ()))
_TN = (((0,), (0,)), ((), ()))


def _cparams(*sem):
    return pltpu.CompilerParams(dimension_semantics=sem, vmem_limit_bytes=VMEM_LIMIT)


def _head0_mask():
    return lax.broadcasted_iota(I32, (1, LANES), 1) < HEAD_DIM


def _norm_proj_kernel(x_ref, g_ref, w_ref, b_ref, cos_ref, sin_ref, o_ref, *, tn, col_ops):
    x = x_ref[...]
    ms = jnp.mean(x * x, axis=-1, keepdims=True)
    xn = (x * lax.rsqrt(ms + EPS) * g_ref[...]).astype(BF16)
    lane = lax.broadcasted_iota(I32, (1, LANES), 1)
    first_half = (lane % HEAD_DIM) < HALF
    for j, op in enumerate(col_ops):
        cols = slice(j * tn, (j + 1) * tn)
        acc = jnp.dot(xn, w_ref[:, cols], preferred_element_type=F32) + b_ref[:, cols]
        if "rope" in op:
            cos = cos_ref[...]
            sin = sin_ref[...]
            slabs = []
            for s in range(tn // LANES):
                a = acc[:, s * LANES:(s + 1) * LANES]
                partner = jnp.where(first_half, pltpu.roll(a, LANES - HALF, 1), pltpu.roll(a, HALF, 1))
                slabs.append(a * cos + partner * sin)
            acc = jnp.concatenate(slabs, axis=1)
        if "scale" in op:
            acc = acc * QK_SCALE
        o_ref[:, cols] = acc.astype(BF16)


def norm_proj(x, g, w, b, cos, sin, *, seq, tn, col_ops):
    n, d = x.shape
    f = w.shape[1]
    tm = min(PROJ_TM, seq)
    assert n % tm == 0 and seq % tm == 0 and f == tn * len(col_ops)
    pos_blocks = seq // tm
    return pl.pallas_call(
        functools.partial(_norm_proj_kernel, tn=tn, col_ops=col_ops),
        out_shape=jax.ShapeDtypeStruct((n, f), BF16),
        grid=(n // tm,),
        in_specs=[
            pl.BlockSpec((tm, d), lambda i: (i, 0)),
            pl.BlockSpec((1, d), lambda i: (0, 0)),
            pl.BlockSpec((d, f), lambda i: (0, 0)),
            pl.BlockSpec((1, f), lambda i: (0, 0)),
            pl.BlockSpec((tm, LANES), lambda i: (i % pos_blocks, 0)),
            pl.BlockSpec((tm, LANES), lambda i: (i % pos_blocks, 0)),
        ],
        out_specs=pl.BlockSpec((tm, f), lambda i: (i, 0)),
        compiler_params=_cparams("arbitrary"),
        name="norm_proj",
    )(x, g.reshape(1, d), w, b.reshape(1, f), cos, sin)


def rope_lane_tables(seq):
    pos = jnp.arange(seq, dtype=F32)
    inv = ROPE_THETA ** (-jnp.arange(0, HEAD_DIM, 2, dtype=F32) / HEAD_DIM)
    ang = pos[:, None] * inv[None, :]
    cos, sin = jnp.cos(ang), jnp.sin(ang)
    return jnp.tile(cos, (1, 4)), jnp.tile(jnp.concatenate([-sin, sin], axis=1), (1, 2))


SB_PAIRS = 4


def _sb_kernel(q_ref, k_ref, v_ref, t_ref, o_ref, acc_ref, carry_ref, *, tile):
    qi = pl.program_id(2)
    m0 = _head0_mask()
    zero = jnp.zeros((tile, LANES), BF16)
    row = lax.broadcasted_iota(I32, (tile, tile), 0)
    col = lax.broadcasted_iota(I32, (tile, tile), 1)
    strict = col < row

    def key_tile(kb, diag):
        start = pl.multiple_of(kb * tile, tile)
        for p in range(SB_PAIRS):
            lanes = slice(p * LANES, (p + 1) * LANES)
            q = q_ref[:, lanes]
            k = k_ref[pl.ds(start, tile), lanes]
            v = v_ref[pl.ds(start, tile), lanes]
            out = None
            for h in range(2):
                qh = jnp.where(m0, q, zero) if h == 0 else jnp.where(m0, zero, q)
                vh = jnp.where(m0, v, zero) if h == 0 else jnp.where(m0, zero, v)
                z = lax.dot_general(qh, k, _NT, preferred_element_type=F32)
                log_fail = -(jnp.maximum(z, 0.0) + jnp.log(1.0 + jnp.exp(-jnp.abs(z))))
                if diag:
                    log_fail = jnp.where(strict, log_fail, 0.0)
                after = jnp.dot(log_fail.astype(BF16), t_ref[...], preferred_element_type=F32)
                logw = z + log_fail + after
                if not diag:
                    logw = logw + carry_ref[2 * p + h]
                w = jnp.exp(logw)
                if diag:
                    w = jnp.where(strict, w, 0.0)
                pv = jnp.dot(w.astype(BF16), vh, preferred_element_type=F32)
                out = pv if out is None else out + pv
                tile_sum = jnp.sum(log_fail, axis=-1, keepdims=True)
                if diag:
                    carry_ref[2 * p + h] = tile_sum
                else:
                    carry_ref[2 * p + h] += tile_sum
            if diag:
                acc_ref[:, lanes] = out
            else:
                acc_ref[:, lanes] += out

    key_tile(qi, True)

    def body(j, c):
        key_tile(qi - 1 - j, False)
        return c

    lax.fori_loop(0, qi, body, 0)
    o_ref[...] = acc_ref[...].astype(BF16)


def sb_attention(proj, later_mat, *, batch, seq, q_blk, k_blk, v_blk):
    n = proj.shape[0]
    tile = min(SB_TILE, seq)
    nq = seq // tile
    width = SB_PAIRS * LANES
    steps = N_HEADS_SB * HEAD_DIM // width
    assert q_blk % SB_PAIRS == 0 and k_blk % SB_PAIRS == 0 and v_blk % SB_PAIRS == 0
    col = lambda blk: blk // SB_PAIRS
    return pl.pallas_call(
        functools.partial(_sb_kernel, tile=tile),
        out_shape=jax.ShapeDtypeStruct((n, steps * width), BF16),
        grid=(batch, steps, nq),
        in_specs=[
            pl.BlockSpec((tile, width), lambda b, p, i: (b * nq + i, col(q_blk) + p)),
            pl.BlockSpec((seq, width), lambda b, p, i: (b, col(k_blk) + p)),
            pl.BlockSpec((seq, width), lambda b, p, i: (b, col(v_blk) + p)),
            pl.BlockSpec((tile, tile), lambda b, p, i: (0, 0)),
        ],
        out_specs=pl.BlockSpec((tile, width), lambda b, p, i: (b * nq + i, p)),
        scratch_shapes=[pltpu.VMEM((tile, width), F32), pltpu.VMEM((2 * SB_PAIRS, tile, 1), F32)],
        compiler_params=_cparams("arbitrary", "arbitrary", "arbitrary"),
        name="sb_attention",
    )(proj, proj, proj, later_mat)


RET_PAIRS = 4


def _retention_kernel(q_ref, k_ref, v_ref, gate_ref, gain_ref, dec_ref, xi_ref, zeta_ref, gch_ref,
                      o_ref, state_ref):
    c = pl.program_id(2)

    @pl.when(c == 0)
    def _():
        state_ref[...] = jnp.zeros_like(state_ref)

    m0 = _head0_mask()
    zero = jnp.zeros((q_ref.shape[0], LANES), BF16)
    r = lax.broadcasted_iota(I32, (LANES, LANES), 0) < HEAD_DIM
    cc = lax.broadcasted_iota(I32, (LANES, LANES), 1) < HEAD_DIM
    same_head = r == cc

    def head_mean(t):
        s0 = jnp.sum(jnp.where(m0, t, 0.0), axis=-1, keepdims=True)
        s1 = jnp.sum(jnp.where(m0, 0.0, t), axis=-1, keepdims=True)
        return jnp.where(m0, s0, s1) * (1.0 / HEAD_DIM)

    for p in range(RET_PAIRS):
        lanes = slice(p * LANES, (p + 1) * LANES)
        q = q_ref[:, lanes]
        k = k_ref[:, lanes]
        v = v_ref[:, lanes]
        state = state_ref[p]
        y = jnp.dot(q, state.astype(BF16), preferred_element_type=F32) * xi_ref[:, lanes]
        for h in range(2):
            qh = jnp.where(m0, q, zero) if h == 0 else jnp.where(m0, zero, q)
            vh = jnp.where(m0, v, zero) if h == 0 else jnp.where(m0, zero, v)
            s = lax.dot_general(qh, k, _NT, preferred_element_type=F32)
            inner = (s * dec_ref[2 * p + h]).astype(BF16)
            y = y + jnp.dot(inner, vh, preferred_element_type=F32)

        kz = (k.astype(F32) * zeta_ref[:, lanes]).astype(BF16)
        upd = lax.dot_general(kz, v, _TN, preferred_element_type=F32)
        state_ref[p] = state * gch_ref[:, lanes] + jnp.where(same_head, upd, 0.0)

        d = y - head_mean(y)
        yn = d * lax.rsqrt(head_mean(d * d) + EPS) * gain_ref[:, lanes]
        g = gate_ref[:, lanes].astype(F32)
        o_ref[:, lanes] = (yn * (g * (1.0 / (1.0 + jnp.exp(-g))))).astype(BF16)


def retention_tables(chunk):
    h = N_HEADS_RET
    log_g = jnp.log(1.0 - 2.0 ** (-5.0 - jnp.arange(h, dtype=F32)))
    idx = jnp.arange(chunk, dtype=F32)
    diff = idx[:, None] - idx[None, :]
    dec = jnp.where(diff[None] >= 0, jnp.exp(log_g[:, None, None] * jnp.maximum(diff, 0.0)[None]), 0.0)
    xi = jnp.exp(log_g[:, None] * (idx[None, :] + 1.0))
    zeta = jnp.exp(log_g[:, None] * (chunk - 1.0 - idx[None, :]))
    gch = jnp.exp(log_g * chunk)
    lanes = lambda t: jnp.repeat(t.T, HEAD_DIM, axis=1)
    return dec, lanes(xi), lanes(zeta), jnp.repeat(gch, HEAD_DIM)[None, :]


def retention(proj, gain, *, batch, seq, q_blk, k_blk, v_blk, g_blk):
    n = proj.shape[0]
    chunk = min(RET_CHUNK, seq)
    nc = seq // chunk
    width = RET_PAIRS * LANES
    steps = N_HEADS_RET * HEAD_DIM // width
    assert all(b % RET_PAIRS == 0 for b in (q_blk, k_blk, v_blk, g_blk))
    dec, xi, zeta, gch = retention_tables(chunk)
    blk = lambda off: pl.BlockSpec((chunk, width), lambda b, p, c: (b * nc + c, off // RET_PAIRS + p))
    return pl.pallas_call(
        _retention_kernel,
        out_shape=jax.ShapeDtypeStruct((n, steps * width), BF16),
        grid=(batch, steps, nc),
        in_specs=[
            blk(q_blk), blk(k_blk), blk(v_blk), blk(g_blk),
            pl.BlockSpec((1, width), lambda b, p, c: (0, p)),
            pl.BlockSpec((2 * RET_PAIRS, chunk, chunk), lambda b, p, c: (p, 0, 0)),
            pl.BlockSpec((chunk, width), lambda b, p, c: (0, p)),
            pl.BlockSpec((chunk, width), lambda b, p, c: (0, p)),
            pl.BlockSpec((1, width), lambda b, p, c: (0, p)),
        ],
        out_specs=pl.BlockSpec((chunk, width), lambda b, p, c: (b * nc + c, p)),
        scratch_shapes=[pltpu.VMEM((RET_PAIRS, LANES, LANES), F32)],
        compiler_params=_cparams("arbitrary", "arbitrary", "arbitrary"),
        name="retention",
    )(proj, proj, proj, proj, gain.reshape(1, -1).astype(F32), dec, xi, zeta, gch)


def _swa_kernel(sink_ref, q_ref, kp_ref, kc_ref, vp_ref, vc_ref, o_ref):
    blk = pl.program_id(1)
    m0 = _head0_mask()
    w = WINDOW
    qi = lax.broadcasted_iota(I32, (w, 2 * w), 0)
    kj = lax.broadcasted_iota(I32, (w, 2 * w), 1)
    rel = qi + w - kj
    valid = (rel >= 0) & (rel < w) & ((blk > 0) | (kj >= w))
    kband = jnp.concatenate([kp_ref[...], kc_ref[...]], axis=0)
    vband = jnp.concatenate([vp_ref[...], vc_ref[...]], axis=0)
    group = N_HEADS_SWA // N_KV_SWA
    slabs = group * HEAD_DIM // LANES
    swap = lambda t: jnp.concatenate([t[:, HEAD_DIM:], t[:, :HEAD_DIM]], axis=1)
    zero_kv = jnp.zeros((2 * w, LANES), BF16)
    zero_q = jnp.zeros((slabs * w, LANES), BF16)
    ones_kv = jnp.ones((2 * w, LANES), BF16)

    def half_heads(q_stack, k_half, v_half, first_head):
        s = lax.dot_general(q_stack, k_half, _NT, preferred_element_type=F32)
        probs, sink_terms = [], []
        for slab in range(slabs):
            sink = sink_ref[first_head + 2 * slab]
            sh = jnp.where(valid, s[slab * w:(slab + 1) * w], -jnp.inf)
            m = jnp.maximum(jnp.max(sh, axis=-1, keepdims=True), sink)
            probs.append(jnp.exp(sh - m).astype(BF16))
            sink_terms.append(jnp.exp(sink - jnp.broadcast_to(m, (w, LANES))))
        p = jnp.concatenate(probs, axis=0)
        pv = jnp.dot(p, jnp.concatenate([v_half, ones_kv], axis=1), preferred_element_type=F32)
        denom = pv[:, LANES:] + jnp.concatenate(sink_terms, axis=0)
        return pv[:, :LANES] / denom

    for pair in range(N_KV_SWA * HEAD_DIM // LANES):
        k2 = kband[:, pair * LANES:(pair + 1) * LANES]
        v2 = vband[:, pair * LANES:(pair + 1) * LANES]
        k2s, v2s = swap(k2), swap(v2)
        for c in range(2):
            kvh = 2 * pair + c
            k_even, k_odd = (k2, k2s) if c == 0 else (k2s, k2)
            v_even, v_odd = (v2, v2s) if c == 0 else (v2s, v2)
            base = kvh * group * HEAD_DIM
            q_stack = jnp.concatenate([q_ref[:, base + t * LANES:base + (t + 1) * LANES] for t in range(slabs)],
                                      axis=0)
            out = (half_heads(jnp.where(m0, q_stack, zero_q), k_even, jnp.where(m0, v_even, zero_kv), kvh * group)
                   + half_heads(jnp.where(m0, zero_q, q_stack), k_odd, jnp.where(m0, zero_kv, v_odd),
                                kvh * group + 1))
            for t in range(slabs):
                o_ref[:, base + t * LANES:base + (t + 1) * LANES] = out[t * w:(t + 1) * w].astype(BF16)


def swa_attention(proj, sinks, *, batch, seq):
    n = proj.shape[0]
    w = WINDOW
    nb = seq // w
    wq = N_HEADS_SWA * HEAD_DIM
    wkv = N_KV_SWA * HEAD_DIM
    k_blk = wq // wkv
    cur = lambda off: pl.BlockSpec((w, wkv), lambda b, i: (b * nb + i, off))
    prev = lambda off: pl.BlockSpec((w, wkv), lambda b, i: (b * nb + jnp.maximum(i - 1, 0), off))
    return pl.pallas_call(
        _swa_kernel,
        out_shape=jax.ShapeDtypeStruct((n, wq), BF16),
        grid=(batch, nb),
        in_specs=[
            pl.BlockSpec(memory_space=pltpu.SMEM),
            pl.BlockSpec((w, wq), lambda b, i: (b * nb + i, 0)),
            prev(k_blk), cur(k_blk), prev(k_blk + 1), cur(k_blk + 1),
        ],
        out_specs=pl.BlockSpec((w, wq), lambda b, i: (b * nb + i, 0)),
        compiler_params=_cparams("arbitrary", "arbitrary"),
        name="swa_attention",
    )(sinks.astype(F32), proj, proj, proj, proj, proj)


ROUTE_EID = 0
GROUP_LANE = N_EXPERTS


def _split_bf16(t):
    hi = t.astype(BF16)
    return hi, (t - hi.astype(F32)).astype(BF16)


def _outproj_router_kernel(*refs, n_lhs):
    lhs = refs[:n_lhs]
    ws = refs[n_lhs:2 * n_lhs]
    b_ref, h_ref, g_ref, wr_ref = refs[2 * n_lhs:2 * n_lhs + 4]
    h1_ref, xn_ref, route_ref, gate_ref, cnt_ref, carry_ref = refs[2 * n_lhs + 4:]
    step = pl.program_id(0)

    @pl.when(step == 0)
    def _():
        carry_ref[...] = jnp.zeros_like(carry_ref)

    w_split = jnp.concatenate(_split_bf16(wr_ref[...]), axis=1)
    for sub in range(h_ref.shape[0] // ROUTER_SUB):
        rows = slice(sub * ROUTER_SUB, (sub + 1) * ROUTER_SUB)
        mix = b_ref[...]
        for a_ref, w_ref in zip(lhs, ws):
            mix = mix + jnp.dot(a_ref[rows], w_ref[...], preferred_element_type=F32)
        h1 = h_ref[rows] + mix
        h1_ref[rows] = h1
        ms = jnp.mean(h1 * h1, axis=-1, keepdims=True)
        xn = h1 * lax.rsqrt(ms + EPS) * g_ref[...]
        _tile_store(xn_ref, sub * ROUTER_SUB * SUBLANES, ROUTER_SUB, xn)
        x_hi, x_lo = _split_bf16(xn)
        parts = (jnp.dot(x_hi, w_split, preferred_element_type=F32)
                 + jnp.dot(x_lo, w_split, preferred_element_type=F32))
        logits = parts[:, :LANES] + parts[:, LANES:]
        route, gate, cnt = _route(logits)
        route_ref[rows] = route
        gate_ref[rows] = gate
        carry_ref[...] += cnt
    cnt_ref[...] = jnp.broadcast_to(carry_ref[...], cnt_ref.shape).astype(I32)


def _route(logits):
    tm = logits.shape[0]
    lane = lax.broadcasted_iota(I32, (tm, LANES), 1)
    lane_f = lane.astype(F32)
    neg = -jnp.inf

    def lane_max(t):
        return jnp.max(t, axis=-1, keepdims=True)

    def lane_sum(t):
        return jnp.sum(t, axis=-1, keepdims=True)

    def first_lane_of(t, value, mask):
        return jnp.min(jnp.where(mask & (t == value), lane_f, float(LANES)), axis=-1, keepdims=True)

    is_group = (lane >= GROUP_LANE) & (lane < GROUP_LANE + N_GROUPS)
    gl = jnp.where(is_group, logits, neg)
    g_max = lane_max(gl)
    g_prob = 1.0 / lane_sum(jnp.exp(gl - g_max))
    g_idx = first_lane_of(gl, g_max, is_group) - float(GROUP_LANE)
    group_of_lane = lax.shift_right_logical(lane, int(np.log2(EXPERTS_PER_GROUP))).astype(F32)
    in_group = (lane < N_EXPERTS) & (group_of_lane == g_idx)
    el = jnp.where(in_group, logits, neg)
    l1 = lane_max(el)
    i1 = first_lane_of(el, l1, in_group)
    rest = in_group & (lane_f != i1)
    el2 = jnp.where(rest, el, neg)
    l2 = lane_max(el2)
    i2 = first_lane_of(el2, l2, rest)
    e2 = jnp.exp(l2 - l1)
    gate1 = g_prob / (1.0 + e2)
    gate2 = g_prob * (e2 / (1.0 + e2))

    cnt = jnp.where((lane_f == i1) | (lane_f == i2), 1.0, 0.0)
    route = jnp.where(lane == ROUTE_EID, i1, 0.0)
    route = jnp.where(lane == ROUTE_EID + 1, i2, route)
    gate = jnp.where(lane == 0, gate1, jnp.where(lane == 1, gate2, 0.0))
    return route.astype(I32), gate, jnp.sum(cnt, axis=0, keepdims=True)


def outproj_router(lhs, ws, bias, h, g, w_router):
    n, d = h.shape
    tm = ROUTER_TM
    assert n % tm == 0
    row_blk = lambda width: pl.BlockSpec((tm, width), lambda i: (i, 0))
    full = lambda a: pl.BlockSpec(a.shape, lambda i: (0, 0))
    bias2, g2 = bias.reshape(1, d), g.reshape(1, d)
    args = [*lhs, *ws, bias2, h, g2, w_router]
    in_specs = ([row_blk(a.shape[1]) for a in lhs] + [full(w) for w in ws]
                + [full(bias2), row_blk(d), full(g2), full(w_router)])
    return pl.pallas_call(
        functools.partial(_outproj_router_kernel, n_lhs=len(lhs)),
        out_shape=(
            jax.ShapeDtypeStruct((n, d), F32), jax.ShapeDtypeStruct((n * d // LANES, LANES), F32),
            jax.ShapeDtypeStruct((n, LANES), I32), jax.ShapeDtypeStruct((n, LANES), F32),
            jax.ShapeDtypeStruct((8, LANES), I32),
        ),
        grid=(n // tm,),
        in_specs=in_specs,
        out_specs=(row_blk(d), pl.BlockSpec((tm * d // LANES, LANES), lambda i: (i, 0)),
                   row_blk(LANES), row_blk(LANES), pl.BlockSpec((8, LANES), lambda i: (0, 0))),
        scratch_shapes=[pltpu.VMEM((1, LANES), F32)],
        compiler_params=_cparams("arbitrary"),
        name="outproj_router",
    )(*args)


def router_weights(w_group, w_expert_router):
    d = w_group.shape[0]
    pad = jnp.zeros((d, LANES - N_EXPERTS - N_GROUPS), F32)
    return jnp.concatenate([w_expert_router.astype(F32), w_group.astype(F32), pad], axis=1)


IDX_CHUNK = 1024
IDX_RING = 4
OUT_RING = 4
DMA_UNROLL = 8


SUBLANES = 8


def _tile_store(ref, first_row, n_tokens, value):
    for c in range(value.shape[1] // LANES):
        ref[pl.ds(first_row + c, n_tokens, stride=SUBLANES), :] = value[:, c * LANES:(c + 1) * LANES]


def _tile_load(ref, first_row, n_tokens, width, pitch=SUBLANES):
    return jnp.concatenate([ref[pl.ds(first_row + c, n_tokens, stride=pitch), :]
                            for c in range(width // LANES)], axis=1)


def _token_rows(ref, token):
    start = token * SUBLANES
    if not isinstance(start, int):
        start = pl.multiple_of(start, SUBLANES)
    return ref.at[pl.ds(start, SUBLANES), :]


def _token_copy(src, src_token, dst, dst_token, sem):
    return pltpu.make_async_copy(_token_rows(src, src_token), _token_rows(dst, dst_token), sem)


def _moe_kernel(blk_exp, q0s, nvalids, nused, order_hbm, xn_hbm, wg_ref, wu_ref, wd_ref, out_hbm,
                ibuf, xbuf, ybuf, wg_bf, wu_bf, wd_bf, isem, gsem, ssem):
    i = pl.program_id(0)
    nu = nused[0]
    m = MOE_BLOCK
    d = wd_bf.shape[1]

    window = 2 * IDX_CHUNK

    def idx_copy(blk):
        base = pl.multiple_of(q0s[blk] & ~(IDX_CHUNK - 1), IDX_CHUNK)
        slot = blk & (IDX_RING - 1)
        dst = ibuf.at[pl.ds(pl.multiple_of(slot * window, window), window)]
        return pltpu.make_async_copy(order_hbm.at[pl.ds(base, window)], dst, isem.at[slot])

    def pair_base(blk):
        return (blk & (IDX_RING - 1)) * window + (q0s[blk] & (IDX_CHUNK - 1))

    def gather_group(blk_base, dst, sem, g):
        for u in range(DMA_UNROLL):
            r = g * DMA_UNROLL + u
            pair = ibuf[blk_base + r]
            first_row = pl.multiple_of((pair & ~1) * (SUBLANES // 2), SUBLANES)
            pltpu.make_async_copy(xn_hbm.at[pl.ds(first_row, SUBLANES), :], _token_rows(dst, r), sem).start()

    def scatter_group(blk_base, src, sem, g):
        for u in range(DMA_UNROLL):
            r = g * DMA_UNROLL + u
            _token_copy(src, r, out_hbm, ibuf[blk_base + r], sem).start()

    def issue_gathers(blk, unrolled=False):
        slot = blk & 1
        args = (pair_base(blk), xbuf.at[slot], gsem.at[slot])
        if unrolled:
            for g in range(m // DMA_UNROLL):
                gather_group(*args, g)
        else:
            lax.fori_loop(0, m // DMA_UNROLL, lambda g, c: (gather_group(*args, g), c)[1], 0)

    def wait_gathers(blk):
        slot = blk & 1
        pltpu.make_async_copy(xn_hbm.at[pl.ds(0, m * SUBLANES), :], xbuf.at[slot], gsem.at[slot]).wait()

    def issue_scatters(blk, unrolled=False):
        slot = blk & (OUT_RING - 1)
        base = pair_base(blk)
        src = ybuf.at[slot]
        sem = ssem.at[slot]
        if unrolled:
            for g in range(m // DMA_UNROLL):
                scatter_group(base, src, sem, g)
            return
        nv = nvalids[blk]
        groups = lax.shift_right_logical(nv, DMA_UNROLL.bit_length() - 1)

        def tail(r, c):
            _token_copy(src, r, out_hbm, ibuf[base + r], sem).start()
            return c

        lax.fori_loop(0, groups, lambda g, c: (scatter_group(base, src, sem, g), c)[1], 0)
        lax.fori_loop(groups * DMA_UNROLL, nv, tail, 0)

    def wait_scatters(blk):
        slot = blk & (OUT_RING - 1)
        nv = nvalids[blk]
        rows = pl.multiple_of(nv * SUBLANES, SUBLANES)

        @pl.when(nv > 0)
        def _():
            pltpu.make_async_copy(ybuf.at[slot].at[pl.ds(0, rows), :], out_hbm.at[pl.ds(0, rows), :],
                                  ssem.at[slot]).wait()

    @pl.when(i == 0)
    def _():
        first = idx_copy(0)
        first.start()
        first.wait()
        issue_gathers(0)

        @pl.when(nu > 1)
        def _():
            idx_copy(1).start()

    def expert():
        x = _tile_load(xbuf.at[i & 1], 0, m, d).astype(BF16)
        gate = jnp.dot(x, wg_bf[...], preferred_element_type=F32)
        up = jnp.dot(x, wu_bf[...], preferred_element_type=F32)
        hidden = (gate * (1.0 / (1.0 + jnp.exp(-gate))) * up).astype(BF16)
        _tile_store(ybuf.at[i & (OUT_RING - 1)], 0, m, jnp.dot(hidden, wd_bf[...], preferred_element_type=F32))

    @pl.when(i < nu)
    def _():
        @pl.when(i + 2 < nu)
        def _():
            idx_copy(i + 2).start()

        @pl.when(i + 1 < nu)
        def _():
            idx_copy(i + 1).wait()

        wait_gathers(i)

        @pl.when((i == 0) | (blk_exp[i] != blk_exp[jnp.maximum(i - 1, 0)]))
        def _():
            wg_bf[...] = wg_ref[0, 0].astype(BF16)
            wu_bf[...] = wu_ref[0, 0].astype(BF16)
            wd_bf[...] = wd_ref[0, 0].astype(BF16)

        @pl.when(i >= OUT_RING - 1)
        def _():
            wait_scatters(i - (OUT_RING - 1))

        prev = jnp.maximum(i - 1, 0)
        steady = (i >= 1) & (i + 1 < nu) & (nvalids[prev] == m)

        @pl.when(steady)
        def _():
            issue_gathers(i + 1, unrolled=True)
            issue_scatters(i - 1, unrolled=True)
            expert()

        @pl.when(jnp.logical_not(steady))
        def _():
            @pl.when(i + 1 < nu)
            def _():
                issue_gathers(i + 1)

            @pl.when(i >= 1)
            def _():
                issue_scatters(i - 1)

            expert()

        @pl.when(i == nu - 1)
        def _():
            issue_scatters(i)
            for back in range(OUT_RING - 2, -1, -1):
                @pl.when(i >= back)
                def _():
                    wait_scatters(i - back)


def moe_experts(order, plan, xn, w_gate, w_up, w_down, layer):
    blk_exp, q0s, nvalids, nused = plan
    d, de = w_gate.shape[2], w_gate.shape[3]
    assert d == SUBLANES * LANES
    rows = xn.shape[0]
    m = MOE_BLOCK
    w_in_spec = pl.BlockSpec((1, 1, d, de), lambda i, be, q0, nv, nu: (layer, be[i], 0, 0))
    return pl.pallas_call(
        _moe_kernel,
        out_shape=jax.ShapeDtypeStruct((2 * rows, LANES), F32),
        grid_spec=pltpu.PrefetchScalarGridSpec(
            num_scalar_prefetch=4,
            grid=(blk_exp.shape[0],),
            in_specs=[
                pl.BlockSpec(memory_space=pl.ANY), pl.BlockSpec(memory_space=pl.ANY),
                w_in_spec, w_in_spec,
                pl.BlockSpec((1, 1, de, d), lambda i, be, q0, nv, nu: (layer, be[i], 0, 0)),
            ],
            out_specs=pl.BlockSpec(memory_space=pl.ANY),
            scratch_shapes=[
                pltpu.SMEM((IDX_RING * 2 * IDX_CHUNK,), I32),
                pltpu.VMEM((2, m * SUBLANES, LANES), F32), pltpu.VMEM((OUT_RING, m * SUBLANES, LANES), F32),
                pltpu.VMEM((d, de), BF16), pltpu.VMEM((d, de), BF16), pltpu.VMEM((de, d), BF16),
                pltpu.SemaphoreType.DMA((IDX_RING,)), pltpu.SemaphoreType.DMA((2,)),
                pltpu.SemaphoreType.DMA((OUT_RING,)),
            ],
        ),
        compiler_params=_cparams("arbitrary"),
        name="moe_experts",
    )(blk_exp, q0s, nvalids, nused, order, xn, w_gate, w_up, w_down)


def _moe_finish_kernel(y_ref, gate_ref, h_ref, g_ref, o_ref, *, final_norm):
    gate = gate_ref[...]
    tm, d = h_ref.shape
    y0 = _tile_load(y_ref, 0, tm, d, pitch=2 * SUBLANES)
    y1 = _tile_load(y_ref, SUBLANES, tm, d, pitch=2 * SUBLANES)
    out = h_ref[...] + (y0 * gate[:, 0:1] + y1 * gate[:, 1:2])
    if final_norm:
        ms = jnp.mean(out * out, axis=-1, keepdims=True)
        out = out * lax.rsqrt(ms + EPS) * g_ref[...]
    o_ref[...] = out


def moe_finish(ys, gate, h, g, *, final_norm):
    n, d = h.shape
    tm = TOK_TM
    row_blk = lambda width: pl.BlockSpec((tm, width), lambda i: (i, 0))
    return pl.pallas_call(
        functools.partial(_moe_finish_kernel, final_norm=final_norm),
        out_shape=jax.ShapeDtypeStruct((n, d), F32),
        grid=(n // tm,),
        in_specs=[pl.BlockSpec((2 * tm * SUBLANES, LANES), lambda i: (i, 0)),
                  row_blk(LANES), row_blk(d), pl.BlockSpec((1, d), lambda i: (0, 0))],
        out_specs=row_blk(d),
        compiler_params=_cparams("arbitrary"),
        name="moe_finish",
    )(ys, gate, h, g.reshape(1, d))


def dispatch_plan(route, counts):
    m = MOE_BLOCK
    n = route.shape[0]
    assert (2 * n) % IDX_CHUNK == 0
    counts = counts[0, :N_EXPERTS]
    padded = ((counts + m - 1) // m) * m
    pend = jnp.cumsum(padded)
    pstart = pend - padded
    start = jnp.cumsum(counts) - counts
    eid = route[:, ROUTE_EID:ROUTE_EID + 2].reshape(-1)
    order = jnp.argsort(eid, stable=True).astype(I32)
    order = jnp.concatenate([order, jnp.zeros((2 * IDX_CHUNK,), I32)])
    nblk = (2 * n + N_EXPERTS * m) // m
    blk_start = jnp.arange(nblk, dtype=I32) * m
    nused = pend[-1] // m
    used = jnp.arange(nblk) < nused
    e = jnp.minimum(jnp.sum(blk_start[:, None] >= pend[None, :], axis=1), N_EXPERTS - 1)
    is_e = e[:, None] == jnp.arange(N_EXPERTS)[None, :]
    of_block = lambda table: jnp.sum(jnp.where(is_e, table[None, :], 0), axis=1)
    r0 = blk_start - of_block(pstart)
    q0s = jnp.where(used, of_block(start) + r0, 0)
    nvalids = jnp.where(used, jnp.clip(of_block(counts) - r0, 0, m), 0)
    last_exp = jnp.sum(jnp.where(jnp.arange(nblk) == nused - 1, e, 0))
    blk_exp = jnp.where(used, e, last_exp)
    as_i32 = lambda t: t.astype(I32)
    return order, (as_i32(blk_exp), as_i32(q0s), as_i32(nvalids), as_i32(nused).reshape(1))


def moe_layer(h1, xn, route, gate, counts, w_gate, w_up, w_down, layer, norm_g, *, final_norm):
    order, plan = dispatch_plan(route, counts)
    ys = moe_experts(order, plan, xn, w_gate, w_up, w_down, layer)
    return moe_finish(ys, gate, h1, norm_g, final_norm=final_norm)


def kernel(x, attn_norm_g, ffn_norm_g, w_in_ab, w_out_ab, ret_gn_g, w_in_c, b_in_c, sinks, w_out_c, b_out_c,
           w_group, w_expert_router, w_gate, w_up, w_down, final_norm_g):
    batch, seq, d = x.shape
    n = batch * seq
    depth = attn_norm_g.shape[0]
    cos, sin = rope_lane_tables(seq)
    later = (jnp.arange(min(SB_TILE, seq))[:, None] > jnp.arange(min(SB_TILE, seq))[None, :]).astype(BF16)
    w_sb = N_HEADS_SB * HEAD_DIM
    w_ret = N_HEADS_RET * HEAD_DIM
    h = x.reshape(n, d)
    for layer in range(depth):
        i = layer // 2
        last = layer == depth - 1
        if layer % 2 == 0:
            proj = norm_proj(h, attn_norm_g[layer], w_in_ab[i].astype(BF16), jnp.zeros((w_in_ab.shape[2],), F32),
                             cos, sin, seq=seq, tn=w_sb,
                             col_ops=("scale", "", "", "rope", "rope scale", "", ""))
            per = w_sb // LANES
            a = sb_attention(proj, later, batch=batch, seq=seq, q_blk=0, k_blk=per, v_blk=2 * per)
            r = retention(proj, ret_gn_g[i], batch=batch, seq=seq,
                          q_blk=3 * per, k_blk=4 * per, v_blk=5 * per, g_blk=6 * per)
            w_out = w_out_ab[i].astype(BF16)
            lhs, ws = [a, r], [w_out[:w_sb], w_out[w_sb:]]
            bias = jnp.zeros((d,), F32)
        else:
            wkv = N_KV_SWA * HEAD_DIM
            nq = N_HEADS_SWA * HEAD_DIM // wkv
            proj = norm_proj(h, attn_norm_g[layer], w_in_c[i].astype(BF16), b_in_c[i].astype(F32),
                             cos, sin, seq=seq, tn=wkv,
                             col_ops=("rope scale",) * nq + ("rope", ""))
            o = swa_attention(proj, sinks[i], batch=batch, seq=seq)
            lhs, ws = [o], [w_out_c[i].astype(BF16)]
            bias = b_out_c[i].astype(F32)
        h1, xn, route, gate, counts = outproj_router(
            lhs, ws, bias, h, ffn_norm_g[layer], router_weights(w_group[layer], w_expert_router[layer]))
        h = moe_layer(h1, xn, route, gate, counts, w_gate, w_up, w_down, layer, final_norm_g, final_norm=last)
    return h.reshape(batch, seq, d)
```

```python
import functools

import jax
import jax.numpy as jnp
import numpy as np
from jax import lax
from jax.experimental import pallas as pl
from jax.experimental.pallas import tpu as pltpu

F32 = jnp.float32
BF16 = jnp.bfloat16
I32 = jnp.int32

HEAD_DIM = 64
N_HEADS_SB = 8
N_HEADS_RET = 8
N_HEADS_SWA = 16
N_KV_SWA = 4
WINDOW = 128
ROPE_THETA = 10000.0
N_GROUPS = 4
EXPERTS_PER_GROUP = 8
N_EXPERTS = N_GROUPS * EXPERTS_PER_GROUP
MOE_BLOCK = 256
EPS = 1e-6

LANES = 128
HALF = HEAD_DIM // 2
QK_SCALE = HEAD_DIM ** -0.5

SB_TILE = 256
RET_CHUNK = 256
PROJ_TM = 512
TOK_TM = 512
ROUTER_TM = 1024
ROUTER_SUB = 256
VMEM_LIMIT = 48 * 1024 * 1024

_NT = (((1,), (1,)), ((), ()))
_TN = (((0,), (0,)), ((), ()))


def _cparams(*sem):
    return pltpu.CompilerParams(dimension_semantics=sem, vmem_limit_bytes=VMEM_LIMIT)


def _head0_mask():
    return lax.broadcasted_iota(I32, (1, LANES), 1) < HEAD_DIM


def _norm_proj_kernel(x_ref, g_ref, w_ref, b_ref, cos_ref, sin_ref, o_ref, *, tn, col_ops):
    x = x_ref[...]
    ms = jnp.mean(x * x, axis=-1, keepdims=True)
    xn = (x * lax.rsqrt(ms + EPS) * g_ref[...]).astype(BF16)
    lane = lax.broadcasted_iota(I32, (1, LANES), 1)
    first_half = (lane % HEAD_DIM) < HALF
    for j, op in enumerate(col_ops):
        cols = slice(j * tn, (j + 1) * tn)
        acc = jnp.dot(xn, w_ref[:, cols], preferred_element_type=F32) + b_ref[:, cols]
        if "rope" in op:
            cos = cos_ref[...]
            sin = sin_ref[...]
            slabs = []
            for s in range(tn // LANES):
                a = acc[:, s * LANES:(s + 1) * LANES]
                partner = jnp.where(first_half, pltpu.roll(a, LANES - HALF, 1), pltpu.roll(a, HALF, 1))
                slabs.append(a * cos + partner * sin)
            acc = jnp.concatenate(slabs, axis=1)
        if "scale" in op:
            acc = acc * QK_SCALE
        o_ref[:, cols] = acc.astype(BF16)


def norm_proj(x, g, w, b, cos, sin, *, seq, tn, col_ops):
    n, d = x.shape
    f = w.shape[1]
    tm = min(PROJ_TM, seq)
    assert n % tm == 0 and seq % tm == 0 and f == tn * len(col_ops)
    pos_blocks = seq // tm
    return pl.pallas_call(
        functools.partial(_norm_proj_kernel, tn=tn, col_ops=col_ops),
        out_shape=jax.ShapeDtypeStruct((n, f), BF16),
        grid=(n // tm,),
        in_specs=[
            pl.BlockSpec((tm, d), lambda i: (i, 0)),
            pl.BlockSpec((1, d), lambda i: (0, 0)),
            pl.BlockSpec((d, f), lambda i: (0, 0)),
            pl.BlockSpec((1, f), lambda i: (0, 0)),
            pl.BlockSpec((tm, LANES), lambda i: (i % pos_blocks, 0)),
            pl.BlockSpec((tm, LANES), lambda i: (i % pos_blocks, 0)),
        ],
        out_specs=pl.BlockSpec((tm, f), lambda i: (i, 0)),
        compiler_params=_cparams("arbitrary"),
        name="norm_proj",
    )(x, g.reshape(1, d), w, b.reshape(1, f), cos, sin)


def rope_lane_tables(seq):
    pos = jnp.arange(seq, dtype=F32)
    inv = ROPE_THETA ** (-jnp.arange(0, HEAD_DIM, 2, dtype=F32) / HEAD_DIM)
    ang = pos[:, None] * inv[None, :]
    cos, sin = jnp.cos(ang), jnp.sin(ang)
    return jnp.tile(cos, (1, 4)), jnp.tile(jnp.concatenate([-sin, sin], axis=1), (1, 2))


SB_PAIRS = 4


def _sb_kernel(q_ref, k_ref, v_ref, t_ref, o_ref, acc_ref, carry_ref, *, tile):
    qi = pl.program_id(2)
    m0 = _head0_mask()
    zero = jnp.zeros((tile, LANES), BF16)
    row = lax.broadcasted_iota(I32, (tile, tile), 0)
    col = lax.broadcasted_iota(I32, (tile, tile), 1)
    strict = col < row

    def key_tile(kb, diag):
        start = pl.multiple_of(kb * tile, tile)
        for p in range(SB_PAIRS):
            lanes = slice(p * LANES, (p + 1) * LANES)
            q = q_ref[:, lanes]
            k = k_ref[pl.ds(start, tile), lanes]
            v = v_ref[pl.ds(start, tile), lanes]
            out = None
            for h in range(2):
                qh = jnp.where(m0, q, zero) if h == 0 else jnp.where(m0, zero, q)
                vh = jnp.where(m0, v, zero) if h == 0 else jnp.where(m0, zero, v)
                z = lax.dot_general(qh, k, _NT, preferred_element_type=F32)
                log_fail = -(jnp.maximum(z, 0.0) + jnp.log(1.0 + jnp.exp(-jnp.abs(z))))
                if diag:
                    log_fail = jnp.where(strict, log_fail, 0.0)
                after = jnp.dot(log_fail.astype(BF16), t_ref[...], preferred_element_type=F32)
                logw = z + log_fail + after
                if not diag:
                    logw = logw + carry_ref[2 * p + h]
                w = jnp.exp(logw)
                if diag:
                    w = jnp.where(strict, w, 0.0)
                pv = jnp.dot(w.astype(BF16), vh, preferred_element_type=F32)
                out = pv if out is None else out + pv
                tile_sum = jnp.sum(log_fail, axis=-1, keepdims=True)
                if diag:
                    carry_ref[2 * p + h] = tile_sum
                else:
                    carry_ref[2 * p + h] += tile_sum
            if diag:
                acc_ref[:, lanes] = out
            else:
                acc_ref[:, lanes] += out

    key_tile(qi, True)

    def body(j, c):
        key_tile(qi - 1 - j, False)
        return c

    lax.fori_loop(0, qi, body, 0)
    o_ref[...] = acc_ref[...].astype(BF16)


def sb_attention(proj, later_mat, *, batch, seq, q_blk, k_blk, v_blk):
    n = proj.shape[0]
    tile = min(SB_TILE, seq)
    nq = seq // tile
    width = SB_PAIRS * LANES
    steps = N_HEADS_SB * HEAD_DIM // width
    assert q_blk % SB_PAIRS == 0 and k_blk % SB_PAIRS == 0 and v_blk % SB_PAIRS == 0
    col = lambda blk: blk // SB_PAIRS
    return pl.pallas_call(
        functools.partial(_sb_kernel, tile=tile),
        out_shape=jax.ShapeDtypeStruct((n, steps * width), BF16),
        grid=(batch, steps, nq),
        in_specs=[
            pl.BlockSpec((tile, width), lambda b, p, i: (b * nq + i, col(q_blk) + p)),
            pl.BlockSpec((seq, width), lambda b, p, i: (b, col(k_blk) + p)),
            pl.BlockSpec((seq, width), lambda b, p, i: (b, col(v_blk) + p)),
            pl.BlockSpec((tile, tile), lambda b, p, i: (0, 0)),
        ],
        out_specs=pl.BlockSpec((tile, width), lambda b, p, i: (b * nq + i, p)),
        scratch_shapes=[pltpu.VMEM((tile, width), F32), pltpu.VMEM((2 * SB_PAIRS, tile, 1), F32)],
        compiler_params=_cparams("arbitrary", "arbitrary", "arbitrary"),
        name="sb_attention",
    )(proj, proj, proj, later_mat)


RET_PAIRS = 4
RET_STEP_CHUNKS = 2


def _retention_kernel(q_ref, k_ref, v_ref, gate_ref, gain_ref, dec_ref, xi_ref, zeta_ref, gch_ref,
                      o_ref, state_ref):
    c = pl.program_id(2)

    @pl.when(c == 0)
    def _():
        state_ref[...] = jnp.zeros_like(state_ref)

    m0 = _head0_mask()
    chunk = dec_ref.shape[1]
    zero = jnp.zeros((chunk, LANES), BF16)
    r = lax.broadcasted_iota(I32, (LANES, LANES), 0) < HEAD_DIM
    cc = lax.broadcasted_iota(I32, (LANES, LANES), 1) < HEAD_DIM
    same_head = r == cc

    def head_mean(t):
        s0 = jnp.sum(jnp.where(m0, t, 0.0), axis=-1, keepdims=True)
        s1 = jnp.sum(jnp.where(m0, 0.0, t), axis=-1, keepdims=True)
        return jnp.where(m0, s0, s1) * (1.0 / HEAD_DIM)

    for p in range(RET_PAIRS):
        lanes = slice(p * LANES, (p + 1) * LANES)
        state = state_ref[p]
        for sub in range(q_ref.shape[0] // chunk):
            rows = slice(sub * chunk, (sub + 1) * chunk)
            q = q_ref[rows, lanes]
            k = k_ref[rows, lanes]
            v = v_ref[rows, lanes]
            y = jnp.dot(q, state.astype(BF16), preferred_element_type=F32) * xi_ref[:, lanes]
            for h in range(2):
                qh = jnp.where(m0, q, zero) if h == 0 else jnp.where(m0, zero, q)
                vh = jnp.where(m0, v, zero) if h == 0 else jnp.where(m0, zero, v)
                s = lax.dot_general(qh, k, _NT, preferred_element_type=F32)
                inner = (s * dec_ref[2 * p + h]).astype(BF16)
                y = y + jnp.dot(inner, vh, preferred_element_type=F32)

            kz = (k.astype(F32) * zeta_ref[:, lanes]).astype(BF16)
            upd = lax.dot_general(kz, v, _TN, preferred_element_type=F32)
            state = state * gch_ref[:, lanes] + jnp.where(same_head, upd, 0.0)

            d = y - head_mean(y)
            yn = d * lax.rsqrt(head_mean(d * d) + EPS) * gain_ref[:, lanes]
            g = gate_ref[rows, lanes].astype(F32)
            o_ref[rows, lanes] = (yn * (g * (1.0 / (1.0 + jnp.exp(-g))))).astype(BF16)
        state_ref[p] = state


def retention_tables(chunk):
    h = N_HEADS_RET
    log_g = jnp.log(1.0 - 2.0 ** (-5.0 - jnp.arange(h, dtype=F32)))
    idx = jnp.arange(chunk, dtype=F32)
    diff = idx[:, None] - idx[None, :]
    dec = jnp.where(diff[None] >= 0, jnp.exp(log_g[:, None, None] * jnp.maximum(diff, 0.0)[None]), 0.0)
    xi = jnp.exp(log_g[:, None] * (idx[None, :] + 1.0))
    zeta = jnp.exp(log_g[:, None] * (chunk - 1.0 - idx[None, :]))
    gch = jnp.exp(log_g * chunk)
    lanes = lambda t: jnp.repeat(t.T, HEAD_DIM, axis=1)
    return dec, lanes(xi), lanes(zeta), jnp.repeat(gch, HEAD_DIM)[None, :]


def retention(proj, gain, *, batch, seq, q_blk, k_blk, v_blk, g_blk):
    n = proj.shape[0]
    chunk = min(RET_CHUNK, seq)
    rows = min(RET_STEP_CHUNKS * chunk, seq)
    nc = seq // rows
    width = RET_PAIRS * LANES
    steps = N_HEADS_RET * HEAD_DIM // width
    assert all(b % RET_PAIRS == 0 for b in (q_blk, k_blk, v_blk, g_blk))
    dec, xi, zeta, gch = retention_tables(chunk)
    blk = lambda off: pl.BlockSpec((rows, width), lambda b, p, c: (b * nc + c, off // RET_PAIRS + p))
    return pl.pallas_call(
        _retention_kernel,
        out_shape=jax.ShapeDtypeStruct((n, steps * width), BF16),
        grid=(batch, steps, nc),
        in_specs=[
            blk(q_blk), blk(k_blk), blk(v_blk), blk(g_blk),
            pl.BlockSpec((1, width), lambda b, p, c: (0, p)),
            pl.BlockSpec((2 * RET_PAIRS, chunk, chunk), lambda b, p, c: (p, 0, 0)),
            pl.BlockSpec((chunk, width), lambda b, p, c: (0, p)),
            pl.BlockSpec((chunk, width), lambda b, p, c: (0, p)),
            pl.BlockSpec((1, width), lambda b, p, c: (0, p)),
        ],
        out_specs=pl.BlockSpec((rows, width), lambda b, p, c: (b * nc + c, p)),
        scratch_shapes=[pltpu.VMEM((RET_PAIRS, LANES, LANES), F32)],
        compiler_params=_cparams("arbitrary", "arbitrary", "arbitrary"),
        name="retention",
    )(proj, proj, proj, proj, gain.reshape(1, -1).astype(F32), dec, xi, zeta, gch)


SWA_BLOCKS = 4


def _swa_kernel(sink_ref, q_ref, kp_ref, kc_ref, vp_ref, vc_ref, o_ref):
    w = WINDOW
    qi = lax.broadcasted_iota(I32, (w, 2 * w), 0)
    kj = lax.broadcasted_iota(I32, (w, 2 * w), 1)
    rel = qi + w - kj
    in_window = (rel >= 0) & (rel < w)
    keys = jnp.concatenate([kp_ref[...], kc_ref[...]], axis=0)
    vals = jnp.concatenate([vp_ref[...], vc_ref[...]], axis=0)
    for sub in range(SWA_BLOCKS):
        first = (pl.program_id(1) == 0) if sub == 0 else False
        valid = in_window & (jnp.logical_not(first) | (kj >= w))
        _swa_block(sink_ref, q_ref, o_ref, slice(sub * w, (sub + 1) * w),
                   keys[sub * w:(sub + 2) * w], vals[sub * w:(sub + 2) * w], valid)


def _swa_block(sink_ref, q_ref, o_ref, rows, kband, vband, valid):
    m0 = _head0_mask()
    w = WINDOW
    group = N_HEADS_SWA // N_KV_SWA
    slabs = group * HEAD_DIM // LANES
    swap = lambda t: jnp.concatenate([t[:, HEAD_DIM:], t[:, :HEAD_DIM]], axis=1)
    zero_kv = jnp.zeros((2 * w, LANES), BF16)
    zero_q = jnp.zeros((slabs * w, LANES), BF16)
    ones_kv = jnp.ones((2 * w, LANES), BF16)

    def half_heads(q_stack, k_half, v_half, first_head):
        s = lax.dot_general(q_stack, k_half, _NT, preferred_element_type=F32)
        probs, sink_terms = [], []
        for slab in range(slabs):
            sink = sink_ref[first_head + 2 * slab]
            sh = jnp.where(valid, s[slab * w:(slab + 1) * w], -jnp.inf)
            m = jnp.maximum(jnp.max(sh, axis=-1, keepdims=True), sink)
            probs.append(jnp.exp(sh - m).astype(BF16))
            sink_terms.append(jnp.exp(sink - jnp.broadcast_to(m, (w, LANES))))
        p = jnp.concatenate(probs, axis=0)
        pv = jnp.dot(p, jnp.concatenate([v_half, ones_kv], axis=1), preferred_element_type=F32)
        denom = pv[:, LANES:] + jnp.concatenate(sink_terms, axis=0)
        return pv[:, :LANES] / denom

    for pair in range(N_KV_SWA * HEAD_DIM // LANES):
        k2 = kband[:, pair * LANES:(pair + 1) * LANES]
        v2 = vband[:, pair * LANES:(pair + 1) * LANES]
        k2s, v2s = swap(k2), swap(v2)
        for c in range(2):
            kvh = 2 * pair + c
            k_even, k_odd = (k2, k2s) if c == 0 else (k2s, k2)
            v_even, v_odd = (v2, v2s) if c == 0 else (v2s, v2)
            base = kvh * group * HEAD_DIM
            q_stack = jnp.concatenate([q_ref[rows, base + t * LANES:base + (t + 1) * LANES] for t in range(slabs)],
                                      axis=0)
            out = (half_heads(jnp.where(m0, q_stack, zero_q), k_even, jnp.where(m0, v_even, zero_kv), kvh * group)
                   + half_heads(jnp.where(m0, zero_q, q_stack), k_odd, jnp.where(m0, zero_kv, v_odd),
                                kvh * group + 1))
            for t in range(slabs):
                o_ref[rows, base + t * LANES:base + (t + 1) * LANES] = out[t * w:(t + 1) * w].astype(BF16)


def swa_attention(proj, sinks, *, batch, seq):
    n = proj.shape[0]
    w = WINDOW
    nb = seq // w
    wq = N_HEADS_SWA * HEAD_DIM
    wkv = N_KV_SWA * HEAD_DIM
    k_blk = wq // wkv
    assert nb % SWA_BLOCKS == 0
    steps = nb // SWA_BLOCKS
    rows = SWA_BLOCKS * w
    cur = lambda off: pl.BlockSpec((rows, wkv), lambda b, i: (b * steps + i, off))
    prev = lambda off: pl.BlockSpec((w, wkv), lambda b, i: (b * nb + jnp.maximum(SWA_BLOCKS * i - 1, 0), off))
    return pl.pallas_call(
        _swa_kernel,
        out_shape=jax.ShapeDtypeStruct((n, wq), BF16),
        grid=(batch, steps),
        in_specs=[
            pl.BlockSpec(memory_space=pltpu.SMEM),
            pl.BlockSpec((rows, wq), lambda b, i: (b * steps + i, 0)),
            prev(k_blk), cur(k_blk), prev(k_blk + 1), cur(k_blk + 1),
        ],
        out_specs=pl.BlockSpec((rows, wq), lambda b, i: (b * steps + i, 0)),
        compiler_params=_cparams("arbitrary", "arbitrary"),
        name="swa_attention",
    )(sinks.astype(F32), proj, proj, proj, proj, proj)


ROUTE_EID = 0
GROUP_LANE = N_EXPERTS


def _split_bf16(t):
    hi = t.astype(BF16)
    return hi, (t - hi.astype(F32)).astype(BF16)


def _outproj_router_kernel(*refs, n_lhs):
    lhs = refs[:n_lhs]
    ws = refs[n_lhs:2 * n_lhs]
    b_ref, h_ref, g_ref, wr_ref = refs[2 * n_lhs:2 * n_lhs + 4]
    h1_ref, xn_ref, route_ref, gate_ref, cnt_ref, carry_ref = refs[2 * n_lhs + 4:]
    step = pl.program_id(0)

    @pl.when(step == 0)
    def _():
        carry_ref[...] = jnp.zeros_like(carry_ref)

    w_split = jnp.concatenate(_split_bf16(wr_ref[...]), axis=1)
    for sub in range(h_ref.shape[0] // ROUTER_SUB):
        rows = slice(sub * ROUTER_SUB, (sub + 1) * ROUTER_SUB)
        mix = b_ref[...]
        for a_ref, w_ref in zip(lhs, ws):
            mix = mix + jnp.dot(a_ref[rows], w_ref[...], preferred_element_type=F32)
        h1 = h_ref[rows] + mix
        h1_ref[rows] = h1
        ms = jnp.mean(h1 * h1, axis=-1, keepdims=True)
        xn = h1 * lax.rsqrt(ms + EPS) * g_ref[...]
        _tile_store(xn_ref, sub * ROUTER_SUB * SUBLANES, ROUTER_SUB, xn)
        x_hi, x_lo = _split_bf16(xn)
        parts = (jnp.dot(x_hi, w_split, preferred_element_type=F32)
                 + jnp.dot(x_lo, w_split, preferred_element_type=F32))
        logits = parts[:, :LANES] + parts[:, LANES:]
        route, gate, cnt = _route(logits)
        route_ref[rows] = route
        gate_ref[rows] = gate
        carry_ref[...] += cnt
    cnt_ref[...] = jnp.broadcast_to(carry_ref[...], cnt_ref.shape).astype(I32)


def _route(logits):
    tm = logits.shape[0]
    lane = lax.broadcasted_iota(I32, (tm, LANES), 1)
    lane_f = lane.astype(F32)
    neg = -jnp.inf

    def lane_max(t):
        return jnp.max(t, axis=-1, keepdims=True)

    def lane_sum(t):
        return jnp.sum(t, axis=-1, keepdims=True)

    def first_lane_of(t, value, mask):
        return jnp.min(jnp.where(mask & (t == value), lane_f, float(LANES)), axis=-1, keepdims=True)

    is_group = (lane >= GROUP_LANE) & (lane < GROUP_LANE + N_GROUPS)
    gl = jnp.where(is_group, logits, neg)
    g_max = lane_max(gl)
    g_prob = 1.0 / lane_sum(jnp.exp(gl - g_max))
    g_idx = first_lane_of(gl, g_max, is_group) - float(GROUP_LANE)
    group_of_lane = lax.shift_right_logical(lane, int(np.log2(EXPERTS_PER_GROUP))).astype(F32)
    in_group = (lane < N_EXPERTS) & (group_of_lane == g_idx)
    el = jnp.where(in_group, logits, neg)
    l1 = lane_max(el)
    i1 = first_lane_of(el, l1, in_group)
    rest = in_group & (lane_f != i1)
    el2 = jnp.where(rest, el, neg)
    l2 = lane_max(el2)
    i2 = first_lane_of(el2, l2, rest)
    e2 = jnp.exp(l2 - l1)
    gate1 = g_prob / (1.0 + e2)
    gate2 = g_prob * (e2 / (1.0 + e2))

    cnt = jnp.where((lane_f == i1) | (lane_f == i2), 1.0, 0.0)
    route = jnp.where(lane == ROUTE_EID, i1, 0.0)
    route = jnp.where(lane == ROUTE_EID + 1, i2, route)
    gate = jnp.where(lane == 0, gate1, jnp.where(lane == 1, gate2, 0.0))
    return route.astype(I32), gate, jnp.sum(cnt, axis=0, keepdims=True)


def outproj_router(lhs, ws, bias, h, g, w_router):
    n, d = h.shape
    tm = ROUTER_TM
    assert n % tm == 0
    row_blk = lambda width: pl.BlockSpec((tm, width), lambda i: (i, 0))
    full = lambda a: pl.BlockSpec(a.shape, lambda i: (0, 0))
    bias2, g2 = bias.reshape(1, d), g.reshape(1, d)
    args = [*lhs, *ws, bias2, h, g2, w_router]
    in_specs = ([row_blk(a.shape[1]) for a in lhs] + [full(w) for w in ws]
                + [full(bias2), row_blk(d), full(g2), full(w_router)])
    return pl.pallas_call(
        functools.partial(_outproj_router_kernel, n_lhs=len(lhs)),
        out_shape=(
            jax.ShapeDtypeStruct((n, d), F32), jax.ShapeDtypeStruct((n * d // LANES, LANES), F32),
            jax.ShapeDtypeStruct((n, LANES), I32), jax.ShapeDtypeStruct((n, LANES), F32),
            jax.ShapeDtypeStruct((8, LANES), I32),
        ),
        grid=(n // tm,),
        in_specs=in_specs,
        out_specs=(row_blk(d), pl.BlockSpec((tm * d // LANES, LANES), lambda i: (i, 0)),
                   row_blk(LANES), row_blk(LANES), pl.BlockSpec((8, LANES), lambda i: (0, 0))),
        scratch_shapes=[pltpu.VMEM((1, LANES), F32)],
        compiler_params=_cparams("arbitrary"),
        name="outproj_router",
    )(*args)


def router_weights(w_group, w_expert_router):
    d = w_group.shape[0]
    pad = jnp.zeros((d, LANES - N_EXPERTS - N_GROUPS), F32)
    return jnp.concatenate([w_expert_router.astype(F32), w_group.astype(F32), pad], axis=1)


IDX_CHUNK = 1024
IDX_RING = 4
OUT_RING = 4
DMA_UNROLL = 8


SUBLANES = 8


def _tile_store(ref, first_row, n_tokens, value):
    for c in range(value.shape[1] // LANES):
        ref[pl.ds(first_row + c, n_tokens, stride=SUBLANES), :] = value[:, c * LANES:(c + 1) * LANES]


def _tile_load(ref, first_row, n_tokens, width, pitch=SUBLANES):
    return jnp.concatenate([ref[pl.ds(first_row + c, n_tokens, stride=pitch), :]
                            for c in range(width // LANES)], axis=1)


def _token_rows(ref, token):
    start = token * SUBLANES
    if not isinstance(start, int):
        start = pl.multiple_of(start, SUBLANES)
    return ref.at[pl.ds(start, SUBLANES), :]


def _token_copy(src, src_token, dst, dst_token, sem):
    return pltpu.make_async_copy(_token_rows(src, src_token), _token_rows(dst, dst_token), sem)


def _moe_kernel(blk_exp, q0s, nvalids, nused, order_hbm, xn_hbm, wg_ref, wu_ref, wd_ref, out_hbm,
                ibuf, xbuf, ybuf, wg_bf, wu_bf, wd_bf, isem, gsem, ssem):
    i = pl.program_id(0)
    nu = nused[0]
    m = MOE_BLOCK
    d = wd_bf.shape[1]

    window = 2 * IDX_CHUNK

    def idx_copy(blk):
        base = pl.multiple_of(q0s[blk] & ~(IDX_CHUNK - 1), IDX_CHUNK)
        slot = blk & (IDX_RING - 1)
        dst = ibuf.at[pl.ds(pl.multiple_of(slot * window, window), window)]
        return pltpu.make_async_copy(order_hbm.at[pl.ds(base, window)], dst, isem.at[slot])

    def pair_base(blk):
        return (blk & (IDX_RING - 1)) * window + (q0s[blk] & (IDX_CHUNK - 1))

    def gather_group(blk_base, dst, sem, g):
        for u in range(DMA_UNROLL):
            r = g * DMA_UNROLL + u
            pair = ibuf[blk_base + r]
            first_row = pl.multiple_of((pair & ~1) * (SUBLANES // 2), SUBLANES)
            pltpu.make_async_copy(xn_hbm.at[pl.ds(first_row, SUBLANES), :], _token_rows(dst, r), sem).start()

    def scatter_group(blk_base, src, sem, g):
        for u in range(DMA_UNROLL):
            r = g * DMA_UNROLL + u
            _token_copy(src, r, out_hbm, ibuf[blk_base + r], sem).start()

    def issue_gathers(blk, unrolled=False):
        slot = blk & 1
        args = (pair_base(blk), xbuf.at[slot], gsem.at[slot])
        if unrolled:
            for g in range(m // DMA_UNROLL):
                gather_group(*args, g)
        else:
            lax.fori_loop(0, m // DMA_UNROLL, lambda g, c: (gather_group(*args, g), c)[1], 0)

    def wait_gathers(blk):
        slot = blk & 1
        pltpu.make_async_copy(xn_hbm.at[pl.ds(0, m * SUBLANES), :], xbuf.at[slot], gsem.at[slot]).wait()

    def issue_scatters(blk, unrolled=False):
        slot = blk & (OUT_RING - 1)
        base = pair_base(blk)
        src = ybuf.at[slot]
        sem = ssem.at[slot]
        if unrolled:
            for g in range(m // DMA_UNROLL):
                scatter_group(base, src, sem, g)
            return
        nv = nvalids[blk]
        groups = lax.shift_right_logical(nv, DMA_UNROLL.bit_length() - 1)

        def tail(r, c):
            _token_copy(src, r, out_hbm, ibuf[base + r], sem).start()
            return c

        lax.fori_loop(0, groups, lambda g, c: (scatter_group(base, src, sem, g), c)[1], 0)
        lax.fori_loop(groups * DMA_UNROLL, nv, tail, 0)

    def wait_scatters(blk):
        slot = blk & (OUT_RING - 1)
        nv = nvalids[blk]
        rows = pl.multiple_of(nv * SUBLANES, SUBLANES)

        @pl.when(nv > 0)
        def _():
            pltpu.make_async_copy(ybuf.at[slot].at[pl.ds(0, rows), :], out_hbm.at[pl.ds(0, rows), :],
                                  ssem.at[slot]).wait()

    @pl.when(i == 0)
    def _():
        first = idx_copy(0)
        first.start()
        first.wait()
        issue_gathers(0)

        @pl.when(nu > 1)
        def _():
            idx_copy(1).start()

    def expert():
        x = _tile_load(xbuf.at[i & 1], 0, m, d).astype(BF16)
        gate = jnp.dot(x, wg_bf[...], preferred_element_type=F32)
        up = jnp.dot(x, wu_bf[...], preferred_element_type=F32)
        hidden = (gate * (1.0 / (1.0 + jnp.exp(-gate))) * up).astype(BF16)
        _tile_store(ybuf.at[i & (OUT_RING - 1)], 0, m, jnp.dot(hidden, wd_bf[...], preferred_element_type=F32))

    @pl.when(i < nu)
    def _():
        @pl.when(i + 2 < nu)
        def _():
            idx_copy(i + 2).start()

        @pl.when(i + 1 < nu)
        def _():
            idx_copy(i + 1).wait()

        wait_gathers(i)

        @pl.when((i == 0) | (blk_exp[i] != blk_exp[jnp.maximum(i - 1, 0)]))
        def _():
            wg_bf[...] = wg_ref[0, 0].astype(BF16)
            wu_bf[...] = wu_ref[0, 0].astype(BF16)
            wd_bf[...] = wd_ref[0, 0].astype(BF16)

        @pl.when(i >= OUT_RING - 1)
        def _():
            wait_scatters(i - (OUT_RING - 1))

        prev = jnp.maximum(i - 1, 0)
        steady = (i >= 1) & (i + 1 < nu) & (nvalids[prev] == m)

        @pl.when(steady)
        def _():
            issue_gathers(i + 1, unrolled=True)
            issue_scatters(i - 1, unrolled=True)
            expert()

        @pl.when(jnp.logical_not(steady))
        def _():
            @pl.when(i + 1 < nu)
            def _():
                issue_gathers(i + 1)

            @pl.when(i >= 1)
            def _():
                issue_scatters(i - 1)

            expert()

        @pl.when(i == nu - 1)
        def _():
            issue_scatters(i)
            for back in range(OUT_RING - 2, -1, -1):
                @pl.when(i >= back)
                def _():
                    wait_scatters(i - back)


def moe_experts(order, plan, xn, w_gate, w_up, w_down, layer):
    blk_exp, q0s, nvalids, nused = plan
    d, de = w_gate.shape[2], w_gate.shape[3]
    assert d == SUBLANES * LANES
    rows = xn.shape[0]
    m = MOE_BLOCK
    w_in_spec = pl.BlockSpec((1, 1, d, de), lambda i, be, q0, nv, nu: (layer, be[i], 0, 0))
    return pl.pallas_call(
        _moe_kernel,
        out_shape=jax.ShapeDtypeStruct((2 * rows, LANES), F32),
        grid_spec=pltpu.PrefetchScalarGridSpec(
            num_scalar_prefetch=4,
            grid=(blk_exp.shape[0],),
            in_specs=[
                pl.BlockSpec(memory_space=pl.ANY), pl.BlockSpec(memory_space=pl.ANY),
                w_in_spec, w_in_spec,
                pl.BlockSpec((1, 1, de, d), lambda i, be, q0, nv, nu: (layer, be[i], 0, 0)),
            ],
            out_specs=pl.BlockSpec(memory_space=pl.ANY),
            scratch_shapes=[
                pltpu.SMEM((IDX_RING * 2 * IDX_CHUNK,), I32),
                pltpu.VMEM((2, m * SUBLANES, LANES), F32), pltpu.VMEM((OUT_RING, m * SUBLANES, LANES), F32),
                pltpu.VMEM((d, de), BF16), pltpu.VMEM((d, de), BF16), pltpu.VMEM((de, d), BF16),
                pltpu.SemaphoreType.DMA((IDX_RING,)), pltpu.SemaphoreType.DMA((2,)),
                pltpu.SemaphoreType.DMA((OUT_RING,)),
            ],
        ),
        compiler_params=_cparams("arbitrary"),
        name="moe_experts",
    )(blk_exp, q0s, nvalids, nused, order, xn, w_gate, w_up, w_down)


def _moe_finish_kernel(y_ref, gate_ref, h_ref, g_ref, o_ref, *, final_norm):
    gate = gate_ref[...]
    tm, d = h_ref.shape
    y0 = _tile_load(y_ref, 0, tm, d, pitch=2 * SUBLANES)
    y1 = _tile_load(y_ref, SUBLANES, tm, d, pitch=2 * SUBLANES)
    out = h_ref[...] + (y0 * gate[:, 0:1] + y1 * gate[:, 1:2])
    if final_norm:
        ms = jnp.mean(out * out, axis=-1, keepdims=True)
        out = out * lax.rsqrt(ms + EPS) * g_ref[...]
    o_ref[...] = out


def moe_finish(ys, gate, h, g, *, final_norm):
    n, d = h.shape
    tm = TOK_TM
    row_blk = lambda width: pl.BlockSpec((tm, width), lambda i: (i, 0))
    return pl.pallas_call(
        functools.partial(_moe_finish_kernel, final_norm=final_norm),
        out_shape=jax.ShapeDtypeStruct((n, d), F32),
        grid=(n // tm,),
        in_specs=[pl.BlockSpec((2 * tm * SUBLANES, LANES), lambda i: (i, 0)),
                  row_blk(LANES), row_blk(d), pl.BlockSpec((1, d), lambda i: (0, 0))],
        out_specs=row_blk(d),
        compiler_params=_cparams("arbitrary"),
        name="moe_finish",
    )(ys, gate, h, g.reshape(1, d))


def dispatch_plan(route, counts):
    m = MOE_BLOCK
    n = route.shape[0]
    assert (2 * n) % IDX_CHUNK == 0
    counts = counts[0, :N_EXPERTS]
    padded = ((counts + m - 1) // m) * m
    pend = jnp.cumsum(padded)
    pstart = pend - padded
    start = jnp.cumsum(counts) - counts
    eid = route[:, ROUTE_EID:ROUTE_EID + 2].reshape(-1)
    order = jnp.argsort(eid, stable=True).astype(I32)
    order = jnp.concatenate([order, jnp.zeros((2 * IDX_CHUNK,), I32)])
    nblk = (2 * n + N_EXPERTS * m) // m
    blk_start = jnp.arange(nblk, dtype=I32) * m
    nused = pend[-1] // m
    used = jnp.arange(nblk) < nused
    e = jnp.minimum(jnp.sum(blk_start[:, None] >= pend[None, :], axis=1), N_EXPERTS - 1)
    is_e = e[:, None] == jnp.arange(N_EXPERTS)[None, :]
    of_block = lambda table: jnp.sum(jnp.where(is_e, table[None, :], 0), axis=1)
    r0 = blk_start - of_block(pstart)
    q0s = jnp.where(used, of_block(start) + r0, 0)
    nvalids = jnp.where(used, jnp.clip(of_block(counts) - r0, 0, m), 0)
    last_exp = jnp.sum(jnp.where(jnp.arange(nblk) == nused - 1, e, 0))
    blk_exp = jnp.where(used, e, last_exp)
    as_i32 = lambda t: t.astype(I32)
    return order, (as_i32(blk_exp), as_i32(q0s), as_i32(nvalids), as_i32(nused).reshape(1))


def moe_layer(h1, xn, route, gate, counts, w_gate, w_up, w_down, layer, norm_g, *, final_norm):
    order, plan = dispatch_plan(route, counts)
    ys = moe_experts(order, plan, xn, w_gate, w_up, w_down, layer)
    return moe_finish(ys, gate, h1, norm_g, final_norm=final_norm)


def kernel(x, attn_norm_g, ffn_norm_g, w_in_ab, w_out_ab, ret_gn_g, w_in_c, b_in_c, sinks, w_out_c, b_out_c,
           w_group, w_expert_router, w_gate, w_up, w_down, final_norm_g):
    batch, seq, d = x.shape
    n = batch * seq
    depth = attn_norm_g.shape[0]
    cos, sin = rope_lane_tables(seq)
    later = (jnp.arange(min(SB_TILE, seq))[:, None] > jnp.arange(min(SB_TILE, seq))[None, :]).astype(BF16)
    w_sb = N_HEADS_SB * HEAD_DIM
    w_ret = N_HEADS_RET * HEAD_DIM
    h = x.reshape(n, d)
    for layer in range(depth):
        i = layer // 2
        last = layer == depth - 1
        if layer % 2 == 0:
            proj = norm_proj(h, attn_norm_g[layer], w_in_ab[i].astype(BF16), jnp.zeros((w_in_ab.shape[2],), F32),
                             cos, sin, seq=seq, tn=w_sb,
                             col_ops=("scale", "", "", "rope", "rope scale", "", ""))
            per = w_sb // LANES
            a = sb_attention(proj, later, batch=batch, seq=seq, q_blk=0, k_blk=per, v_blk=2 * per)
            r = retention(proj, ret_gn_g[i], batch=batch, seq=seq,
                          q_blk=3 * per, k_blk=4 * per, v_blk=5 * per, g_blk=6 * per)
            w_out = w_out_ab[i].astype(BF16)
            lhs, ws = [a, r], [w_out[:w_sb], w_out[w_sb:]]
            bias = jnp.zeros((d,), F32)
        else:
            wkv = N_KV_SWA * HEAD_DIM
            nq = N_HEADS_SWA * HEAD_DIM // wkv
            proj = norm_proj(h, attn_norm_g[layer], w_in_c[i].astype(BF16), b_in_c[i].astype(F32),
                             cos, sin, seq=seq, tn=wkv,
                             col_ops=("rope scale",) * nq + ("rope", ""))
            o = swa_attention(proj, sinks[i], batch=batch, seq=seq)
            lhs, ws = [o], [w_out_c[i].astype(BF16)]
            bias = b_out_c[i].astype(F32)
        h1, xn, route, gate, counts = outproj_router(
            lhs, ws, bias, h, ffn_norm_g[layer], router_weights(w_group[layer], w_expert_router[layer]))
        h = moe_layer(h1, xn, route, gate, counts, w_gate, w_up, w_down, layer, final_norm_g, final_norm=last)
    return h.reshape(batch, seq, d)
```

```python
import functools

import jax
import jax.numpy as jnp
import numpy as np
from jax import lax
from jax.experimental import pallas as pl
from jax.experimental.pallas import tpu as pltpu

F32 = jnp.float32
BF16 = jnp.bfloat16
I32 = jnp.int32

HEAD_DIM = 64
N_HEADS_SB = 8
N_HEADS_RET = 8
N_HEADS_SWA = 16
N_KV_SWA = 4
WINDOW = 128
ROPE_THETA = 10000.0
N_GROUPS = 4
EXPERTS_PER_GROUP = 8
N_EXPERTS = N_GROUPS * EXPERTS_PER_GROUP
MOE_BLOCK = 256
EPS = 1e-6

LANES = 128
HALF = HEAD_DIM // 2
QK_SCALE = HEAD_DIM ** -0.5

SB_TILE = 256
RET_CHUNK = 256
PROJ_TM = 512
TOK_TM = 512
ROUTER_TM = 1024
ROUTER_SUB = 256
VMEM_LIMIT = 48 * 1024 * 1024

_NT = (((1,), (1,)), ((), ()))
_TN = (((0,), (0,)), ((), ()))


def _cparams(*sem):
    return pltpu.CompilerParams(dimension_semantics=sem, vmem_limit_bytes=VMEM_LIMIT)


def _head0_mask():
    return lax.broadcasted_iota(I32, (1, LANES), 1) < HEAD_DIM


def _norm_proj_kernel(*refs, tn, col_ops, combine):
    if combine:
        y_ref, gate_ref, x_ref, g_ref, w_ref, b_ref, cos_ref, sin_ref, h_ref, o_ref = refs
        x = _gated_sum(y_ref, gate_ref, x_ref[...])
        h_ref[...] = x
    else:
        x_ref, g_ref, w_ref, b_ref, cos_ref, sin_ref, o_ref = refs
        x = x_ref[...]
    ms = jnp.mean(x * x, axis=-1, keepdims=True)
    xn = (x * lax.rsqrt(ms + EPS) * g_ref[...]).astype(BF16)
    lane = lax.broadcasted_iota(I32, (1, LANES), 1)
    first_half = (lane % HEAD_DIM) < HALF
    for j, op in enumerate(col_ops):
        cols = slice(j * tn, (j + 1) * tn)
        acc = jnp.dot(xn, w_ref[:, cols], preferred_element_type=F32) + b_ref[:, cols]
        if "rope" in op:
            cos = cos_ref[...]
            sin = sin_ref[...]
            slabs = []
            for s in range(tn // LANES):
                a = acc[:, s * LANES:(s + 1) * LANES]
                partner = jnp.where(first_half, pltpu.roll(a, LANES - HALF, 1), pltpu.roll(a, HALF, 1))
                slabs.append(a * cos + partner * sin)
            acc = jnp.concatenate(slabs, axis=1)
        if "scale" in op:
            acc = acc * QK_SCALE
        o_ref[:, cols] = acc.astype(BF16)


def norm_proj(x, g, w, b, cos, sin, *, seq, tn, col_ops, moe=None):
    n, d = x.shape
    f = w.shape[1]
    tm = min(PROJ_TM, seq)
    assert n % tm == 0 and seq % tm == 0 and f == tn * len(col_ops)
    pos_blocks = seq // tm
    row_blk = lambda width: pl.BlockSpec((tm, width), lambda i: (i, 0))
    in_specs = [
        row_blk(d),
        pl.BlockSpec((1, d), lambda i: (0, 0)),
        pl.BlockSpec((d, f), lambda i: (0, 0)),
        pl.BlockSpec((1, f), lambda i: (0, 0)),
        pl.BlockSpec((tm, LANES), lambda i: (i % pos_blocks, 0)),
        pl.BlockSpec((tm, LANES), lambda i: (i % pos_blocks, 0)),
    ]
    args = (x, g.reshape(1, d), w, b.reshape(1, f), cos, sin)
    proj_shape, proj_spec = jax.ShapeDtypeStruct((n, f), BF16), row_blk(f)
    call = functools.partial(
        pl.pallas_call, functools.partial(_norm_proj_kernel, tn=tn, col_ops=col_ops, combine=moe is not None),
        grid=(n // tm,), compiler_params=_cparams("arbitrary"), name="norm_proj")
    if moe is None:
        return call(out_shape=proj_shape, in_specs=in_specs, out_specs=proj_spec)(*args), x
    ys, gate = moe
    pair_tiles = pl.BlockSpec((2 * tm * SUBLANES, LANES), lambda i: (i, 0))
    h, proj = call(out_shape=(jax.ShapeDtypeStruct((n, d), F32), proj_shape),
                   in_specs=[pair_tiles, row_blk(LANES)] + in_specs,
                   out_specs=(row_blk(d), proj_spec))(ys, gate, *args)
    return proj, h


def rope_lane_tables(seq):
    pos = jnp.arange(seq, dtype=F32)
    inv = ROPE_THETA ** (-jnp.arange(0, HEAD_DIM, 2, dtype=F32) / HEAD_DIM)
    ang = pos[:, None] * inv[None, :]
    cos, sin = jnp.cos(ang), jnp.sin(ang)
    return jnp.tile(cos, (1, 4)), jnp.tile(jnp.concatenate([-sin, sin], axis=1), (1, 2))


SB_PAIRS = 4


def _sb_kernel(q_ref, k_ref, v_ref, t_ref, o_ref, acc_ref, carry_ref, *, tile):
    qi = pl.program_id(2)
    m0 = _head0_mask()
    zero = jnp.zeros((tile, LANES), BF16)
    row = lax.broadcasted_iota(I32, (tile, tile), 0)
    col = lax.broadcasted_iota(I32, (tile, tile), 1)
    strict = col < row

    def key_tile(kb, diag):
        start = pl.multiple_of(kb * tile, tile)
        for p in range(SB_PAIRS):
            lanes = slice(p * LANES, (p + 1) * LANES)
            q = q_ref[:, lanes]
            k = k_ref[pl.ds(start, tile), lanes]
            v = v_ref[pl.ds(start, tile), lanes]
            out = None
            for h in range(2):
                qh = jnp.where(m0, q, zero) if h == 0 else jnp.where(m0, zero, q)
                vh = jnp.where(m0, v, zero) if h == 0 else jnp.where(m0, zero, v)
                z = lax.dot_general(qh, k, _NT, preferred_element_type=F32)
                log_fail = -(jnp.maximum(z, 0.0) + jnp.log(1.0 + jnp.exp(-jnp.abs(z))))
                if diag:
                    log_fail = jnp.where(strict, log_fail, 0.0)
                after = jnp.dot(log_fail.astype(BF16), t_ref[...], preferred_element_type=F32)
                logw = z + log_fail + after
                if not diag:
                    logw = logw + carry_ref[2 * p + h]
                w = jnp.exp(logw)
                if diag:
                    w = jnp.where(strict, w, 0.0)
                pv = jnp.dot(w.astype(BF16), vh, preferred_element_type=F32)
                out = pv if out is None else out + pv
                tile_sum = jnp.sum(log_fail, axis=-1, keepdims=True)
                if diag:
                    carry_ref[2 * p + h] = tile_sum
                else:
                    carry_ref[2 * p + h] += tile_sum
            if diag:
                acc_ref[:, lanes] = out
            else:
                acc_ref[:, lanes] += out

    key_tile(qi, True)

    def body(j, c):
        key_tile(qi - 1 - j, False)
        return c

    lax.fori_loop(0, qi, body, 0)
    o_ref[...] = acc_ref[...].astype(BF16)


def sb_attention(proj, later_mat, *, batch, seq, q_blk, k_blk, v_blk):
    n = proj.shape[0]
    tile = min(SB_TILE, seq)
    nq = seq // tile
    width = SB_PAIRS * LANES
    steps = N_HEADS_SB * HEAD_DIM // width
    assert q_blk % SB_PAIRS == 0 and k_blk % SB_PAIRS == 0 and v_blk % SB_PAIRS == 0
    col = lambda blk: blk // SB_PAIRS
    return pl.pallas_call(
        functools.partial(_sb_kernel, tile=tile),
        out_shape=jax.ShapeDtypeStruct((n, steps * width), BF16),
        grid=(batch, steps, nq),
        in_specs=[
            pl.BlockSpec((tile, width), lambda b, p, i: (b * nq + i, col(q_blk) + p)),
            pl.BlockSpec((seq, width), lambda b, p, i: (b, col(k_blk) + p)),
            pl.BlockSpec((seq, width), lambda b, p, i: (b, col(v_blk) + p)),
            pl.BlockSpec((tile, tile), lambda b, p, i: (0, 0)),
        ],
        out_specs=pl.BlockSpec((tile, width), lambda b, p, i: (b * nq + i, p)),
        scratch_shapes=[pltpu.VMEM((tile, width), F32), pltpu.VMEM((2 * SB_PAIRS, tile, 1), F32)],
        compiler_params=_cparams("arbitrary", "arbitrary", "arbitrary"),
        name="sb_attention",
    )(proj, proj, proj, later_mat)


RET_PAIRS = 4
RET_STEP_CHUNKS = 2


def _retention_kernel(q_ref, k_ref, v_ref, gate_ref, gain_ref, dec_ref, xi_ref, zeta_ref, gch_ref,
                      o_ref, state_ref):
    c = pl.program_id(2)

    @pl.when(c == 0)
    def _():
        state_ref[...] = jnp.zeros_like(state_ref)

    m0 = _head0_mask()
    chunk = dec_ref.shape[1]
    zero = jnp.zeros((chunk, LANES), BF16)
    r = lax.broadcasted_iota(I32, (LANES, LANES), 0) < HEAD_DIM
    cc = lax.broadcasted_iota(I32, (LANES, LANES), 1) < HEAD_DIM
    same_head = r == cc

    def head_mean(t):
        s0 = jnp.sum(jnp.where(m0, t, 0.0), axis=-1, keepdims=True)
        s1 = jnp.sum(jnp.where(m0, 0.0, t), axis=-1, keepdims=True)
        return jnp.where(m0, s0, s1) * (1.0 / HEAD_DIM)

    for p in range(RET_PAIRS):
        lanes = slice(p * LANES, (p + 1) * LANES)
        state = state_ref[p]
        for sub in range(q_ref.shape[0] // chunk):
            rows = slice(sub * chunk, (sub + 1) * chunk)
            q = q_ref[rows, lanes]
            k = k_ref[rows, lanes]
            v = v_ref[rows, lanes]
            y = jnp.dot(q, state.astype(BF16), preferred_element_type=F32) * xi_ref[:, lanes]
            for h in range(2):
                qh = jnp.where(m0, q, zero) if h == 0 else jnp.where(m0, zero, q)
                vh = jnp.where(m0, v, zero) if h == 0 else jnp.where(m0, zero, v)
                s = lax.dot_general(qh, k, _NT, preferred_element_type=F32)
                inner = (s * dec_ref[2 * p + h]).astype(BF16)
                y = y + jnp.dot(inner, vh, preferred_element_type=F32)

            kz = (k.astype(F32) * zeta_ref[:, lanes]).astype(BF16)
            upd = lax.dot_general(kz, v, _TN, preferred_element_type=F32)
            state = state * gch_ref[:, lanes] + jnp.where(same_head, upd, 0.0)

            d = y - head_mean(y)
            yn = d * lax.rsqrt(head_mean(d * d) + EPS) * gain_ref[:, lanes]
            g = gate_ref[rows, lanes].astype(F32)
            o_ref[rows, lanes] = (yn * (g * (1.0 / (1.0 + jnp.exp(-g))))).astype(BF16)
        state_ref[p] = state


def retention_tables(chunk):
    h = N_HEADS_RET
    log_g = jnp.log(1.0 - 2.0 ** (-5.0 - jnp.arange(h, dtype=F32)))
    idx = jnp.arange(chunk, dtype=F32)
    diff = idx[:, None] - idx[None, :]
    dec = jnp.where(diff[None] >= 0, jnp.exp(log_g[:, None, None] * jnp.maximum(diff, 0.0)[None]), 0.0)
    xi = jnp.exp(log_g[:, None] * (idx[None, :] + 1.0))
    zeta = jnp.exp(log_g[:, None] * (chunk - 1.0 - idx[None, :]))
    gch = jnp.exp(log_g * chunk)
    lanes = lambda t: jnp.repeat(t.T, HEAD_DIM, axis=1)
    return dec, lanes(xi), lanes(zeta), jnp.repeat(gch, HEAD_DIM)[None, :]


def retention(proj, gain, *, batch, seq, q_blk, k_blk, v_blk, g_blk):
    n = proj.shape[0]
    chunk = min(RET_CHUNK, seq)
    rows = min(RET_STEP_CHUNKS * chunk, seq)
    nc = seq // rows
    width = RET_PAIRS * LANES
    steps = N_HEADS_RET * HEAD_DIM // width
    assert all(b % RET_PAIRS == 0 for b in (q_blk, k_blk, v_blk, g_blk))
    dec, xi, zeta, gch = retention_tables(chunk)
    blk = lambda off: pl.BlockSpec((rows, width), lambda b, p, c: (b * nc + c, off // RET_PAIRS + p))
    return pl.pallas_call(
        _retention_kernel,
        out_shape=jax.ShapeDtypeStruct((n, steps * width), BF16),
        grid=(batch, steps, nc),
        in_specs=[
            blk(q_blk), blk(k_blk), blk(v_blk), blk(g_blk),
            pl.BlockSpec((1, width), lambda b, p, c: (0, p)),
            pl.BlockSpec((2 * RET_PAIRS, chunk, chunk), lambda b, p, c: (p, 0, 0)),
            pl.BlockSpec((chunk, width), lambda b, p, c: (0, p)),
            pl.BlockSpec((chunk, width), lambda b, p, c: (0, p)),
            pl.BlockSpec((1, width), lambda b, p, c: (0, p)),
        ],
        out_specs=pl.BlockSpec((rows, width), lambda b, p, c: (b * nc + c, p)),
        scratch_shapes=[pltpu.VMEM((RET_PAIRS, LANES, LANES), F32)],
        compiler_params=_cparams("arbitrary", "arbitrary", "arbitrary"),
        name="retention",
    )(proj, proj, proj, proj, gain.reshape(1, -1).astype(F32), dec, xi, zeta, gch)


SWA_BLOCKS = 4


def _swa_kernel(sink_ref, q_ref, kp_ref, kc_ref, vp_ref, vc_ref, o_ref):
    w = WINDOW
    qi = lax.broadcasted_iota(I32, (w, 2 * w), 0)
    kj = lax.broadcasted_iota(I32, (w, 2 * w), 1)
    rel = qi + w - kj
    in_window = (rel >= 0) & (rel < w)
    keys = jnp.concatenate([kp_ref[...], kc_ref[...]], axis=0)
    vals = jnp.concatenate([vp_ref[...], vc_ref[...]], axis=0)
    for sub in range(SWA_BLOCKS):
        first = (pl.program_id(1) == 0) if sub == 0 else False
        valid = in_window & (jnp.logical_not(first) | (kj >= w))
        _swa_block(sink_ref, q_ref, o_ref, slice(sub * w, (sub + 1) * w),
                   keys[sub * w:(sub + 2) * w], vals[sub * w:(sub + 2) * w], valid)


def _swa_block(sink_ref, q_ref, o_ref, rows, kband, vband, valid):
    m0 = _head0_mask()
    w = WINDOW
    group = N_HEADS_SWA // N_KV_SWA
    slabs = group * HEAD_DIM // LANES
    swap = lambda t: jnp.concatenate([t[:, HEAD_DIM:], t[:, :HEAD_DIM]], axis=1)
    zero_kv = jnp.zeros((2 * w, LANES), BF16)
    zero_q = jnp.zeros((slabs * w, LANES), BF16)
    ones_kv = jnp.ones((2 * w, LANES), BF16)

    def half_heads(q_stack, k_half, v_half, first_head):
        s = lax.dot_general(q_stack, k_half, _NT, preferred_element_type=F32)
        probs, sink_terms = [], []
        for slab in range(slabs):
            sink = sink_ref[first_head + 2 * slab]
            sh = jnp.where(valid, s[slab * w:(slab + 1) * w], -jnp.inf)
            m = jnp.maximum(jnp.max(sh, axis=-1, keepdims=True), sink)
            probs.append(jnp.exp(sh - m).astype(BF16))
            sink_terms.append(jnp.exp(sink - jnp.broadcast_to(m, (w, LANES))))
        p = jnp.concatenate(probs, axis=0)
        pv = jnp.dot(p, jnp.concatenate([v_half, ones_kv], axis=1), preferred_element_type=F32)
        denom = pv[:, LANES:] + jnp.concatenate(sink_terms, axis=0)
        return pv[:, :LANES] / denom

    for pair in range(N_KV_SWA * HEAD_DIM // LANES):
        k2 = kband[:, pair * LANES:(pair + 1) * LANES]
        v2 = vband[:, pair * LANES:(pair + 1) * LANES]
        k2s, v2s = swap(k2), swap(v2)
        for c in range(2):
            kvh = 2 * pair + c
            k_even, k_odd = (k2, k2s) if c == 0 else (k2s, k2)
            v_even, v_odd = (v2, v2s) if c == 0 else (v2s, v2)
            base = kvh * group * HEAD_DIM
            q_stack = jnp.concatenate([q_ref[rows, base + t * LANES:base + (t + 1) * LANES] for t in range(slabs)],
                                      axis=0)
            out = (half_heads(jnp.where(m0, q_stack, zero_q), k_even, jnp.where(m0, v_even, zero_kv), kvh * group)
                   + half_heads(jnp.where(m0, zero_q, q_stack), k_odd, jnp.where(m0, zero_kv, v_odd),
                                kvh * group + 1))
            for t in range(slabs):
                o_ref[rows, base + t * LANES:base + (t + 1) * LANES] = out[t * w:(t + 1) * w].astype(BF16)


def swa_attention(proj, sinks, *, batch, seq):
    n = proj.shape[0]
    w = WINDOW
    nb = seq // w
    wq = N_HEADS_SWA * HEAD_DIM
    wkv = N_KV_SWA * HEAD_DIM
    k_blk = wq // wkv
    assert nb % SWA_BLOCKS == 0
    steps = nb // SWA_BLOCKS
    rows = SWA_BLOCKS * w
    cur = lambda off: pl.BlockSpec((rows, wkv), lambda b, i: (b * steps + i, off))
    prev = lambda off: pl.BlockSpec((w, wkv), lambda b, i: (b * nb + jnp.maximum(SWA_BLOCKS * i - 1, 0), off))
    return pl.pallas_call(
        _swa_kernel,
        out_shape=jax.ShapeDtypeStruct((n, wq), BF16),
        grid=(batch, steps),
        in_specs=[
            pl.BlockSpec(memory_space=pltpu.SMEM),
            pl.BlockSpec((rows, wq), lambda b, i: (b * steps + i, 0)),
            prev(k_blk), cur(k_blk), prev(k_blk + 1), cur(k_blk + 1),
        ],
        out_specs=pl.BlockSpec((rows, wq), lambda b, i: (b * steps + i, 0)),
        compiler_params=_cparams("arbitrary", "arbitrary"),
        name="swa_attention",
    )(sinks.astype(F32), proj, proj, proj, proj, proj)


ROUTE_EID = 0
GROUP_LANE = N_EXPERTS


def _split_bf16(t):
    hi = t.astype(BF16)
    return hi, (t - hi.astype(F32)).astype(BF16)


def _outproj_router_kernel(*refs, n_lhs):
    lhs = refs[:n_lhs]
    ws = refs[n_lhs:2 * n_lhs]
    b_ref, h_ref, g_ref, wr_ref = refs[2 * n_lhs:2 * n_lhs + 4]
    h1_ref, xn_ref, route_ref, gate_ref, cnt_ref, carry_ref = refs[2 * n_lhs + 4:]
    step = pl.program_id(0)

    @pl.when(step == 0)
    def _():
        carry_ref[...] = jnp.zeros_like(carry_ref)

    w_split = jnp.concatenate(_split_bf16(wr_ref[...]), axis=1)
    for sub in range(h_ref.shape[0] // ROUTER_SUB):
        rows = slice(sub * ROUTER_SUB, (sub + 1) * ROUTER_SUB)
        mix = b_ref[...]
        for a_ref, w_ref in zip(lhs, ws):
            mix = mix + jnp.dot(a_ref[rows], w_ref[...], preferred_element_type=F32)
        h1 = h_ref[rows] + mix
        h1_ref[rows] = h1
        ms = jnp.mean(h1 * h1, axis=-1, keepdims=True)
        xn = h1 * lax.rsqrt(ms + EPS) * g_ref[...]
        _tile_store(xn_ref, sub * ROUTER_SUB * SUBLANES, ROUTER_SUB, xn)
        x_hi, x_lo = _split_bf16(xn)
        parts = (jnp.dot(x_hi, w_split, preferred_element_type=F32)
                 + jnp.dot(x_lo, w_split, preferred_element_type=F32))
        logits = parts[:, :LANES] + parts[:, LANES:]
        route, gate, cnt = _route(logits)
        route_ref[rows] = route
        gate_ref[rows] = gate
        carry_ref[...] += cnt
    cnt_ref[...] = jnp.broadcast_to(carry_ref[...], cnt_ref.shape).astype(I32)


def _route(logits):
    tm = logits.shape[0]
    lane = lax.broadcasted_iota(I32, (tm, LANES), 1)
    lane_f = lane.astype(F32)
    neg = -jnp.inf

    def lane_max(t):
        return jnp.max(t, axis=-1, keepdims=True)

    def lane_sum(t):
        return jnp.sum(t, axis=-1, keepdims=True)

    def first_lane_of(t, value, mask):
        return jnp.min(jnp.where(mask & (t == value), lane_f, float(LANES)), axis=-1, keepdims=True)

    is_group = (lane >= GROUP_LANE) & (lane < GROUP_LANE + N_GROUPS)
    gl = jnp.where(is_group, logits, neg)
    g_max = lane_max(gl)
    g_prob = 1.0 / lane_sum(jnp.exp(gl - g_max))
    g_idx = first_lane_of(gl, g_max, is_group) - float(GROUP_LANE)
    group_of_lane = lax.shift_right_logical(lane, int(np.log2(EXPERTS_PER_GROUP))).astype(F32)
    in_group = (lane < N_EXPERTS) & (group_of_lane == g_idx)
    el = jnp.where(in_group, logits, neg)
    l1 = lane_max(el)
    i1 = first_lane_of(el, l1, in_group)
    rest = in_group & (lane_f != i1)
    el2 = jnp.where(rest, el, neg)
    l2 = lane_max(el2)
    i2 = first_lane_of(el2, l2, rest)
    e2 = jnp.exp(l2 - l1)
    gate1 = g_prob / (1.0 + e2)
    gate2 = g_prob * (e2 / (1.0 + e2))

    cnt = jnp.where((lane_f == i1) | (lane_f == i2), 1.0, 0.0)
    route = jnp.where(lane == ROUTE_EID, i1, 0.0)
    route = jnp.where(lane == ROUTE_EID + 1, i2, route)
    gate = jnp.where(lane == 0, gate1, jnp.where(lane == 1, gate2, 0.0))
    return route.astype(I32), gate, jnp.sum(cnt, axis=0, keepdims=True)


def outproj_router(lhs, ws, bias, h, g, w_router):
    n, d = h.shape
    tm = ROUTER_TM
    assert n % tm == 0
    row_blk = lambda width: pl.BlockSpec((tm, width), lambda i: (i, 0))
    full = lambda a: pl.BlockSpec(a.shape, lambda i: (0, 0))
    bias2, g2 = bias.reshape(1, d), g.reshape(1, d)
    args = [*lhs, *ws, bias2, h, g2, w_router]
    in_specs = ([row_blk(a.shape[1]) for a in lhs] + [full(w) for w in ws]
                + [full(bias2), row_blk(d), full(g2), full(w_router)])
    return pl.pallas_call(
        functools.partial(_outproj_router_kernel, n_lhs=len(lhs)),
        out_shape=(
            jax.ShapeDtypeStruct((n, d), F32), jax.ShapeDtypeStruct((n * d // LANES, LANES), F32),
            jax.ShapeDtypeStruct((n, LANES), I32), jax.ShapeDtypeStruct((n, LANES), F32),
            jax.ShapeDtypeStruct((8, LANES), I32),
        ),
        grid=(n // tm,),
        in_specs=in_specs,
        out_specs=(row_blk(d), pl.BlockSpec((tm * d // LANES, LANES), lambda i: (i, 0)),
                   row_blk(LANES), row_blk(LANES), pl.BlockSpec((8, LANES), lambda i: (0, 0))),
        scratch_shapes=[pltpu.VMEM((1, LANES), F32)],
        compiler_params=_cparams("arbitrary"),
        name="outproj_router",
    )(*args)


def router_weights(w_group, w_expert_router):
    d = w_group.shape[0]
    pad = jnp.zeros((d, LANES - N_EXPERTS - N_GROUPS), F32)
    return jnp.concatenate([w_expert_router.astype(F32), w_group.astype(F32), pad], axis=1)


IDX_CHUNK = 1024
IDX_RING = 4
OUT_RING = 4
DMA_UNROLL = 8


SUBLANES = 8


def _tile_store(ref, first_row, n_tokens, value):
    for c in range(value.shape[1] // LANES):
        ref[pl.ds(first_row + c, n_tokens, stride=SUBLANES), :] = value[:, c * LANES:(c + 1) * LANES]


def _tile_load(ref, first_row, n_tokens, width, pitch=SUBLANES):
    return jnp.concatenate([ref[pl.ds(first_row + c, n_tokens, stride=pitch), :]
                            for c in range(width // LANES)], axis=1)


def _token_rows(ref, token):
    start = token * SUBLANES
    if not isinstance(start, int):
        start = pl.multiple_of(start, SUBLANES)
    return ref.at[pl.ds(start, SUBLANES), :]


def _token_copy(src, src_token, dst, dst_token, sem):
    return pltpu.make_async_copy(_token_rows(src, src_token), _token_rows(dst, dst_token), sem)


def _moe_kernel(blk_exp, q0s, nvalids, nused, order_hbm, xn_hbm, wg_ref, wu_ref, wd_ref, out_hbm,
                ibuf, xbuf, ybuf, wg_bf, wu_bf, wd_bf, isem, gsem, ssem):
    i = pl.program_id(0)
    nu = nused[0]
    m = MOE_BLOCK
    d = wd_bf.shape[1]

    window = 2 * IDX_CHUNK

    def idx_copy(blk):
        base = pl.multiple_of(q0s[blk] & ~(IDX_CHUNK - 1), IDX_CHUNK)
        slot = blk & (IDX_RING - 1)
        dst = ibuf.at[pl.ds(pl.multiple_of(slot * window, window), window)]
        return pltpu.make_async_copy(order_hbm.at[pl.ds(base, window)], dst, isem.at[slot])

    def pair_base(blk):
        return (blk & (IDX_RING - 1)) * window + (q0s[blk] & (IDX_CHUNK - 1))

    def gather_group(blk_base, dst, sem, g):
        for u in range(DMA_UNROLL):
            r = g * DMA_UNROLL + u
            pair = ibuf[blk_base + r]
            first_row = pl.multiple_of((pair & ~1) * (SUBLANES // 2), SUBLANES)
            pltpu.make_async_copy(xn_hbm.at[pl.ds(first_row, SUBLANES), :], _token_rows(dst, r), sem).start()

    def scatter_group(blk_base, src, sem, g):
        for u in range(DMA_UNROLL):
            r = g * DMA_UNROLL + u
            _token_copy(src, r, out_hbm, ibuf[blk_base + r], sem).start()

    def issue_gathers(blk, unrolled=False):
        slot = blk & 1
        args = (pair_base(blk), xbuf.at[slot], gsem.at[slot])
        if unrolled:
            for g in range(m // DMA_UNROLL):
                gather_group(*args, g)
        else:
            lax.fori_loop(0, m // DMA_UNROLL, lambda g, c: (gather_group(*args, g), c)[1], 0)

    def wait_gathers(blk):
        slot = blk & 1
        pltpu.make_async_copy(xn_hbm.at[pl.ds(0, m * SUBLANES), :], xbuf.at[slot], gsem.at[slot]).wait()

    def issue_scatters(blk, unrolled=False):
        slot = blk & (OUT_RING - 1)
        base = pair_base(blk)
        src = ybuf.at[slot]
        sem = ssem.at[slot]
        if unrolled:
            for g in range(m // DMA_UNROLL):
                scatter_group(base, src, sem, g)
            return
        nv = nvalids[blk]
        groups = lax.shift_right_logical(nv, DMA_UNROLL.bit_length() - 1)

        def tail(r, c):
            _token_copy(src, r, out_hbm, ibuf[base + r], sem).start()
            return c

        lax.fori_loop(0, groups, lambda g, c: (scatter_group(base, src, sem, g), c)[1], 0)
        lax.fori_loop(groups * DMA_UNROLL, nv, tail, 0)

    def wait_scatters(blk):
        slot = blk & (OUT_RING - 1)
        nv = nvalids[blk]
        rows = pl.multiple_of(nv * SUBLANES, SUBLANES)

        @pl.when(nv > 0)
        def _():
            pltpu.make_async_copy(ybuf.at[slot].at[pl.ds(0, rows), :], out_hbm.at[pl.ds(0, rows), :],
                                  ssem.at[slot]).wait()

    @pl.when(i == 0)
    def _():
        first = idx_copy(0)
        first.start()
        first.wait()
        issue_gathers(0)

        @pl.when(nu > 1)
        def _():
            idx_copy(1).start()

    def expert():
        x = _tile_load(xbuf.at[i & 1], 0, m, d).astype(BF16)
        gate = jnp.dot(x, wg_bf[...], preferred_element_type=F32)
        up = jnp.dot(x, wu_bf[...], preferred_element_type=F32)
        hidden = (gate * (1.0 / (1.0 + jnp.exp(-gate))) * up).astype(BF16)
        _tile_store(ybuf.at[i & (OUT_RING - 1)], 0, m, jnp.dot(hidden, wd_bf[...], preferred_element_type=F32))

    @pl.when(i < nu)
    def _():
        @pl.when(i + 2 < nu)
        def _():
            idx_copy(i + 2).start()

        @pl.when(i + 1 < nu)
        def _():
            idx_copy(i + 1).wait()

        wait_gathers(i)

        @pl.when((i == 0) | (blk_exp[i] != blk_exp[jnp.maximum(i - 1, 0)]))
        def _():
            wg_bf[...] = wg_ref[0, 0].astype(BF16)
            wu_bf[...] = wu_ref[0, 0].astype(BF16)
            wd_bf[...] = wd_ref[0, 0].astype(BF16)

        @pl.when(i >= OUT_RING - 1)
        def _():
            wait_scatters(i - (OUT_RING - 1))

        prev = jnp.maximum(i - 1, 0)
        steady = (i >= 1) & (i + 1 < nu) & (nvalids[prev] == m)

        @pl.when(steady)
        def _():
            issue_gathers(i + 1, unrolled=True)
            issue_scatters(i - 1, unrolled=True)
            expert()

        @pl.when(jnp.logical_not(steady))
        def _():
            @pl.when(i + 1 < nu)
            def _():
                issue_gathers(i + 1)

            @pl.when(i >= 1)
            def _():
                issue_scatters(i - 1)

            expert()

        @pl.when(i == nu - 1)
        def _():
            issue_scatters(i)
            for back in range(OUT_RING - 2, -1, -1):
                @pl.when(i >= back)
                def _():
                    wait_scatters(i - back)


def moe_experts(order, plan, xn, w_gate, w_up, w_down, layer):
    blk_exp, q0s, nvalids, nused = plan
    d, de = w_gate.shape[2], w_gate.shape[3]
    assert d == SUBLANES * LANES
    rows = xn.shape[0]
    m = MOE_BLOCK
    w_in_spec = pl.BlockSpec((1, 1, d, de), lambda i, be, q0, nv, nu: (layer, be[i], 0, 0))
    return pl.pallas_call(
        _moe_kernel,
        out_shape=jax.ShapeDtypeStruct((2 * rows, LANES), F32),
        grid_spec=pltpu.PrefetchScalarGridSpec(
            num_scalar_prefetch=4,
            grid=(blk_exp.shape[0],),
            in_specs=[
                pl.BlockSpec(memory_space=pl.ANY), pl.BlockSpec(memory_space=pl.ANY),
                w_in_spec, w_in_spec,
                pl.BlockSpec((1, 1, de, d), lambda i, be, q0, nv, nu: (layer, be[i], 0, 0)),
            ],
            out_specs=pl.BlockSpec(memory_space=pl.ANY),
            scratch_shapes=[
                pltpu.SMEM((IDX_RING * 2 * IDX_CHUNK,), I32),
                pltpu.VMEM((2, m * SUBLANES, LANES), F32), pltpu.VMEM((OUT_RING, m * SUBLANES, LANES), F32),
                pltpu.VMEM((d, de), BF16), pltpu.VMEM((d, de), BF16), pltpu.VMEM((de, d), BF16),
                pltpu.SemaphoreType.DMA((IDX_RING,)), pltpu.SemaphoreType.DMA((2,)),
                pltpu.SemaphoreType.DMA((OUT_RING,)),
            ],
        ),
        compiler_params=_cparams("arbitrary"),
        name="moe_experts",
    )(blk_exp, q0s, nvalids, nused, order, xn, w_gate, w_up, w_down)


def _gated_sum(y_ref, gate_ref, h):
    tm, d = h.shape
    gate = gate_ref[...]
    y0 = _tile_load(y_ref, 0, tm, d, pitch=2 * SUBLANES)
    y1 = _tile_load(y_ref, SUBLANES, tm, d, pitch=2 * SUBLANES)
    return h + (y0 * gate[:, 0:1] + y1 * gate[:, 1:2])


def _moe_finish_kernel(y_ref, gate_ref, h_ref, g_ref, o_ref):
    out = _gated_sum(y_ref, gate_ref, h_ref[...])
    ms = jnp.mean(out * out, axis=-1, keepdims=True)
    o_ref[...] = out * lax.rsqrt(ms + EPS) * g_ref[...]


def moe_finish(ys, gate, h, g):
    n, d = h.shape
    tm = TOK_TM
    row_blk = lambda width: pl.BlockSpec((tm, width), lambda i: (i, 0))
    return pl.pallas_call(
        _moe_finish_kernel,
        out_shape=jax.ShapeDtypeStruct((n, d), F32),
        grid=(n // tm,),
        in_specs=[pl.BlockSpec((2 * tm * SUBLANES, LANES), lambda i: (i, 0)),
                  row_blk(LANES), row_blk(d), pl.BlockSpec((1, d), lambda i: (0, 0))],
        out_specs=row_blk(d),
        compiler_params=_cparams("arbitrary"),
        name="moe_finish",
    )(ys, gate, h, g.reshape(1, d))


def dispatch_plan(route, counts):
    m = MOE_BLOCK
    n = route.shape[0]
    assert (2 * n) % IDX_CHUNK == 0
    counts = counts[0, :N_EXPERTS]
    padded = ((counts + m - 1) // m) * m
    pend = jnp.cumsum(padded)
    pstart = pend - padded
    start = jnp.cumsum(counts) - counts
    eid = route[:, ROUTE_EID:ROUTE_EID + 2].reshape(-1)
    order = jnp.argsort(eid, stable=True).astype(I32)
    order = jnp.concatenate([order, jnp.zeros((2 * IDX_CHUNK,), I32)])
    nblk = (2 * n + N_EXPERTS * m) // m
    blk_start = jnp.arange(nblk, dtype=I32) * m
    nused = pend[-1] // m
    used = jnp.arange(nblk) < nused
    e = jnp.minimum(jnp.sum(blk_start[:, None] >= pend[None, :], axis=1), N_EXPERTS - 1)
    is_e = e[:, None] == jnp.arange(N_EXPERTS)[None, :]
    of_block = lambda table: jnp.sum(jnp.where(is_e, table[None, :], 0), axis=1)
    r0 = blk_start - of_block(pstart)
    q0s = jnp.where(used, of_block(start) + r0, 0)
    nvalids = jnp.where(used, jnp.clip(of_block(counts) - r0, 0, m), 0)
    last_exp = jnp.sum(jnp.where(jnp.arange(nblk) == nused - 1, e, 0))
    blk_exp = jnp.where(used, e, last_exp)
    as_i32 = lambda t: t.astype(I32)
    return order, (as_i32(blk_exp), as_i32(q0s), as_i32(nvalids), as_i32(nused).reshape(1))


def kernel(x, attn_norm_g, ffn_norm_g, w_in_ab, w_out_ab, ret_gn_g, w_in_c, b_in_c, sinks, w_out_c, b_out_c,
           w_group, w_expert_router, w_gate, w_up, w_down, final_norm_g):
    batch, seq, d = x.shape
    n = batch * seq
    depth = attn_norm_g.shape[0]
    cos, sin = rope_lane_tables(seq)
    later = (jnp.arange(min(SB_TILE, seq))[:, None] > jnp.arange(min(SB_TILE, seq))[None, :]).astype(BF16)
    w_sb = N_HEADS_SB * HEAD_DIM
    w_ret = N_HEADS_RET * HEAD_DIM
    h = x.reshape(n, d)
    pending = None
    for layer in range(depth):
        i = layer // 2
        if layer % 2 == 0:
            proj, h = norm_proj(h, attn_norm_g[layer], w_in_ab[i].astype(BF16),
                                jnp.zeros((w_in_ab.shape[2],), F32), cos, sin, seq=seq, tn=w_sb,
                                col_ops=("scale", "", "", "rope", "rope scale", "", ""), moe=pending)
            per = w_sb // LANES
            a = sb_attention(proj, later, batch=batch, seq=seq, q_blk=0, k_blk=per, v_blk=2 * per)
            r = retention(proj, ret_gn_g[i], batch=batch, seq=seq,
                          q_blk=3 * per, k_blk=4 * per, v_blk=5 * per, g_blk=6 * per)
            w_out = w_out_ab[i].astype(BF16)
            lhs, ws = [a, r], [w_out[:w_sb], w_out[w_sb:]]
            bias = jnp.zeros((d,), F32)
        else:
            wkv = N_KV_SWA * HEAD_DIM
            nq = N_HEADS_SWA * HEAD_DIM // wkv
            proj, h = norm_proj(h, attn_norm_g[layer], w_in_c[i].astype(BF16), b_in_c[i].astype(F32),
                                cos, sin, seq=seq, tn=wkv,
                                col_ops=("rope scale",) * nq + ("rope", ""), moe=pending)
            o = swa_attention(proj, sinks[i], batch=batch, seq=seq)
            lhs, ws = [o], [w_out_c[i].astype(BF16)]
            bias = b_out_c[i].astype(F32)
        h, xn, route, gate, counts = outproj_router(
            lhs, ws, bias, h, ffn_norm_g[layer], router_weights(w_group[layer], w_expert_router[layer]))
        order, plan = dispatch_plan(route, counts)
        pending = (moe_experts(order, plan, xn, w_gate, w_up, w_down, layer), gate)
    return moe_finish(*pending, h, final_norm_g).reshape(batch, seq, d)
```

```python
import functools

import jax
import jax.numpy as jnp
import numpy as np
from jax import lax
from jax.experimental import pallas as pl
from jax.experimental.pallas import tpu as pltpu

F32 = jnp.float32
BF16 = jnp.bfloat16
I32 = jnp.int32

HEAD_DIM = 64
N_HEADS_SB = 8
N_HEADS_RET = 8
N_HEADS_SWA = 16
N_KV_SWA = 4
WINDOW = 128
ROPE_THETA = 10000.0
N_GROUPS = 4
EXPERTS_PER_GROUP = 8
N_EXPERTS = N_GROUPS * EXPERTS_PER_GROUP
MOE_BLOCK = 256
EPS = 1e-6

LANES = 128
SUBLANES = 8
HALF = HEAD_DIM // 2
QK_SCALE = HEAD_DIM ** -0.5

SB_TILE = 256
RET_CHUNK = 256
PROJ_TM = 512
TOK_TM = 512
ROUTER_TM = 1024
ROUTER_SUB = 256
VMEM_LIMIT = 48 * 1024 * 1024

_NT = (((1,), (1,)), ((), ()))
_TN = (((0,), (0,)), ((), ()))


def _cparams(*sem):
    return pltpu.CompilerParams(dimension_semantics=sem, vmem_limit_bytes=VMEM_LIMIT)


def _head0_mask():
    return lax.broadcasted_iota(I32, (1, LANES), 1) < HEAD_DIM


def _tile_store(ref, first_row, n_tokens, value):
    for c in range(value.shape[1] // LANES):
        ref[pl.ds(first_row + c, n_tokens, stride=SUBLANES), :] = value[:, c * LANES:(c + 1) * LANES]


def _tile_load(ref, first_row, n_tokens, width, pitch=SUBLANES):
    return jnp.concatenate([ref[pl.ds(first_row + c, n_tokens, stride=pitch), :]
                            for c in range(width // LANES)], axis=1)


def _norm_proj_kernel(*refs, tn, col_ops, combine):
    if combine:
        y_ref, gate_ref, x_ref, g_ref, w_ref, b_ref, cos_ref, sin_ref, h_ref, o_ref = refs
        x = _gated_sum(y_ref, gate_ref, x_ref[...])
        h_ref[...] = x
    else:
        x_ref, g_ref, w_ref, b_ref, cos_ref, sin_ref, o_ref = refs
        x = x_ref[...]
    ms = jnp.mean(x * x, axis=-1, keepdims=True)
    xn = (x * lax.rsqrt(ms + EPS) * g_ref[...]).astype(BF16)
    lane = lax.broadcasted_iota(I32, (1, LANES), 1)
    first_half = (lane % HEAD_DIM) < HALF
    for j, op in enumerate(col_ops):
        cols = slice(j * tn, (j + 1) * tn)
        acc = jnp.dot(xn, w_ref[:, cols], preferred_element_type=F32) + b_ref[:, cols]
        if "rope" in op:
            cos = cos_ref[...]
            sin = sin_ref[...]
            slabs = []
            for s in range(tn // LANES):
                a = acc[:, s * LANES:(s + 1) * LANES]
                partner = jnp.where(first_half, pltpu.roll(a, LANES - HALF, 1), pltpu.roll(a, HALF, 1))
                slabs.append(a * cos + partner * sin)
            acc = jnp.concatenate(slabs, axis=1)
        if "scale" in op:
            acc = acc * QK_SCALE
        o_ref[:, cols] = acc.astype(BF16)


def norm_proj(x, g, w, b, cos, sin, *, seq, tn, col_ops, moe=None):
    n, d = x.shape
    f = w.shape[1]
    tm = min(PROJ_TM, seq)
    assert n % tm == 0 and seq % tm == 0 and f == tn * len(col_ops)
    pos_blocks = seq // tm
    row_blk = lambda width: pl.BlockSpec((tm, width), lambda i: (i, 0))
    in_specs = [
        row_blk(d),
        pl.BlockSpec((1, d), lambda i: (0, 0)),
        pl.BlockSpec((d, f), lambda i: (0, 0)),
        pl.BlockSpec((1, f), lambda i: (0, 0)),
        pl.BlockSpec((tm, LANES), lambda i: (i % pos_blocks, 0)),
        pl.BlockSpec((tm, LANES), lambda i: (i % pos_blocks, 0)),
    ]
    args = (x, g.reshape(1, d), w, b.reshape(1, f), cos, sin)
    proj_shape, proj_spec = jax.ShapeDtypeStruct((n, f), BF16), row_blk(f)
    call = functools.partial(
        pl.pallas_call, functools.partial(_norm_proj_kernel, tn=tn, col_ops=col_ops, combine=moe is not None),
        grid=(n // tm,), compiler_params=_cparams("arbitrary"), name="norm_proj")
    if moe is None:
        return call(out_shape=proj_shape, in_specs=in_specs, out_specs=proj_spec)(*args), x
    ys, gate = moe
    pair_tiles = pl.BlockSpec((2 * tm * SUBLANES, LANES), lambda i: (i, 0))
    h, proj = call(out_shape=(jax.ShapeDtypeStruct((n, d), F32), proj_shape),
                   in_specs=[pair_tiles, row_blk(LANES)] + in_specs,
                   out_specs=(row_blk(d), proj_spec))(ys, gate, *args)
    return proj, h


def rope_lane_tables(seq):
    pos = jnp.arange(seq, dtype=F32)
    inv = ROPE_THETA ** (-jnp.arange(0, HEAD_DIM, 2, dtype=F32) / HEAD_DIM)
    ang = pos[:, None] * inv[None, :]
    cos, sin = jnp.cos(ang), jnp.sin(ang)
    return jnp.tile(cos, (1, 4)), jnp.tile(jnp.concatenate([-sin, sin], axis=1), (1, 2))


SB_PAIRS = 4


def _sb_kernel(q_ref, k_ref, v_ref, t_ref, o_ref, acc_ref, carry_ref, *, tile):
    qi = pl.program_id(2)
    m0 = _head0_mask()
    zero = jnp.zeros((tile, LANES), BF16)
    row = lax.broadcasted_iota(I32, (tile, tile), 0)
    col = lax.broadcasted_iota(I32, (tile, tile), 1)
    strict = col < row

    def key_tile(kb, diag):
        start = pl.multiple_of(kb * tile, tile)
        for p in range(SB_PAIRS):
            lanes = slice(p * LANES, (p + 1) * LANES)
            q = q_ref[:, lanes]
            k = k_ref[pl.ds(start, tile), lanes]
            v = v_ref[pl.ds(start, tile), lanes]
            out = None
            for h in range(2):
                qh = jnp.where(m0, q, zero) if h == 0 else jnp.where(m0, zero, q)
                vh = jnp.where(m0, v, zero) if h == 0 else jnp.where(m0, zero, v)
                z = lax.dot_general(qh, k, _NT, preferred_element_type=F32)
                log_fail = -(jnp.maximum(z, 0.0) + jnp.log(1.0 + jnp.exp(-jnp.abs(z))))
                if diag:
                    log_fail = jnp.where(strict, log_fail, 0.0)
                after = jnp.dot(log_fail.astype(BF16), t_ref[...], preferred_element_type=F32)
                logw = z + log_fail + after
                if not diag:
                    logw = logw + carry_ref[2 * p + h]
                w = jnp.exp(logw)
                if diag:
                    w = jnp.where(strict, w, 0.0)
                pv = jnp.dot(w.astype(BF16), vh, preferred_element_type=F32)
                out = pv if out is None else out + pv
                tile_sum = jnp.sum(log_fail, axis=-1, keepdims=True)
                if diag:
                    carry_ref[2 * p + h] = tile_sum
                else:
                    carry_ref[2 * p + h] += tile_sum
            if diag:
                acc_ref[:, lanes] = out
            else:
                acc_ref[:, lanes] += out

    key_tile(qi, True)

    def body(j, c):
        key_tile(qi - 1 - j, False)
        return c

    lax.fori_loop(0, qi, body, 0)
    o_ref[...] = acc_ref[...].astype(BF16)


def sb_attention(proj, later_mat, *, batch, seq, q_blk, k_blk, v_blk):
    n = proj.shape[0]
    tile = min(SB_TILE, seq)
    nq = seq // tile
    width = SB_PAIRS * LANES
    steps = N_HEADS_SB * HEAD_DIM // width
    assert q_blk % SB_PAIRS == 0 and k_blk % SB_PAIRS == 0 and v_blk % SB_PAIRS == 0
    col = lambda blk: blk // SB_PAIRS
    return pl.pallas_call(
        functools.partial(_sb_kernel, tile=tile),
        out_shape=jax.ShapeDtypeStruct((n, steps * width), BF16),
        grid=(batch, steps, nq),
        in_specs=[
            pl.BlockSpec((tile, width), lambda b, p, i: (b * nq + i, col(q_blk) + p)),
            pl.BlockSpec((seq, width), lambda b, p, i: (b, col(k_blk) + p)),
            pl.BlockSpec((seq, width), lambda b, p, i: (b, col(v_blk) + p)),
            pl.BlockSpec((tile, tile), lambda b, p, i: (0, 0)),
        ],
        out_specs=pl.BlockSpec((tile, width), lambda b, p, i: (b * nq + i, p)),
        scratch_shapes=[pltpu.VMEM((tile, width), F32), pltpu.VMEM((2 * SB_PAIRS, tile, 1), F32)],
        compiler_params=_cparams("arbitrary", "arbitrary", "arbitrary"),
        name="sb_attention",
    )(proj, proj, proj, later_mat)


RET_PAIRS = 4
RET_STEP_CHUNKS = 2


def _retention_kernel(q_ref, k_ref, v_ref, gate_ref, gain_ref, dec_ref, xi_ref, zeta_ref, gch_ref,
                      o_ref, state_ref):
    c = pl.program_id(2)

    @pl.when(c == 0)
    def _():
        state_ref[...] = jnp.zeros_like(state_ref)

    m0 = _head0_mask()
    chunk = dec_ref.shape[1]
    zero = jnp.zeros((chunk, LANES), BF16)
    r = lax.broadcasted_iota(I32, (LANES, LANES), 0) < HEAD_DIM
    cc = lax.broadcasted_iota(I32, (LANES, LANES), 1) < HEAD_DIM
    same_head = r == cc

    def head_mean(t):
        s0 = jnp.sum(jnp.where(m0, t, 0.0), axis=-1, keepdims=True)
        s1 = jnp.sum(jnp.where(m0, 0.0, t), axis=-1, keepdims=True)
        return jnp.where(m0, s0, s1) * (1.0 / HEAD_DIM)

    for p in range(RET_PAIRS):
        lanes = slice(p * LANES, (p + 1) * LANES)
        state = state_ref[p]
        for sub in range(q_ref.shape[0] // chunk):
            rows = slice(sub * chunk, (sub + 1) * chunk)
            q = q_ref[rows, lanes]
            k = k_ref[rows, lanes]
            v = v_ref[rows, lanes]
            y = jnp.dot(q, state.astype(BF16), preferred_element_type=F32) * xi_ref[:, lanes]
            for h in range(2):
                qh = jnp.where(m0, q, zero) if h == 0 else jnp.where(m0, zero, q)
                vh = jnp.where(m0, v, zero) if h == 0 else jnp.where(m0, zero, v)
                s = lax.dot_general(qh, k, _NT, preferred_element_type=F32)
                inner = (s * dec_ref[2 * p + h]).astype(BF16)
                y = y + jnp.dot(inner, vh, preferred_element_type=F32)

            kz = (k.astype(F32) * zeta_ref[:, lanes]).astype(BF16)
            upd = lax.dot_general(kz, v, _TN, preferred_element_type=F32)
            state = state * gch_ref[:, lanes] + jnp.where(same_head, upd, 0.0)

            d = y - head_mean(y)
            yn = d * lax.rsqrt(head_mean(d * d) + EPS) * gain_ref[:, lanes]
            g = gate_ref[rows, lanes].astype(F32)
            o_ref[rows, lanes] = (yn * (g * (1.0 / (1.0 + jnp.exp(-g))))).astype(BF16)
        state_ref[p] = state


def retention_tables(chunk):
    h = N_HEADS_RET
    log_g = jnp.log(1.0 - 2.0 ** (-5.0 - jnp.arange(h, dtype=F32)))
    idx = jnp.arange(chunk, dtype=F32)
    diff = idx[:, None] - idx[None, :]
    dec = jnp.where(diff[None] >= 0, jnp.exp(log_g[:, None, None] * jnp.maximum(diff, 0.0)[None]), 0.0)
    xi = jnp.exp(log_g[:, None] * (idx[None, :] + 1.0))
    zeta = jnp.exp(log_g[:, None] * (chunk - 1.0 - idx[None, :]))
    gch = jnp.exp(log_g * chunk)
    lanes = lambda t: jnp.repeat(t.T, HEAD_DIM, axis=1)
    return dec, lanes(xi), lanes(zeta), jnp.repeat(gch, HEAD_DIM)[None, :]


def retention(proj, gain, *, batch, seq, q_blk, k_blk, v_blk, g_blk):
    n = proj.shape[0]
    chunk = min(RET_CHUNK, seq)
    rows = min(RET_STEP_CHUNKS * chunk, seq)
    nc = seq // rows
    width = RET_PAIRS * LANES
    steps = N_HEADS_RET * HEAD_DIM // width
    assert all(b % RET_PAIRS == 0 for b in (q_blk, k_blk, v_blk, g_blk))
    dec, xi, zeta, gch = retention_tables(chunk)
    blk = lambda off: pl.BlockSpec((rows, width), lambda b, p, c: (b * nc + c, off // RET_PAIRS + p))
    return pl.pallas_call(
        _retention_kernel,
        out_shape=jax.ShapeDtypeStruct((n, steps * width), BF16),
        grid=(batch, steps, nc),
        in_specs=[
            blk(q_blk), blk(k_blk), blk(v_blk), blk(g_blk),
            pl.BlockSpec((1, width), lambda b, p, c: (0, p)),
            pl.BlockSpec((2 * RET_PAIRS, chunk, chunk), lambda b, p, c: (p, 0, 0)),
            pl.BlockSpec((chunk, width), lambda b, p, c: (0, p)),
            pl.BlockSpec((chunk, width), lambda b, p, c: (0, p)),
            pl.BlockSpec((1, width), lambda b, p, c: (0, p)),
        ],
        out_specs=pl.BlockSpec((rows, width), lambda b, p, c: (b * nc + c, p)),
        scratch_shapes=[pltpu.VMEM((RET_PAIRS, LANES, LANES), F32)],
        compiler_params=_cparams("arbitrary", "arbitrary", "arbitrary"),
        name="retention",
    )(proj, proj, proj, proj, gain.reshape(1, -1).astype(F32), dec, xi, zeta, gch)


SWA_BLOCKS = 8


def _swa_kernel(sink_ref, q_ref, kp_ref, kc_ref, vp_ref, vc_ref, o_ref):
    w = WINDOW
    qi = lax.broadcasted_iota(I32, (w, 2 * w), 0)
    kj = lax.broadcasted_iota(I32, (w, 2 * w), 1)
    rel = qi + w - kj
    in_window = (rel >= 0) & (rel < w)
    keys = jnp.concatenate([kp_ref[...], kc_ref[...]], axis=0)
    vals = jnp.concatenate([vp_ref[...], vc_ref[...]], axis=0)
    for sub in range(q_ref.shape[0] // w):
        first = (pl.program_id(1) == 0) if sub == 0 else False
        valid = in_window & (jnp.logical_not(first) | (kj >= w))
        _swa_block(sink_ref, q_ref, o_ref, slice(sub * w, (sub + 1) * w),
                   keys[sub * w:(sub + 2) * w], vals[sub * w:(sub + 2) * w], valid)


def _swa_block(sink_ref, q_ref, o_ref, rows, kband, vband, valid):
    m0 = _head0_mask()
    w = WINDOW
    group = N_HEADS_SWA // N_KV_SWA
    slabs = group * HEAD_DIM // LANES
    swap = lambda t: jnp.concatenate([t[:, HEAD_DIM:], t[:, :HEAD_DIM]], axis=1)
    zero_kv = jnp.zeros((2 * w, LANES), BF16)
    zero_q = jnp.zeros((slabs * w, LANES), BF16)
    ones_kv = jnp.ones((2 * w, LANES), BF16)

    def half_heads(q_stack, k_half, v_half, first_head):
        s = lax.dot_general(q_stack, k_half, _NT, preferred_element_type=F32)
        probs, sink_terms = [], []
        for slab in range(slabs):
            sink = sink_ref[first_head + 2 * slab]
            sh = jnp.where(valid, s[slab * w:(slab + 1) * w], -jnp.inf)
            m = jnp.maximum(jnp.max(sh, axis=-1, keepdims=True), sink)
            probs.append(jnp.exp(sh - m).astype(BF16))
            sink_terms.append(jnp.exp(sink - jnp.broadcast_to(m, (w, LANES))))
        p = jnp.concatenate(probs, axis=0)
        pv = jnp.dot(p, jnp.concatenate([v_half, ones_kv], axis=1), preferred_element_type=F32)
        denom = pv[:, LANES:] + jnp.concatenate(sink_terms, axis=0)
        return pv[:, :LANES] / denom

    for pair in range(N_KV_SWA * HEAD_DIM // LANES):
        k2 = kband[:, pair * LANES:(pair + 1) * LANES]
        v2 = vband[:, pair * LANES:(pair + 1) * LANES]
        k2s, v2s = swap(k2), swap(v2)
        for c in range(2):
            kvh = 2 * pair + c
            k_even, k_odd = (k2, k2s) if c == 0 else (k2s, k2)
            v_even, v_odd = (v2, v2s) if c == 0 else (v2s, v2)
            base = kvh * group * HEAD_DIM
            q_stack = jnp.concatenate([q_ref[rows, base + t * LANES:base + (t + 1) * LANES] for t in range(slabs)],
                                      axis=0)
            out = (half_heads(jnp.where(m0, q_stack, zero_q), k_even, jnp.where(m0, v_even, zero_kv), kvh * group)
                   + half_heads(jnp.where(m0, zero_q, q_stack), k_odd, jnp.where(m0, zero_kv, v_odd),
                                kvh * group + 1))
            for t in range(slabs):
                o_ref[rows, base + t * LANES:base + (t + 1) * LANES] = out[t * w:(t + 1) * w].astype(BF16)


def swa_attention(proj, sinks, *, batch, seq):
    n = proj.shape[0]
    w = WINDOW
    nb = seq // w
    wq = N_HEADS_SWA * HEAD_DIM
    wkv = N_KV_SWA * HEAD_DIM
    k_blk = wq // wkv
    blocks = min(SWA_BLOCKS, nb)
    assert nb % blocks == 0
    steps = nb // blocks
    rows = blocks * w
    cur = lambda off: pl.BlockSpec((rows, wkv), lambda b, i: (b * steps + i, off))
    prev = lambda off: pl.BlockSpec((w, wkv), lambda b, i: (b * nb + jnp.maximum(blocks * i - 1, 0), off))
    return pl.pallas_call(
        _swa_kernel,
        out_shape=jax.ShapeDtypeStruct((n, wq), BF16),
        grid=(batch, steps),
        in_specs=[
            pl.BlockSpec(memory_space=pltpu.SMEM),
            pl.BlockSpec((rows, wq), lambda b, i: (b * steps + i, 0)),
            prev(k_blk), cur(k_blk), prev(k_blk + 1), cur(k_blk + 1),
        ],
        out_specs=pl.BlockSpec((rows, wq), lambda b, i: (b * steps + i, 0)),
        compiler_params=_cparams("arbitrary", "arbitrary"),
        name="swa_attention",
    )(sinks.astype(F32), proj, proj, proj, proj, proj)


ROUTE_EID = 0
GROUP_LANE = N_EXPERTS


def _split_bf16(t):
    hi = t.astype(BF16)
    return hi, (t - hi.astype(F32)).astype(BF16)


def _outproj_router_kernel(*refs, n_lhs):
    lhs = refs[:n_lhs]
    ws = refs[n_lhs:2 * n_lhs]
    b_ref, h_ref, g_ref, wr_ref = refs[2 * n_lhs:2 * n_lhs + 4]
    h1_ref, xn_ref, route_ref, gate_ref, cnt_ref, carry_ref = refs[2 * n_lhs + 4:]
    step = pl.program_id(0)

    @pl.when(step == 0)
    def _():
        carry_ref[...] = jnp.zeros_like(carry_ref)

    w_split = jnp.concatenate(_split_bf16(wr_ref[...]), axis=1)
    for sub in range(h_ref.shape[0] // ROUTER_SUB):
        rows = slice(sub * ROUTER_SUB, (sub + 1) * ROUTER_SUB)
        mix = b_ref[...]
        for a_ref, w_ref in zip(lhs, ws):
            mix = mix + jnp.dot(a_ref[rows], w_ref[...], preferred_element_type=F32)
        h1 = h_ref[rows] + mix
        h1_ref[rows] = h1
        ms = jnp.mean(h1 * h1, axis=-1, keepdims=True)
        xn = h1 * lax.rsqrt(ms + EPS) * g_ref[...]
        _tile_store(xn_ref, sub * ROUTER_SUB * SUBLANES, ROUTER_SUB, xn)
        x_hi, x_lo = _split_bf16(xn)
        parts = (jnp.dot(x_hi, w_split, preferred_element_type=F32)
                 + jnp.dot(x_lo, w_split, preferred_element_type=F32))
        logits = parts[:, :LANES] + parts[:, LANES:]
        route, gate, cnt = _route(logits)
        route_ref[rows] = route
        gate_ref[rows] = gate
        carry_ref[...] += cnt
    cnt_ref[...] = jnp.broadcast_to(carry_ref[...], cnt_ref.shape).astype(I32)


def _route(logits):
    tm = logits.shape[0]
    lane = lax.broadcasted_iota(I32, (tm, LANES), 1)
    lane_f = lane.astype(F32)
    neg = -jnp.inf

    def lane_max(t):
        return jnp.max(t, axis=-1, keepdims=True)

    def lane_sum(t):
        return jnp.sum(t, axis=-1, keepdims=True)

    def first_lane_of(t, value, mask):
        return jnp.min(jnp.where(mask & (t == value), lane_f, float(LANES)), axis=-1, keepdims=True)

    is_group = (lane >= GROUP_LANE) & (lane < GROUP_LANE + N_GROUPS)
    gl = jnp.where(is_group, logits, neg)
    g_max = lane_max(gl)
    g_prob = 1.0 / lane_sum(jnp.exp(gl - g_max))
    g_idx = first_lane_of(gl, g_max, is_group) - float(GROUP_LANE)
    group_of_lane = lax.shift_right_logical(lane, int(np.log2(EXPERTS_PER_GROUP))).astype(F32)
    in_group = (lane < N_EXPERTS) & (group_of_lane == g_idx)
    el = jnp.where(in_group, logits, neg)
    l1 = lane_max(el)
    i1 = first_lane_of(el, l1, in_group)
    rest = in_group & (lane_f != i1)
    el2 = jnp.where(rest, el, neg)
    l2 = lane_max(el2)
    i2 = first_lane_of(el2, l2, rest)
    e2 = jnp.exp(l2 - l1)
    gate1 = g_prob / (1.0 + e2)
    gate2 = g_prob * (e2 / (1.0 + e2))

    cnt = jnp.where((lane_f == i1) | (lane_f == i2), 1.0, 0.0)
    route = jnp.where(lane == ROUTE_EID, i1, 0.0)
    route = jnp.where(lane == ROUTE_EID + 1, i2, route)
    gate = jnp.where(lane == 0, gate1, jnp.where(lane == 1, gate2, 0.0))
    return route.astype(I32), gate, jnp.sum(cnt, axis=0, keepdims=True)


def outproj_router(lhs, ws, bias, h, g, w_router):
    n, d = h.shape
    tm = ROUTER_TM
    assert n % tm == 0
    row_blk = lambda width: pl.BlockSpec((tm, width), lambda i: (i, 0))
    full = lambda a: pl.BlockSpec(a.shape, lambda i: (0, 0))
    bias2, g2 = bias.reshape(1, d), g.reshape(1, d)
    args = [*lhs, *ws, bias2, h, g2, w_router]
    in_specs = ([row_blk(a.shape[1]) for a in lhs] + [full(w) for w in ws]
                + [full(bias2), row_blk(d), full(g2), full(w_router)])
    return pl.pallas_call(
        functools.partial(_outproj_router_kernel, n_lhs=len(lhs)),
        out_shape=(
            jax.ShapeDtypeStruct((n, d), F32), jax.ShapeDtypeStruct((n * d // LANES, LANES), F32),
            jax.ShapeDtypeStruct((n, LANES), I32), jax.ShapeDtypeStruct((n, LANES), F32),
            jax.ShapeDtypeStruct((8, LANES), I32),
        ),
        grid=(n // tm,),
        in_specs=in_specs,
        out_specs=(row_blk(d), pl.BlockSpec((tm * d // LANES, LANES), lambda i: (i, 0)),
                   row_blk(LANES), row_blk(LANES), pl.BlockSpec((8, LANES), lambda i: (0, 0))),
        scratch_shapes=[pltpu.VMEM((1, LANES), F32)],
        compiler_params=_cparams("arbitrary"),
        name="outproj_router",
    )(*args)


def router_weights(w_group, w_expert_router):
    d = w_group.shape[0]
    pad = jnp.zeros((d, LANES - N_EXPERTS - N_GROUPS), F32)
    return jnp.concatenate([w_expert_router.astype(F32), w_group.astype(F32), pad], axis=1)


IDX_CHUNK = 1024
IDX_RING = 4
OUT_RING = 4
DMA_UNROLL = 8


def _token_rows(ref, token):
    start = token * SUBLANES
    if not isinstance(start, int):
        start = pl.multiple_of(start, SUBLANES)
    return ref.at[pl.ds(start, SUBLANES), :]


def _token_copy(src, src_token, dst, dst_token, sem):
    return pltpu.make_async_copy(_token_rows(src, src_token), _token_rows(dst, dst_token), sem)


def _moe_kernel(blk_exp, q0s, nvalids, nused, order_hbm, xn_hbm, wg_ref, wu_ref, wd_ref, out_hbm,
                ibuf, xbuf, ybuf, wg_bf, wu_bf, wd_bf, isem, gsem, ssem):
    i = pl.program_id(0)
    nu = nused[0]
    m = MOE_BLOCK
    d = wd_bf.shape[1]

    window = 2 * IDX_CHUNK

    def idx_copy(blk):
        base = pl.multiple_of(q0s[blk] & ~(IDX_CHUNK - 1), IDX_CHUNK)
        slot = blk & (IDX_RING - 1)
        dst = ibuf.at[pl.ds(pl.multiple_of(slot * window, window), window)]
        return pltpu.make_async_copy(order_hbm.at[pl.ds(base, window)], dst, isem.at[slot])

    def pair_base(blk):
        return (blk & (IDX_RING - 1)) * window + (q0s[blk] & (IDX_CHUNK - 1))

    def gather_group(blk_base, dst, sem, g):
        for u in range(DMA_UNROLL):
            r = g * DMA_UNROLL + u
            pair = ibuf[blk_base + r]
            first_row = pl.multiple_of((pair & ~1) * (SUBLANES // 2), SUBLANES)
            pltpu.make_async_copy(xn_hbm.at[pl.ds(first_row, SUBLANES), :], _token_rows(dst, r), sem).start()

    def scatter_group(blk_base, src, sem, g):
        for u in range(DMA_UNROLL):
            r = g * DMA_UNROLL + u
            _token_copy(src, r, out_hbm, ibuf[blk_base + r], sem).start()

    def issue_gathers(blk, unrolled=False):
        slot = blk & 1
        args = (pair_base(blk), xbuf.at[slot], gsem.at[slot])
        if unrolled:
            for g in range(m // DMA_UNROLL):
                gather_group(*args, g)
        else:
            lax.fori_loop(0, m // DMA_UNROLL, lambda g, c: (gather_group(*args, g), c)[1], 0)

    def wait_gathers(blk):
        slot = blk & 1
        pltpu.make_async_copy(xn_hbm.at[pl.ds(0, m * SUBLANES), :], xbuf.at[slot], gsem.at[slot]).wait()

    def issue_scatters(blk, unrolled=False):
        slot = blk & (OUT_RING - 1)
        base = pair_base(blk)
        src = ybuf.at[slot]
        sem = ssem.at[slot]
        if unrolled:
            for g in range(m // DMA_UNROLL):
                scatter_group(base, src, sem, g)
            return
        nv = nvalids[blk]
        groups = lax.shift_right_logical(nv, DMA_UNROLL.bit_length() - 1)

        def tail(r, c):
            _token_copy(src, r, out_hbm, ibuf[base + r], sem).start()
            return c

        lax.fori_loop(0, groups, lambda g, c: (scatter_group(base, src, sem, g), c)[1], 0)
        lax.fori_loop(groups * DMA_UNROLL, nv, tail, 0)

    def wait_scatters(blk):
        slot = blk & (OUT_RING - 1)
        nv = nvalids[blk]
        rows = pl.multiple_of(nv * SUBLANES, SUBLANES)

        @pl.when(nv > 0)
        def _():
            pltpu.make_async_copy(ybuf.at[slot].at[pl.ds(0, rows), :], out_hbm.at[pl.ds(0, rows), :],
                                  ssem.at[slot]).wait()

    @pl.when(i == 0)
    def _():
        first = idx_copy(0)
        first.start()
        first.wait()
        issue_gathers(0)

        @pl.when(nu > 1)
        def _():
            idx_copy(1).start()

    def expert():
        x = _tile_load(xbuf.at[i & 1], 0, m, d).astype(BF16)
        gate = jnp.dot(x, wg_bf[...], preferred_element_type=F32)
        up = jnp.dot(x, wu_bf[...], preferred_element_type=F32)
        hidden = (gate * (1.0 / (1.0 + jnp.exp(-gate))) * up).astype(BF16)
        _tile_store(ybuf.at[i & (OUT_RING - 1)], 0, m, jnp.dot(hidden, wd_bf[...], preferred_element_type=F32))

    @pl.when(i < nu)
    def _():
        @pl.when(i + 2 < nu)
        def _():
            idx_copy(i + 2).start()

        @pl.when(i + 1 < nu)
        def _():
            idx_copy(i + 1).wait()

        wait_gathers(i)

        @pl.when((i == 0) | (blk_exp[i] != blk_exp[jnp.maximum(i - 1, 0)]))
        def _():
            wg_bf[...] = wg_ref[0, 0].astype(BF16)
            wu_bf[...] = wu_ref[0, 0].astype(BF16)
            wd_bf[...] = wd_ref[0, 0].astype(BF16)

        @pl.when(i >= OUT_RING - 1)
        def _():
            wait_scatters(i - (OUT_RING - 1))

        prev = jnp.maximum(i - 1, 0)
        steady = (i >= 1) & (i + 1 < nu) & (nvalids[prev] == m)

        @pl.when(steady)
        def _():
            issue_gathers(i + 1, unrolled=True)
            issue_scatters(i - 1, unrolled=True)
            expert()

        @pl.when(jnp.logical_not(steady))
        def _():
            @pl.when(i + 1 < nu)
            def _():
                issue_gathers(i + 1)

            @pl.when(i >= 1)
            def _():
                issue_scatters(i - 1)

            expert()

        @pl.when(i == nu - 1)
        def _():
            issue_scatters(i)
            for back in range(OUT_RING - 2, -1, -1):
                @pl.when(i >= back)
                def _():
                    wait_scatters(i - back)


def moe_experts(order, plan, xn, w_gate, w_up, w_down, layer):
    blk_exp, q0s, nvalids, nused = plan
    d, de = w_gate.shape[2], w_gate.shape[3]
    assert d == SUBLANES * LANES
    rows = xn.shape[0]
    m = MOE_BLOCK
    w_in_spec = pl.BlockSpec((1, 1, d, de), lambda i, be, q0, nv, nu: (layer, be[i], 0, 0))
    return pl.pallas_call(
        _moe_kernel,
        out_shape=jax.ShapeDtypeStruct((2 * rows, LANES), F32),
        grid_spec=pltpu.PrefetchScalarGridSpec(
            num_scalar_prefetch=4,
            grid=(blk_exp.shape[0],),
            in_specs=[
                pl.BlockSpec(memory_space=pl.ANY), pl.BlockSpec(memory_space=pl.ANY),
                w_in_spec, w_in_spec,
                pl.BlockSpec((1, 1, de, d), lambda i, be, q0, nv, nu: (layer, be[i], 0, 0)),
            ],
            out_specs=pl.BlockSpec(memory_space=pl.ANY),
            scratch_shapes=[
                pltpu.SMEM((IDX_RING * 2 * IDX_CHUNK,), I32),
                pltpu.VMEM((2, m * SUBLANES, LANES), F32), pltpu.VMEM((OUT_RING, m * SUBLANES, LANES), F32),
                pltpu.VMEM((d, de), BF16), pltpu.VMEM((d, de), BF16), pltpu.VMEM((de, d), BF16),
                pltpu.SemaphoreType.DMA((IDX_RING,)), pltpu.SemaphoreType.DMA((2,)),
                pltpu.SemaphoreType.DMA((OUT_RING,)),
            ],
        ),
        compiler_params=_cparams("arbitrary"),
        name="moe_experts",
    )(blk_exp, q0s, nvalids, nused, order, xn, w_gate, w_up, w_down)


def _gated_sum(y_ref, gate_ref, h):
    tm, d = h.shape
    gate = gate_ref[...]
    y0 = _tile_load(y_ref, 0, tm, d, pitch=2 * SUBLANES)
    y1 = _tile_load(y_ref, SUBLANES, tm, d, pitch=2 * SUBLANES)
    return h + (y0 * gate[:, 0:1] + y1 * gate[:, 1:2])


def _moe_finish_kernel(y_ref, gate_ref, h_ref, g_ref, o_ref):
    out = _gated_sum(y_ref, gate_ref, h_ref[...])
    ms = jnp.mean(out * out, axis=-1, keepdims=True)
    o_ref[...] = out * lax.rsqrt(ms + EPS) * g_ref[...]


def moe_finish(ys, gate, h, g):
    n, d = h.shape
    tm = TOK_TM
    row_blk = lambda width: pl.BlockSpec((tm, width), lambda i: (i, 0))
    return pl.pallas_call(
        _moe_finish_kernel,
        out_shape=jax.ShapeDtypeStruct((n, d), F32),
        grid=(n // tm,),
        in_specs=[pl.BlockSpec((2 * tm * SUBLANES, LANES), lambda i: (i, 0)),
                  row_blk(LANES), row_blk(d), pl.BlockSpec((1, d), lambda i: (0, 0))],
        out_specs=row_blk(d),
        compiler_params=_cparams("arbitrary"),
        name="moe_finish",
    )(ys, gate, h, g.reshape(1, d))


def dispatch_plan(route, counts):
    m = MOE_BLOCK
    n = route.shape[0]
    assert (2 * n) % IDX_CHUNK == 0
    counts = counts[0, :N_EXPERTS]
    padded = ((counts + m - 1) // m) * m
    pend = jnp.cumsum(padded)
    pstart = pend - padded
    start = jnp.cumsum(counts) - counts
    eid = route[:, ROUTE_EID:ROUTE_EID + 2].reshape(-1)
    order = jnp.argsort(eid, stable=True).astype(I32)
    order = jnp.concatenate([order, jnp.zeros((2 * IDX_CHUNK,), I32)])
    nblk = (2 * n + N_EXPERTS * m) // m
    blk_start = jnp.arange(nblk, dtype=I32) * m
    nused = pend[-1] // m
    used = jnp.arange(nblk) < nused
    e = jnp.minimum(jnp.sum(blk_start[:, None] >= pend[None, :], axis=1), N_EXPERTS - 1)
    is_e = e[:, None] == jnp.arange(N_EXPERTS)[None, :]
    of_block = lambda table: jnp.sum(jnp.where(is_e, table[None, :], 0), axis=1)
    r0 = blk_start - of_block(pstart)
    q0s = jnp.where(used, of_block(start) + r0, 0)
    nvalids = jnp.where(used, jnp.clip(of_block(counts) - r0, 0, m), 0)
    last_exp = jnp.sum(jnp.where(jnp.arange(nblk) == nused - 1, e, 0))
    blk_exp = jnp.where(used, e, last_exp)
    as_i32 = lambda t: t.astype(I32)
    return order, (as_i32(blk_exp), as_i32(q0s), as_i32(nvalids), as_i32(nused).reshape(1))


def kernel(x, attn_norm_g, ffn_norm_g, w_in_ab, w_out_ab, ret_gn_g, w_in_c, b_in_c, sinks, w_out_c, b_out_c,
           w_group, w_expert_router, w_gate, w_up, w_down, final_norm_g):
    batch, seq, d = x.shape
    n = batch * seq
    depth = attn_norm_g.shape[0]
    cos, sin = rope_lane_tables(seq)
    later = (jnp.arange(min(SB_TILE, seq))[:, None] > jnp.arange(min(SB_TILE, seq))[None, :]).astype(BF16)
    w_sb = N_HEADS_SB * HEAD_DIM
    w_ret = N_HEADS_RET * HEAD_DIM
    h = x.reshape(n, d)
    pending = None
    for layer in range(depth):
        i = layer // 2
        if layer % 2 == 0:
            proj, h = norm_proj(h, attn_norm_g[layer], w_in_ab[i].astype(BF16),
                                jnp.zeros((w_in_ab.shape[2],), F32), cos, sin, seq=seq, tn=w_sb,
                                col_ops=("scale", "", "", "rope", "rope scale", "", ""), moe=pending)
            per = w_sb // LANES
            a = sb_attention(proj, later, batch=batch, seq=seq, q_blk=0, k_blk=per, v_blk=2 * per)
            r = retention(proj, ret_gn_g[i], batch=batch, seq=seq,
                          q_blk=3 * per, k_blk=4 * per, v_blk=5 * per, g_blk=6 * per)
            w_out = w_out_ab[i].astype(BF16)
            lhs, ws = [a, r], [w_out[:w_sb], w_out[w_sb:]]
            bias = jnp.zeros((d,), F32)
        else:
            wkv = N_KV_SWA * HEAD_DIM
            nq = N_HEADS_SWA * HEAD_DIM // wkv
            proj, h = norm_proj(h, attn_norm_g[layer], w_in_c[i].astype(BF16), b_in_c[i].astype(F32),
                                cos, sin, seq=seq, tn=wkv,
                                col_ops=("rope scale",) * nq + ("rope", ""), moe=pending)
            o = swa_attention(proj, sinks[i], batch=batch, seq=seq)
            lhs, ws = [o], [w_out_c[i].astype(BF16)]
            bias = b_out_c[i].astype(F32)
        h, xn, route, gate, counts = outproj_router(
            lhs, ws, bias, h, ffn_norm_g[layer], router_weights(w_group[layer], w_expert_router[layer]))
        order, plan = dispatch_plan(route, counts)
        pending = (moe_experts(order, plan, xn, w_gate, w_up, w_down, layer), gate)
    return moe_finish(*pending, h, final_norm_g).reshape(batch, seq, d)
```

```python
import functools

import jax
import jax.numpy as jnp
import numpy as np
from jax import lax
from jax.experimental import pallas as pl
from jax.experimental.pallas import tpu as pltpu

F32 = jnp.float32
BF16 = jnp.bfloat16
I32 = jnp.int32

HEAD_DIM = 64
N_HEADS_SB = 8
N_HEADS_RET = 8
N_HEADS_SWA = 16
N_KV_SWA = 4
WINDOW = 128
ROPE_THETA = 10000.0
N_GROUPS = 4
EXPERTS_PER_GROUP = 8
N_EXPERTS = N_GROUPS * EXPERTS_PER_GROUP
MOE_BLOCK = 256
EPS = 1e-6

LANES = 128
SUBLANES = 8
HALF = HEAD_DIM // 2
QK_SCALE = HEAD_DIM ** -0.5

SB_TILE = 256
RET_CHUNK = 256
PROJ_TM = 512
TOK_TM = 512
ROUTER_TM = 1024
ROUTER_SUB = 256
VMEM_LIMIT = 48 * 1024 * 1024

_NT = (((1,), (1,)), ((), ()))
_TN = (((0,), (0,)), ((), ()))


def _cparams(*sem):
    return pltpu.CompilerParams(dimension_semantics=sem, vmem_limit_bytes=VMEM_LIMIT)


def _head0_mask():
    return lax.broadcasted_iota(I32, (1, LANES), 1) < HEAD_DIM


def _tile_store(ref, first_row, n_tokens, value):
    for c in range(value.shape[1] // LANES):
        ref[pl.ds(first_row + c, n_tokens, stride=SUBLANES), :] = value[:, c * LANES:(c + 1) * LANES]


def _tile_load(ref, first_row, n_tokens, width, pitch=SUBLANES):
    return jnp.concatenate([ref[pl.ds(first_row + c, n_tokens, stride=pitch), :]
                            for c in range(width // LANES)], axis=1)


def _norm_proj_kernel(*refs, tn, col_ops, combine):
    if combine:
        y_ref, gate_ref, x_ref, g_ref, w_ref, b_ref, cos_ref, sin_ref, h_ref, o_ref = refs
        x = _gated_sum(y_ref, gate_ref, x_ref[...])
        h_ref[...] = x
    else:
        x_ref, g_ref, w_ref, b_ref, cos_ref, sin_ref, o_ref = refs
        x = x_ref[...]
    ms = jnp.mean(x * x, axis=-1, keepdims=True)
    xn = (x * lax.rsqrt(ms + EPS) * g_ref[...]).astype(BF16)
    lane = lax.broadcasted_iota(I32, (1, LANES), 1)
    first_half = (lane % HEAD_DIM) < HALF
    for j, op in enumerate(col_ops):
        cols = slice(j * tn, (j + 1) * tn)
        acc = jnp.dot(xn, w_ref[:, cols], preferred_element_type=F32) + b_ref[:, cols]
        if "rope" in op:
            cos = cos_ref[...]
            sin = sin_ref[...]
            slabs = []
            for s in range(tn // LANES):
                a = acc[:, s * LANES:(s + 1) * LANES]
                partner = jnp.where(first_half, pltpu.roll(a, LANES - HALF, 1), pltpu.roll(a, HALF, 1))
                slabs.append(a * cos + partner * sin)
            acc = jnp.concatenate(slabs, axis=1)
        if "scale" in op:
            acc = acc * QK_SCALE
        o_ref[:, cols] = acc.astype(BF16)


def norm_proj(x, g, w, b, cos, sin, *, seq, tn, col_ops, moe=None):
    n, d = x.shape
    f = w.shape[1]
    tm = min(PROJ_TM, seq)
    assert n % tm == 0 and seq % tm == 0 and f == tn * len(col_ops)
    pos_blocks = seq // tm
    row_blk = lambda width: pl.BlockSpec((tm, width), lambda i: (i, 0))
    in_specs = [
        row_blk(d),
        pl.BlockSpec((1, d), lambda i: (0, 0)),
        pl.BlockSpec((d, f), lambda i: (0, 0)),
        pl.BlockSpec((1, f), lambda i: (0, 0)),
        pl.BlockSpec((tm, LANES), lambda i: (i % pos_blocks, 0)),
        pl.BlockSpec((tm, LANES), lambda i: (i % pos_blocks, 0)),
    ]
    args = (x, g.reshape(1, d), w, b.reshape(1, f), cos, sin)
    proj_shape, proj_spec = jax.ShapeDtypeStruct((n, f), BF16), row_blk(f)
    call = functools.partial(
        pl.pallas_call, functools.partial(_norm_proj_kernel, tn=tn, col_ops=col_ops, combine=moe is not None),
        grid=(n // tm,), compiler_params=_cparams("arbitrary"), name="norm_proj")
    if moe is None:
        return call(out_shape=proj_shape, in_specs=in_specs, out_specs=proj_spec)(*args), x
    ys, gate = moe
    pair_tiles = pl.BlockSpec((2 * tm * SUBLANES, LANES), lambda i: (i, 0))
    h, proj = call(out_shape=(jax.ShapeDtypeStruct((n, d), F32), proj_shape),
                   in_specs=[pair_tiles, row_blk(LANES)] + in_specs,
                   out_specs=(row_blk(d), proj_spec))(ys, gate, *args)
    return proj, h


def rope_lane_tables(seq):
    pos = jnp.arange(seq, dtype=F32)
    inv = ROPE_THETA ** (-jnp.arange(0, HEAD_DIM, 2, dtype=F32) / HEAD_DIM)
    ang = pos[:, None] * inv[None, :]
    cos, sin = jnp.cos(ang), jnp.sin(ang)
    return jnp.tile(cos, (1, 4)), jnp.tile(jnp.concatenate([-sin, sin], axis=1), (1, 2))


SB_PAIRS = 4


def _sb_kernel(q_ref, k_ref, v_ref, t_ref, o_ref, acc_ref, carry_ref, *, tile):
    qi = pl.program_id(2)
    m0 = _head0_mask()
    zero = jnp.zeros((tile, LANES), BF16)
    row = lax.broadcasted_iota(I32, (tile, tile), 0)
    col = lax.broadcasted_iota(I32, (tile, tile), 1)
    strict = col < row

    def key_tile(kb, diag):
        start = pl.multiple_of(kb * tile, tile)
        for p in range(SB_PAIRS):
            lanes = slice(p * LANES, (p + 1) * LANES)
            q = q_ref[:, lanes]
            k = k_ref[pl.ds(start, tile), lanes]
            v = v_ref[pl.ds(start, tile), lanes]
            out = None
            for h in range(2):
                qh = jnp.where(m0, q, zero) if h == 0 else jnp.where(m0, zero, q)
                vh = jnp.where(m0, v, zero) if h == 0 else jnp.where(m0, zero, v)
                z = lax.dot_general(qh, k, _NT, preferred_element_type=F32)
                log_fail = -(jnp.maximum(z, 0.0) + jnp.log(1.0 + jnp.exp(-jnp.abs(z))))
                if diag:
                    log_fail = jnp.where(strict, log_fail, 0.0)
                after = jnp.dot(log_fail.astype(BF16), t_ref[...], preferred_element_type=F32)
                logw = z + log_fail + after
                if not diag:
                    logw = logw + carry_ref[2 * p + h]
                w = jnp.exp(logw)
                if diag:
                    w = jnp.where(strict, w, 0.0)
                pv = jnp.dot(w.astype(BF16), vh, preferred_element_type=F32)
                out = pv if out is None else out + pv
                tile_sum = jnp.sum(log_fail, axis=-1, keepdims=True)
                if diag:
                    carry_ref[2 * p + h] = tile_sum
                else:
                    carry_ref[2 * p + h] += tile_sum
            if diag:
                acc_ref[:, lanes] = out
            else:
                acc_ref[:, lanes] += out

    key_tile(qi, True)

    def body(j, c):
        key_tile(qi - 1 - j, False)
        return c

    lax.fori_loop(0, qi, body, 0)
    o_ref[...] = acc_ref[...].astype(BF16)


def sb_attention(proj, later_mat, *, batch, seq, q_blk, k_blk, v_blk):
    n = proj.shape[0]
    tile = min(SB_TILE, seq)
    nq = seq // tile
    width = SB_PAIRS * LANES
    steps = N_HEADS_SB * HEAD_DIM // width
    assert q_blk % SB_PAIRS == 0 and k_blk % SB_PAIRS == 0 and v_blk % SB_PAIRS == 0
    col = lambda blk: blk // SB_PAIRS
    return pl.pallas_call(
        functools.partial(_sb_kernel, tile=tile),
        out_shape=jax.ShapeDtypeStruct((n, steps * width), BF16),
        grid=(batch, steps, nq),
        in_specs=[
            pl.BlockSpec((tile, width), lambda b, p, i: (b * nq + i, col(q_blk) + p)),
            pl.BlockSpec((seq, width), lambda b, p, i: (b, col(k_blk) + p)),
            pl.BlockSpec((seq, width), lambda b, p, i: (b, col(v_blk) + p)),
            pl.BlockSpec((tile, tile), lambda b, p, i: (0, 0)),
        ],
        out_specs=pl.BlockSpec((tile, width), lambda b, p, i: (b * nq + i, p)),
        scratch_shapes=[pltpu.VMEM((tile, width), F32), pltpu.VMEM((2 * SB_PAIRS, tile, 1), F32)],
        compiler_params=_cparams("arbitrary", "arbitrary", "arbitrary"),
        name="sb_attention",
    )(proj, proj, proj, later_mat)


RET_PAIRS = 4
RET_STEP_CHUNKS = 2


def _retention_kernel(q_ref, k_ref, v_ref, gate_ref, gain_ref, dec_ref, xi_ref, zeta_ref, gch_ref,
                      o_ref, state_ref):
    c = pl.program_id(2)

    @pl.when(c == 0)
    def _():
        state_ref[...] = jnp.zeros_like(state_ref)

    m0 = _head0_mask()
    chunk = dec_ref.shape[1]
    zero = jnp.zeros((chunk, LANES), BF16)
    r = lax.broadcasted_iota(I32, (LANES, LANES), 0) < HEAD_DIM
    cc = lax.broadcasted_iota(I32, (LANES, LANES), 1) < HEAD_DIM
    same_head = r == cc

    def head_mean(t):
        s0 = jnp.sum(jnp.where(m0, t, 0.0), axis=-1, keepdims=True)
        s1 = jnp.sum(jnp.where(m0, 0.0, t), axis=-1, keepdims=True)
        return jnp.where(m0, s0, s1) * (1.0 / HEAD_DIM)

    for p in range(RET_PAIRS):
        lanes = slice(p * LANES, (p + 1) * LANES)
        state = state_ref[p]
        for sub in range(q_ref.shape[0] // chunk):
            rows = slice(sub * chunk, (sub + 1) * chunk)
            q = q_ref[rows, lanes]
            k = k_ref[rows, lanes]
            v = v_ref[rows, lanes]
            y = jnp.dot(q, state.astype(BF16), preferred_element_type=F32) * xi_ref[:, lanes]
            for h in range(2):
                qh = jnp.where(m0, q, zero) if h == 0 else jnp.where(m0, zero, q)
                vh = jnp.where(m0, v, zero) if h == 0 else jnp.where(m0, zero, v)
                s = lax.dot_general(qh, k, _NT, preferred_element_type=F32)
                inner = (s * dec_ref[2 * p + h]).astype(BF16)
                y = y + jnp.dot(inner, vh, preferred_element_type=F32)

            kz = (k.astype(F32) * zeta_ref[:, lanes]).astype(BF16)
            upd = lax.dot_general(kz, v, _TN, preferred_element_type=F32)
            state = state * gch_ref[:, lanes] + jnp.where(same_head, upd, 0.0)

            d = y - head_mean(y)
            yn = d * lax.rsqrt(head_mean(d * d) + EPS) * gain_ref[:, lanes]
            g = gate_ref[rows, lanes].astype(F32)
            o_ref[rows, lanes] = (yn * (g * (1.0 / (1.0 + jnp.exp(-g))))).astype(BF16)
        state_ref[p] = state


def retention_tables(chunk):
    h = N_HEADS_RET
    log_g = jnp.log(1.0 - 2.0 ** (-5.0 - jnp.arange(h, dtype=F32)))
    idx = jnp.arange(chunk, dtype=F32)
    diff = idx[:, None] - idx[None, :]
    dec = jnp.where(diff[None] >= 0, jnp.exp(log_g[:, None, None] * jnp.maximum(diff, 0.0)[None]), 0.0)
    xi = jnp.exp(log_g[:, None] * (idx[None, :] + 1.0))
    zeta = jnp.exp(log_g[:, None] * (chunk - 1.0 - idx[None, :]))
    gch = jnp.exp(log_g * chunk)
    lanes = lambda t: jnp.repeat(t.T, HEAD_DIM, axis=1)
    return dec, lanes(xi), lanes(zeta), jnp.repeat(gch, HEAD_DIM)[None, :]


def retention(proj, gain, *, batch, seq, q_blk, k_blk, v_blk, g_blk):
    n = proj.shape[0]
    chunk = min(RET_CHUNK, seq)
    rows = min(RET_STEP_CHUNKS * chunk, seq)
    nc = seq // rows
    width = RET_PAIRS * LANES
    steps = N_HEADS_RET * HEAD_DIM // width
    assert all(b % RET_PAIRS == 0 for b in (q_blk, k_blk, v_blk, g_blk))
    dec, xi, zeta, gch = retention_tables(chunk)
    blk = lambda off: pl.BlockSpec((rows, width), lambda b, p, c: (b * nc + c, off // RET_PAIRS + p))
    return pl.pallas_call(
        _retention_kernel,
        out_shape=jax.ShapeDtypeStruct((n, steps * width), BF16),
        grid=(batch, steps, nc),
        in_specs=[
            blk(q_blk), blk(k_blk), blk(v_blk), blk(g_blk),
            pl.BlockSpec((1, width), lambda b, p, c: (0, p)),
            pl.BlockSpec((2 * RET_PAIRS, chunk, chunk), lambda b, p, c: (p, 0, 0)),
            pl.BlockSpec((chunk, width), lambda b, p, c: (0, p)),
            pl.BlockSpec((chunk, width), lambda b, p, c: (0, p)),
            pl.BlockSpec((1, width), lambda b, p, c: (0, p)),
        ],
        out_specs=pl.BlockSpec((rows, width), lambda b, p, c: (b * nc + c, p)),
        scratch_shapes=[pltpu.VMEM((RET_PAIRS, LANES, LANES), F32)],
        compiler_params=_cparams("arbitrary", "arbitrary", "arbitrary"),
        name="retention",
    )(proj, proj, proj, proj, gain.reshape(1, -1).astype(F32), dec, xi, zeta, gch)


SWA_BLOCKS = 8


def _swa_kernel(sink_ref, q_ref, kp_ref, kc_ref, vp_ref, vc_ref, o_ref):
    w = WINDOW
    qi = lax.broadcasted_iota(I32, (w, 2 * w), 0)
    kj = lax.broadcasted_iota(I32, (w, 2 * w), 1)
    rel = qi + w - kj
    in_window = (rel >= 0) & (rel < w)
    keys = jnp.concatenate([kp_ref[...], kc_ref[...]], axis=0)
    vals = jnp.concatenate([vp_ref[...], vc_ref[...]], axis=0)
    for sub in range(q_ref.shape[0] // w):
        first = (pl.program_id(1) == 0) if sub == 0 else False
        valid = in_window & (jnp.logical_not(first) | (kj >= w))
        _swa_block(sink_ref, q_ref, o_ref, slice(sub * w, (sub + 1) * w),
                   keys[sub * w:(sub + 2) * w], vals[sub * w:(sub + 2) * w], valid)


def _swa_block(sink_ref, q_ref, o_ref, rows, kband, vband, valid):
    m0 = _head0_mask()
    w = WINDOW
    group = N_HEADS_SWA // N_KV_SWA
    slabs = group * HEAD_DIM // LANES
    swap = lambda t: jnp.concatenate([t[:, HEAD_DIM:], t[:, :HEAD_DIM]], axis=1)
    zero_kv = jnp.zeros((2 * w, LANES), BF16)
    zero_q = jnp.zeros((slabs * w, LANES), BF16)
    ones_kv = jnp.ones((2 * w, LANES), BF16)

    def half_heads(q_stack, k_half, v_half, first_head):
        s = lax.dot_general(q_stack, k_half, _NT, preferred_element_type=F32)
        probs, sink_terms = [], []
        for slab in range(slabs):
            sink = sink_ref[first_head + 2 * slab]
            sh = jnp.where(valid, s[slab * w:(slab + 1) * w], -jnp.inf)
            m = jnp.maximum(jnp.max(sh, axis=-1, keepdims=True), sink)
            probs.append(jnp.exp(sh - m).astype(BF16))
            sink_terms.append(jnp.exp(sink - jnp.broadcast_to(m, (w, LANES))))
        p = jnp.concatenate(probs, axis=0)
        pv = jnp.dot(p, jnp.concatenate([v_half, ones_kv], axis=1), preferred_element_type=F32)
        denom = pv[:, LANES:] + jnp.concatenate(sink_terms, axis=0)
        return pv[:, :LANES] / denom

    for pair in range(N_KV_SWA * HEAD_DIM // LANES):
        k2 = kband[:, pair * LANES:(pair + 1) * LANES]
        v2 = vband[:, pair * LANES:(pair + 1) * LANES]
        k2s, v2s = swap(k2), swap(v2)
        for c in range(2):
            kvh = 2 * pair + c
            k_even, k_odd = (k2, k2s) if c == 0 else (k2s, k2)
            v_even, v_odd = (v2, v2s) if c == 0 else (v2s, v2)
            base = kvh * group * HEAD_DIM
            q_stack = jnp.concatenate([q_ref[rows, base + t * LANES:base + (t + 1) * LANES] for t in range(slabs)],
                                      axis=0)
            out = (half_heads(jnp.where(m0, q_stack, zero_q), k_even, jnp.where(m0, v_even, zero_kv), kvh * group)
                   + half_heads(jnp.where(m0, zero_q, q_stack), k_odd, jnp.where(m0, zero_kv, v_odd),
                                kvh * group + 1))
            for t in range(slabs):
                o_ref[rows, base + t * LANES:base + (t + 1) * LANES] = out[t * w:(t + 1) * w].astype(BF16)


def swa_attention(proj, sinks, *, batch, seq):
    n = proj.shape[0]
    w = WINDOW
    nb = seq // w
    wq = N_HEADS_SWA * HEAD_DIM
    wkv = N_KV_SWA * HEAD_DIM
    k_blk = wq // wkv
    blocks = min(SWA_BLOCKS, nb)
    assert nb % blocks == 0
    steps = nb // blocks
    rows = blocks * w
    cur = lambda off: pl.BlockSpec((rows, wkv), lambda b, i: (b * steps + i, off))
    prev = lambda off: pl.BlockSpec((w, wkv), lambda b, i: (b * nb + jnp.maximum(blocks * i - 1, 0), off))
    return pl.pallas_call(
        _swa_kernel,
        out_shape=jax.ShapeDtypeStruct((n, wq), BF16),
        grid=(batch, steps),
        in_specs=[
            pl.BlockSpec(memory_space=pltpu.SMEM),
            pl.BlockSpec((rows, wq), lambda b, i: (b * steps + i, 0)),
            prev(k_blk), cur(k_blk), prev(k_blk + 1), cur(k_blk + 1),
        ],
        out_specs=pl.BlockSpec((rows, wq), lambda b, i: (b * steps + i, 0)),
        compiler_params=_cparams("arbitrary", "arbitrary"),
        name="swa_attention",
    )(sinks.astype(F32), proj, proj, proj, proj, proj)


ROUTE_EID = 0
GROUP_LANE = N_EXPERTS


def _split_bf16(t):
    hi = t.astype(BF16)
    return hi, (t - hi.astype(F32)).astype(BF16)


def _outproj_router_kernel(*refs, n_lhs):
    lhs = refs[:n_lhs]
    ws = refs[n_lhs:2 * n_lhs]
    b_ref, h_ref, g_ref, wr_ref = refs[2 * n_lhs:2 * n_lhs + 4]
    h1_ref, xn_ref, route_ref, gate_ref, cnt_ref, carry_ref = refs[2 * n_lhs + 4:]
    step = pl.program_id(0)

    @pl.when(step == 0)
    def _():
        carry_ref[...] = jnp.zeros_like(carry_ref)

    w_split = jnp.concatenate(_split_bf16(wr_ref[...]), axis=1)
    for sub in range(h_ref.shape[0] // ROUTER_SUB):
        rows = slice(sub * ROUTER_SUB, (sub + 1) * ROUTER_SUB)
        mix = b_ref[...]
        for a_ref, w_ref in zip(lhs, ws):
            mix = mix + jnp.dot(a_ref[rows], w_ref[...], preferred_element_type=F32)
        h1 = h_ref[rows] + mix
        h1_ref[rows] = h1
        ms = jnp.mean(h1 * h1, axis=-1, keepdims=True)
        xn = h1 * lax.rsqrt(ms + EPS) * g_ref[...]
        _tile_store(xn_ref, sub * ROUTER_SUB * SUBLANES, ROUTER_SUB, xn)
        x_hi, x_lo = _split_bf16(xn)
        parts = (jnp.dot(x_hi, w_split, preferred_element_type=F32)
                 + jnp.dot(x_lo, w_split, preferred_element_type=F32))
        logits = parts[:, :LANES] + parts[:, LANES:]
        route, gate, cnt = _route(logits)
        route_ref[rows] = route
        gate_ref[rows] = gate
        carry_ref[...] += cnt
    cnt_ref[...] = jnp.broadcast_to(carry_ref[...], cnt_ref.shape).astype(I32)


def _route(logits):
    tm = logits.shape[0]
    lane = lax.broadcasted_iota(I32, (tm, LANES), 1)
    lane_f = lane.astype(F32)
    neg = -jnp.inf

    def lane_max(t):
        return jnp.max(t, axis=-1, keepdims=True)

    def lane_sum(t):
        return jnp.sum(t, axis=-1, keepdims=True)

    def first_lane_of(t, value, mask):
        return jnp.min(jnp.where(mask & (t == value), lane_f, float(LANES)), axis=-1, keepdims=True)

    is_group = (lane >= GROUP_LANE) & (lane < GROUP_LANE + N_GROUPS)
    gl = jnp.where(is_group, logits, neg)
    g_max = lane_max(gl)
    g_prob = 1.0 / lane_sum(jnp.exp(gl - g_max))
    g_idx = first_lane_of(gl, g_max, is_group) - float(GROUP_LANE)
    group_of_lane = lax.shift_right_logical(lane, int(np.log2(EXPERTS_PER_GROUP))).astype(F32)
    in_group = (lane < N_EXPERTS) & (group_of_lane == g_idx)
    el = jnp.where(in_group, logits, neg)
    l1 = lane_max(el)
    i1 = first_lane_of(el, l1, in_group)
    rest = in_group & (lane_f != i1)
    el2 = jnp.where(rest, el, neg)
    l2 = lane_max(el2)
    i2 = first_lane_of(el2, l2, rest)
    e2 = jnp.exp(l2 - l1)
    gate1 = g_prob / (1.0 + e2)
    gate2 = g_prob * (e2 / (1.0 + e2))

    cnt = jnp.where((lane_f == i1) | (lane_f == i2), 1.0, 0.0)
    route = jnp.where(lane == ROUTE_EID, i1, 0.0)
    route = jnp.where(lane == ROUTE_EID + 1, i2, route)
    gate = jnp.where(lane == 0, gate1, jnp.where(lane == 1, gate2, 0.0))
    return route.astype(I32), gate, jnp.sum(cnt, axis=0, keepdims=True)


def outproj_router(lhs, ws, bias, h, g, w_router):
    n, d = h.shape
    tm = ROUTER_TM
    assert n % tm == 0
    row_blk = lambda width: pl.BlockSpec((tm, width), lambda i: (i, 0))
    full = lambda a: pl.BlockSpec(a.shape, lambda i: (0, 0))
    bias2, g2 = bias.reshape(1, d), g.reshape(1, d)
    args = [*lhs, *ws, bias2, h, g2, w_router]
    in_specs = ([row_blk(a.shape[1]) for a in lhs] + [full(w) for w in ws]
                + [full(bias2), row_blk(d), full(g2), full(w_router)])
    return pl.pallas_call(
        functools.partial(_outproj_router_kernel, n_lhs=len(lhs)),
        out_shape=(
            jax.ShapeDtypeStruct((n, d), F32), jax.ShapeDtypeStruct((n * d // LANES, LANES), F32),
            jax.ShapeDtypeStruct((n, LANES), I32), jax.ShapeDtypeStruct((n, LANES), F32),
            jax.ShapeDtypeStruct((8, LANES), I32),
        ),
        grid=(n // tm,),
        in_specs=in_specs,
        out_specs=(row_blk(d), pl.BlockSpec((tm * d // LANES, LANES), lambda i: (i, 0)),
                   row_blk(LANES), row_blk(LANES), pl.BlockSpec((8, LANES), lambda i: (0, 0))),
        scratch_shapes=[pltpu.VMEM((1, LANES), F32)],
        compiler_params=_cparams("arbitrary"),
        name="outproj_router",
    )(*args)


def router_weights(w_group, w_expert_router):
    d = w_group.shape[0]
    pad = jnp.zeros((d, LANES - N_EXPERTS - N_GROUPS), F32)
    return jnp.concatenate([w_expert_router.astype(F32), w_group.astype(F32), pad], axis=1)


IDX_CHUNK = 1024
IDX_RING = 4
OUT_RING = 4
DMA_UNROLL = 8


def _token_rows(ref, token):
    start = token * SUBLANES
    if not isinstance(start, int):
        start = pl.multiple_of(start, SUBLANES)
    return ref.at[pl.ds(start, SUBLANES), :]


def _token_copy(src, src_token, dst, dst_token, sem):
    return pltpu.make_async_copy(_token_rows(src, src_token), _token_rows(dst, dst_token), sem)


def _moe_kernel(blk_exp, q0s, nvalids, nused, order_hbm, xn_hbm, wg_ref, wu_ref, wd_ref, out_hbm,
                ibuf, xbuf, ybuf, wg_bf, wu_bf, wd_bf, isem, gsem, ssem):
    i = pl.program_id(0)
    nu = nused[0]
    m = MOE_BLOCK
    d = wd_bf.shape[1]

    window = 2 * IDX_CHUNK

    def idx_copy(blk):
        base = pl.multiple_of(q0s[blk] & ~(IDX_CHUNK - 1), IDX_CHUNK)
        slot = blk & (IDX_RING - 1)
        dst = ibuf.at[pl.ds(pl.multiple_of(slot * window, window), window)]
        return pltpu.make_async_copy(order_hbm.at[pl.ds(base, window)], dst, isem.at[slot])

    def pair_base(blk):
        return (blk & (IDX_RING - 1)) * window + (q0s[blk] & (IDX_CHUNK - 1))

    def gather_group(blk_base, dst, sem, g):
        for u in range(DMA_UNROLL):
            r = g * DMA_UNROLL + u
            pair = ibuf[blk_base + r]
            first_row = pl.multiple_of((pair & ~1) * (SUBLANES // 2), SUBLANES)
            pltpu.make_async_copy(xn_hbm.at[pl.ds(first_row, SUBLANES), :], _token_rows(dst, r), sem).start()

    def scatter_group(blk_base, src, sem, g):
        for u in range(DMA_UNROLL):
            r = g * DMA_UNROLL + u
            _token_copy(src, r, out_hbm, ibuf[blk_base + r], sem).start()

    def issue_gathers(blk, unrolled=False):
        slot = blk & 1
        args = (pair_base(blk), xbuf.at[slot], gsem.at[slot])
        if unrolled:
            for g in range(m // DMA_UNROLL):
                gather_group(*args, g)
        else:
            lax.fori_loop(0, m // DMA_UNROLL, lambda g, c: (gather_group(*args, g), c)[1], 0)

    def wait_gathers(blk):
        slot = blk & 1
        pltpu.make_async_copy(xn_hbm.at[pl.ds(0, m * SUBLANES), :], xbuf.at[slot], gsem.at[slot]).wait()

    def issue_scatters(blk, unrolled=False):
        slot = blk & (OUT_RING - 1)
        base = pair_base(blk)
        src = ybuf.at[slot]
        sem = ssem.at[slot]
        if unrolled:
            for g in range(m // DMA_UNROLL):
                scatter_group(base, src, sem, g)
            return
        nv = nvalids[blk]
        groups = lax.shift_right_logical(nv, DMA_UNROLL.bit_length() - 1)

        def tail(r, c):
            _token_copy(src, r, out_hbm, ibuf[base + r], sem).start()
            return c

        lax.fori_loop(0, groups, lambda g, c: (scatter_group(base, src, sem, g), c)[1], 0)
        lax.fori_loop(groups * DMA_UNROLL, nv, tail, 0)

    def wait_scatters(blk):
        slot = blk & (OUT_RING - 1)
        nv = nvalids[blk]
        rows = pl.multiple_of(nv * SUBLANES, SUBLANES)

        @pl.when(nv > 0)
        def _():
            pltpu.make_async_copy(ybuf.at[slot].at[pl.ds(0, rows), :], out_hbm.at[pl.ds(0, rows), :],
                                  ssem.at[slot]).wait()

    @pl.when(i == 0)
    def _():
        first = idx_copy(0)
        first.start()
        first.wait()
        issue_gathers(0)

        @pl.when(nu > 1)
        def _():
            idx_copy(1).start()

    def expert():
        x = _tile_load(xbuf.at[i & 1], 0, m, d).astype(BF16)
        gate = jnp.dot(x, wg_bf[...], preferred_element_type=F32)
        up = jnp.dot(x, wu_bf[...], preferred_element_type=F32)
        hidden = (gate * (1.0 / (1.0 + jnp.exp(-gate))) * up).astype(BF16)
        _tile_store(ybuf.at[i & (OUT_RING - 1)], 0, m, jnp.dot(hidden, wd_bf[...], preferred_element_type=F32))

    @pl.when(i < nu)
    def _():
        @pl.when(i + 2 < nu)
        def _():
            idx_copy(i + 2).start()

        @pl.when(i + 1 < nu)
        def _():
            idx_copy(i + 1).wait()

        wait_gathers(i)

        @pl.when((i == 0) | (blk_exp[i] != blk_exp[jnp.maximum(i - 1, 0)]))
        def _():
            wg_bf[...] = wg_ref[0, 0].astype(BF16)
            wu_bf[...] = wu_ref[0, 0].astype(BF16)
            wd_bf[...] = wd_ref[0, 0].astype(BF16)

        @pl.when(i >= OUT_RING - 1)
        def _():
            wait_scatters(i - (OUT_RING - 1))

        prev = jnp.maximum(i - 1, 0)
        steady = (i >= 1) & (i + 1 < nu) & (nvalids[prev] == m)

        @pl.when(steady)
        def _():
            issue_gathers(i + 1, unrolled=True)
            issue_scatters(i - 1, unrolled=True)
            expert()

        @pl.when(jnp.logical_not(steady))
        def _():
            @pl.when(i + 1 < nu)
            def _():
                issue_gathers(i + 1)

            @pl.when(i >= 1)
            def _():
                issue_scatters(i - 1)

            expert()

        @pl.when(i == nu - 1)
        def _():
            issue_scatters(i)
            for back in range(OUT_RING - 2, -1, -1):
                @pl.when(i >= back)
                def _():
                    wait_scatters(i - back)


def moe_experts(order, plan, xn, w_gate, w_up, w_down, layer):
    blk_exp, q0s, nvalids, nused = plan
    d, de = w_gate.shape[2], w_gate.shape[3]
    assert d == SUBLANES * LANES
    rows = xn.shape[0]
    m = MOE_BLOCK
    w_in_spec = pl.BlockSpec((1, 1, d, de), lambda i, be, q0, nv, nu: (layer, be[i], 0, 0))
    return pl.pallas_call(
        _moe_kernel,
        out_shape=jax.ShapeDtypeStruct((2 * rows, LANES), F32),
        grid_spec=pltpu.PrefetchScalarGridSpec(
            num_scalar_prefetch=4,
            grid=(blk_exp.shape[0],),
            in_specs=[
                pl.BlockSpec(memory_space=pl.ANY), pl.BlockSpec(memory_space=pl.ANY),
                w_in_spec, w_in_spec,
                pl.BlockSpec((1, 1, de, d), lambda i, be, q0, nv, nu: (layer, be[i], 0, 0)),
            ],
            out_specs=pl.BlockSpec(memory_space=pl.ANY),
            scratch_shapes=[
                pltpu.SMEM((IDX_RING * 2 * IDX_CHUNK,), I32),
                pltpu.VMEM((2, m * SUBLANES, LANES), F32), pltpu.VMEM((OUT_RING, m * SUBLANES, LANES), F32),
                pltpu.VMEM((d, de), BF16), pltpu.VMEM((d, de), BF16), pltpu.VMEM((de, d), BF16),
                pltpu.SemaphoreType.DMA((IDX_RING,)), pltpu.SemaphoreType.DMA((2,)),
                pltpu.SemaphoreType.DMA((OUT_RING,)),
            ],
        ),
        compiler_params=_cparams("arbitrary"),
        name="moe_experts",
    )(blk_exp, q0s, nvalids, nused, order, xn, w_gate, w_up, w_down)


def _gated_sum(y_ref, gate_ref, h):
    tm, d = h.shape
    gate = gate_ref[...]
    y0 = _tile_load(y_ref, 0, tm, d, pitch=2 * SUBLANES)
    y1 = _tile_load(y_ref, SUBLANES, tm, d, pitch=2 * SUBLANES)
    return h + (y0 * gate[:, 0:1] + y1 * gate[:, 1:2])


FINISH_RING = 3


def _moe_finish_kernel(y_hbm, gate_ref, h_ref, g_ref, o_ref, ybuf, sem):
    i = pl.program_id(0)
    steps = pl.num_programs(0)
    rows = ybuf.shape[1]

    def tile_copy(step):
        slot = lax.rem(step, FINISH_RING)
        src = y_hbm.at[pl.ds(pl.multiple_of(step * rows, rows), rows), :]
        return pltpu.make_async_copy(src, ybuf.at[slot], sem.at[slot])

    @pl.when(i == 0)
    def _():
        for ahead in range(FINISH_RING - 1):
            @pl.when(ahead < steps)
            def _():
                tile_copy(ahead).start()

    @pl.when(i + FINISH_RING - 1 < steps)
    def _():
        tile_copy(i + FINISH_RING - 1).start()

    tile_copy(i).wait()
    out = _gated_sum(ybuf.at[lax.rem(i, FINISH_RING)], gate_ref, h_ref[...])
    ms = jnp.mean(out * out, axis=-1, keepdims=True)
    o_ref[...] = out * lax.rsqrt(ms + EPS) * g_ref[...]


def moe_finish(ys, gate, h, g):
    n, d = h.shape
    tm = TOK_TM
    row_blk = lambda width: pl.BlockSpec((tm, width), lambda i: (i, 0))
    return pl.pallas_call(
        _moe_finish_kernel,
        out_shape=jax.ShapeDtypeStruct((n, d), F32),
        grid=(n // tm,),
        in_specs=[pl.BlockSpec(memory_space=pl.ANY), row_blk(LANES), row_blk(d),
                  pl.BlockSpec((1, d), lambda i: (0, 0))],
        out_specs=row_blk(d),
        scratch_shapes=[pltpu.VMEM((FINISH_RING, 2 * tm * SUBLANES, LANES), F32),
                        pltpu.SemaphoreType.DMA((FINISH_RING,))],
        compiler_params=_cparams("arbitrary"),
        name="moe_finish",
    )(ys, gate, h, g.reshape(1, d))


def dispatch_plan(route, counts):
    m = MOE_BLOCK
    n = route.shape[0]
    assert (2 * n) % IDX_CHUNK == 0
    counts = counts[0, :N_EXPERTS]
    padded = ((counts + m - 1) // m) * m
    pend = jnp.cumsum(padded)
    pstart = pend - padded
    start = jnp.cumsum(counts) - counts
    eid = route[:, ROUTE_EID:ROUTE_EID + 2].reshape(-1)
    order = jnp.argsort(eid, stable=True).astype(I32)
    order = jnp.concatenate([order, jnp.zeros((2 * IDX_CHUNK,), I32)])
    nblk = (2 * n + N_EXPERTS * m) // m
    blk_start = jnp.arange(nblk, dtype=I32) * m
    nused = pend[-1] // m
    used = jnp.arange(nblk) < nused
    e = jnp.minimum(jnp.sum(blk_start[:, None] >= pend[None, :], axis=1), N_EXPERTS - 1)
    is_e = e[:, None] == jnp.arange(N_EXPERTS)[None, :]
    of_block = lambda table: jnp.sum(jnp.where(is_e, table[None, :], 0), axis=1)
    r0 = blk_start - of_block(pstart)
    q0s = jnp.where(used, of_block(start) + r0, 0)
    nvalids = jnp.where(used, jnp.clip(of_block(counts) - r0, 0, m), 0)
    last_exp = jnp.sum(jnp.where(jnp.arange(nblk) == nused - 1, e, 0))
    blk_exp = jnp.where(used, e, last_exp)
    as_i32 = lambda t: t.astype(I32)
    return order, (as_i32(blk_exp), as_i32(q0s), as_i32(nvalids), as_i32(nused).reshape(1))


def kernel(x, attn_norm_g, ffn_norm_g, w_in_ab, w_out_ab, ret_gn_g, w_in_c, b_in_c, sinks, w_out_c, b_out_c,
           w_group, w_expert_router, w_gate, w_up, w_down, final_norm_g):
    batch, seq, d = x.shape
    n = batch * seq
    depth = attn_norm_g.shape[0]
    cos, sin = rope_lane_tables(seq)
    later = (jnp.arange(min(SB_TILE, seq))[:, None] > jnp.arange(min(SB_TILE, seq))[None, :]).astype(BF16)
    w_sb = N_HEADS_SB * HEAD_DIM
    w_ret = N_HEADS_RET * HEAD_DIM
    h = x.reshape(n, d)
    pending = None
    for layer in range(depth):
        i = layer // 2
        if layer % 2 == 0:
            proj, h = norm_proj(h, attn_norm_g[layer], w_in_ab[i].astype(BF16),
                                jnp.zeros((w_in_ab.shape[2],), F32), cos, sin, seq=seq, tn=w_sb,
                                col_ops=("scale", "", "", "rope", "rope scale", "", ""), moe=pending)
            per = w_sb // LANES
            a = sb_attention(proj, later, batch=batch, seq=seq, q_blk=0, k_blk=per, v_blk=2 * per)
            r = retention(proj, ret_gn_g[i], batch=batch, seq=seq,
                          q_blk=3 * per, k_blk=4 * per, v_blk=5 * per, g_blk=6 * per)
            w_out = w_out_ab[i].astype(BF16)
            lhs, ws = [a, r], [w_out[:w_sb], w_out[w_sb:]]
            bias = jnp.zeros((d,), F32)
        else:
            wkv = N_KV_SWA * HEAD_DIM
            nq = N_HEADS_SWA * HEAD_DIM // wkv
            proj, h = norm_proj(h, attn_norm_g[layer], w_in_c[i].astype(BF16), b_in_c[i].astype(F32),
                                cos, sin, seq=seq, tn=wkv,
                                col_ops=("rope scale",) * nq + ("rope", ""), moe=pending)
            o = swa_attention(proj, sinks[i], batch=batch, seq=seq)
            lhs, ws = [o], [w_out_c[i].astype(BF16)]
            bias = b_out_c[i].astype(F32)
        h, xn, route, gate, counts = outproj_router(
            lhs, ws, bias, h, ffn_norm_g[layer], router_weights(w_group[layer], w_expert_router[layer]))
        order, plan = dispatch_plan(route, counts)
        pending = (moe_experts(order, plan, xn, w_gate, w_up, w_down, layer), gate)
    return moe_finish(*pending, h, final_norm_g).reshape(batch, seq, d)
```
